```python
import math
import jax, jax.numpy as jnp
from jax import lax
import numpy as np

D_MODEL = 1024
BATCH = 8
SEQ = 4096
DEPTH = 2

HEAD_DIM = 64
H_A = 4
H_B = 4
H_C = 4
KV_C = 2
H_D = 4
N_MIXERS = 4
D_MIX = HEAD_DIM * (H_A + H_B + H_C + H_D)
N_IN = 3 * HEAD_DIM * H_A + 3 * HEAD_DIM * H_B + HEAD_DIM * (H_C + 2 * KV_C) + 3 * HEAD_DIM * H_D + H_D
DILATED_GROUPS = ((128, 1), (512, 4), (2048, 16))
BAND_BLOCK = 128
MOBA_BLOCK = 256
MOBA_TOPK = 3
MOBA_Q_CHUNK = 128
SWA_WINDOW = 128
FOX_Q_BLOCK = 128
N_BUCKETS = 32
T5_MAX_DISTANCE = 2048
N_BIAS_HEADS = H_A + H_B + H_C
D_FF = -(-(8 * D_MODEL) // (3 * 256)) * 256
EPS = 1e-6
NEG = -1e30
SCALE = HEAD_DIM ** -0.5

kernel_name = 'hybrid_parallel_heads_dilated_moba_swa_fox'


def rmsnorm(x, g):
    xf = x.astype(jnp.float32)
    y = xf * lax.rsqrt(jnp.mean(xf * xf, axis=-1, keepdims=True) + EPS)
    return (y * g.astype(jnp.float32)).astype(x.dtype)


def group_rmsnorm(y, g):
    B, S, _ = y.shape
    yg = y.reshape(B, S, N_MIXERS, D_MIX // N_MIXERS)
    return rmsnorm(yg, g.reshape(N_MIXERS, D_MIX // N_MIXERS)).reshape(B, S, D_MIX)


def t5_bucket(dist):
    n = jnp.maximum(dist, 0)
    max_exact = N_BUCKETS // 2
    nf = jnp.maximum(n, 1).astype(jnp.float32)
    large = max_exact + (jnp.log(nf / max_exact) / math.log(T5_MAX_DISTANCE / max_exact)
                         * (N_BUCKETS - max_exact)).astype(jnp.int32)
    large = jnp.minimum(large, N_BUCKETS - 1)
    return jnp.where(n < max_exact, n, large)


def band_bias(table, step):
    i = jnp.arange(BAND_BLOCK)[:, None]
    j = jnp.arange(2 * BAND_BLOCK)[None, :]
    b = t5_bucket((i + BAND_BLOCK - j) * step)
    return jnp.transpose(table[b], (2, 0, 1))


def to_heads(t, h):
    B, S, _ = t.shape
    return t.reshape(B, S, h, HEAD_DIM).transpose(0, 2, 1, 3)


def from_heads(o):
    B, H, S, D = o.shape
    return o.transpose(0, 2, 1, 3).reshape(B, S, H * D)


def banded_attn(q, k, v, max_dist, bias, sink):
    B, G, R, L, D = q.shape
    n = -(-L // BAND_BLOCK)
    pad = n * BAND_BLOCK - L
    q = jnp.pad(q, ((0, 0), (0, 0), (0, 0), (0, pad), (0, 0)))
    k = jnp.pad(k, ((0, 0), (0, 0), (0, pad), (0, 0)))
    v = jnp.pad(v, ((0, 0), (0, 0), (0, pad), (0, 0)))
    qb = q.reshape(B, G, R, n, BAND_BLOCK, D)
    kb = k.reshape(B, G, n, BAND_BLOCK, D)
    vb = v.reshape(B, G, n, BAND_BLOCK, D)
    shift = ((0, 0), (0, 0), (1, 0), (0, 0), (0, 0))
    kw = jnp.concatenate([jnp.pad(kb, shift)[:, :, :n], kb], axis=3)
    vw = jnp.concatenate([jnp.pad(vb, shift)[:, :, :n], vb], axis=3)
    s = jnp.einsum('bgrnqd,bgnkd->bgrnqk', qb, kw).astype(jnp.float32) * SCALE
    s = s + bias[:, :, None].astype(jnp.float32)
    i = jnp.arange(BAND_BLOCK)[:, None]
    j = jnp.arange(2 * BAND_BLOCK)[None, :]
    dist = i + BAND_BLOCK - j
    in_band = (dist >= 0) & (dist <= max_dist)
    has_prev = (jnp.arange(n)[:, None, None] > 0) | (j >= BAND_BLOCK)[None]
    s = jnp.where(in_band[None] & has_prev, s, NEG)
    if sink is not None:
        sk = jnp.broadcast_to(sink.astype(jnp.float32)[None, :, :, None, None, None], s.shape[:-1] + (1,))
        s = jnp.concatenate([s, sk], axis=-1)
    lse = jax.nn.logsumexp(s, axis=-1, keepdims=True)
    p = jnp.exp(s - lse)[..., :2 * BAND_BLOCK]
    o = jnp.einsum('bgrnqk,bgnkd->bgrnqd', p.astype(v.dtype), vw)
    o = o.reshape(B, G, R, n * BAND_BLOCK, D)[:, :, :, :L]
    lse = lse[..., 0].reshape(B, G, R, n * BAND_BLOCK)[:, :, :, :L]
    return o, lse


def dilated_mixture(q, k, v, table):
    B, H, S, D = q.shape
    outs, lses = [], []
    for window, dil in DILATED_GROUPS:
        L = S // dil

        def fold(t):
            return t.reshape(B, H, L, dil, D).transpose(0, 3, 1, 2, 4).reshape(B * dil, H, L, D)

        bias = band_bias(table, dil)[:, None]
        o, lse = banded_attn(fold(q)[:, :, None], fold(k), fold(v), window // dil, bias, None)
        outs.append(o[:, :, 0].reshape(B, dil, H, L, D).transpose(0, 2, 3, 1, 4).reshape(B, H, S, D))
        lses.append(lse[:, :, 0].reshape(B, dil, H, L).transpose(0, 2, 3, 1).reshape(B, H, S))
    w = jax.nn.softmax(jnp.stack(lses), axis=0)
    return jnp.einsum('gbhs,gbhsd->bhsd', w.astype(q.dtype), jnp.stack(outs))


def moba_attn(q, k, v, table):
    B, H, S, D = q.shape
    nb = -(-S // MOBA_BLOCK)
    Sp = nb * MOBA_BLOCK
    padw = ((0, 0), (0, 0), (0, Sp - S), (0, 0))
    q, k, v = jnp.pad(q, padw), jnp.pad(k, padw), jnp.pad(v, padw)
    kb = k.reshape(B, H, nb, MOBA_BLOCK, D)
    vb = v.reshape(B, H, nb, MOBA_BLOCK, D)
    k_mean = jnp.mean(kb.astype(jnp.float32), axis=3)
    topk = min(MOBA_TOPK, nb)
    table_t = table.T
    bi = jnp.arange(B)[:, None, None, None]
    hi = jnp.arange(H)[None, :, None, None]
    blk_ids = jnp.arange(nb)
    in_blk = jnp.arange(MOBA_BLOCK)
    Q = MOBA_Q_CHUNK

    def chunk(c):
        start = c * Q
        t = start + jnp.arange(Q)
        own = start // MOBA_BLOCK
        qc = lax.dynamic_slice_in_dim(q, start, Q, axis=2)
        gate = jnp.einsum('bhqd,bhnd->bhqn', qc.astype(jnp.float32), k_mean)
        gate = jnp.where(blk_ids < own, gate, -jnp.inf)
        _, idx = lax.top_k(gate, topk)
        valid = jnp.arange(topk) < own
        k_sel = kb[bi, hi, idx]
        v_sel = vb[bi, hi, idx]
        dist_sel = t[:, None, None] - (idx[..., None] * MOBA_BLOCK + in_blk)
        bias_sel = table_t[hi[..., None], t5_bucket(dist_sel)]
        s_sel = jnp.einsum('bhqd,bhqnkd->bhqnk', qc, k_sel).astype(jnp.float32) * SCALE
        s_sel = jnp.where(valid[:, None], s_sel + bias_sel.astype(jnp.float32), NEG)
        k_own = lax.dynamic_index_in_dim(kb, own, axis=2, keepdims=False)
        v_own = lax.dynamic_index_in_dim(vb, own, axis=2, keepdims=False)
        dist_own = t[:, None] - (own * MOBA_BLOCK + in_blk)[None, :]
        s_own = jnp.einsum('bhqd,bhkd->bhqk', qc, k_own).astype(jnp.float32) * SCALE
        s_own = s_own + table_t[:, t5_bucket(dist_own)].astype(jnp.float32)
        s_own = jnp.where(dist_own >= 0, s_own, NEG)
        s = jnp.concatenate([s_sel.reshape(B, H, Q, topk * MOBA_BLOCK), s_own], axis=-1)
        p = jax.nn.softmax(s, axis=-1).astype(v.dtype)
        p_sel = p[..., :topk * MOBA_BLOCK].reshape(B, H, Q, topk, MOBA_BLOCK)
        return (jnp.einsum('bhqnk,bhqnkd->bhqd', p_sel, v_sel)
                + jnp.einsum('bhqk,bhkd->bhqd', p[..., topk * MOBA_BLOCK:], v_own))

    o = lax.map(chunk, jnp.arange(Sp // Q))
    return o.transpose(1, 2, 0, 3, 4).reshape(B, H, Sp, D)[:, :, :S]


def forgetting_attn(q, k, v, log_f):
    B, H, S, D = q.shape
    n = -(-S // FOX_Q_BLOCK)
    Sp = n * FOX_Q_BLOCK
    padw = ((0, 0), (0, 0), (0, Sp - S), (0, 0))
    q, k, v = jnp.pad(q, padw), jnp.pad(k, padw), jnp.pad(v, padw)
    c = lax.cumsum(jnp.pad(log_f, ((0, 0), (0, 0), (0, Sp - S))), axis=2)
    kpos = jnp.arange(Sp)

    def blk(i):
        start = i * FOX_Q_BLOCK
        qi = lax.dynamic_slice_in_dim(q, start, FOX_Q_BLOCK, axis=2)
        ci = lax.dynamic_slice_in_dim(c, start, FOX_Q_BLOCK, axis=2)
        s = jnp.einsum('bhqd,bhkd->bhqk', qi, k).astype(jnp.float32) * SCALE
        s = s + ci[..., None] - c[:, :, None, :]
        causal = (start + jnp.arange(FOX_Q_BLOCK))[:, None] >= kpos[None, :]
        p = jax.nn.softmax(jnp.where(causal, s, NEG), axis=-1)
        return jnp.einsum('bhqk,bhkd->bhqd', p.astype(v.dtype), v)

    o = lax.map(blk, jnp.arange(n))
    return o.transpose(1, 2, 0, 3, 4).reshape(B, H, Sp, D)[:, :, :S]


def setup_inputs(seed: int = 0) -> dict:
    key = jax.random.key(seed)
    ks = jax.random.split(key, 13)
    nrm = jax.random.normal
    return {
        'x': nrm(ks[0], (BATCH, SEQ, D_MODEL), jnp.float32),
        'norm1_g': 1.0 + 0.1 * nrm(ks[1], (DEPTH, D_MODEL), jnp.float32),
        'w_in': nrm(ks[2], (DEPTH, D_MODEL, N_IN), jnp.float32) * D_MODEL ** -0.5,
        'f_bias': 2.0 + 0.5 * nrm(ks[3], (DEPTH, H_D), jnp.float32),
        'sinks': 0.5 * nrm(ks[4], (DEPTH, H_C), jnp.float32),
        'mix_norm_g': 1.0 + 0.1 * nrm(ks[5], (DEPTH, D_MIX), jnp.float32),
        'w_out': nrm(ks[6], (DEPTH, D_MIX, D_MODEL), jnp.float32) * D_MIX ** -0.5,
        'norm2_g': 1.0 + 0.1 * nrm(ks[7], (DEPTH, D_MODEL), jnp.float32),
        'w_gate': nrm(ks[8], (DEPTH, D_MODEL, D_FF), jnp.float32) * D_MODEL ** -0.5,
        'w_up': nrm(ks[9], (DEPTH, D_MODEL, D_FF), jnp.float32) * D_MODEL ** -0.5,
        'w_down': nrm(ks[10], (DEPTH, D_FF, D_MODEL), jnp.float32) * D_FF ** -0.5,
        'rel_bias': 0.5 * nrm(ks[11], (N_BUCKETS, N_BIAS_HEADS), jnp.float32),
        'final_g': 1.0 + 0.1 * nrm(ks[12], (D_MODEL,), jnp.float32),
    }


def reference(x, norm1_g, w_in, f_bias, sinks, mix_norm_g, w_out, norm2_g, w_gate, w_up, w_down, rel_bias, final_g):
    B, S, _ = x.shape
    table_a = rel_bias[:, :H_A]
    table_b = rel_bias[:, H_A:H_A + H_B]
    table_c = rel_bias[:, H_A + H_B:]
    rep_c = H_C // KV_C
    bias_c = band_bias(table_c, 1).reshape(KV_C, rep_c, BAND_BLOCK, 2 * BAND_BLOCK)
    hd = HEAD_DIM
    sizes = [hd * H_A] * 3 + [hd * H_B] * 3 + [hd * H_C, hd * KV_C, hd * KV_C] + [hd * H_D] * 3 + [H_D]
    offsets = np.cumsum(sizes)[:-1].tolist()
    for l in range(DEPTH):
        h = rmsnorm(x, norm1_g[l])
        qa, ka, va, qb, kb, vb, qc, kc, vc, qd, kd, vd, fd = jnp.split(h @ w_in[l], offsets, axis=-1)
        o_a = dilated_mixture(to_heads(qa, H_A), to_heads(ka, H_A), to_heads(va, H_A), table_a)
        o_b = moba_attn(to_heads(qb, H_B), to_heads(kb, H_B), to_heads(vb, H_B), table_b)
        q_c = to_heads(qc, H_C).reshape(B, KV_C, rep_c, S, HEAD_DIM)
        o_c, _ = banded_attn(q_c, to_heads(kc, KV_C), to_heads(vc, KV_C), SWA_WINDOW - 1,
                             bias_c, sinks[l].reshape(KV_C, rep_c))
        o_c = o_c.reshape(B, H_C, S, HEAD_DIM)
        log_f = jax.nn.log_sigmoid((fd + f_bias[l]).astype(jnp.float32)).transpose(0, 2, 1)
        o_d = forgetting_attn(to_heads(qd, H_D), to_heads(kd, H_D), to_heads(vd, H_D), log_f)
        mix = jnp.concatenate([from_heads(o_a), from_heads(o_b), from_heads(o_c), from_heads(o_d)], axis=-1)
        x = x + group_rmsnorm(mix, mix_norm_g[l]) @ w_out[l]
        h = rmsnorm(x, norm2_g[l])
        x = x + (jax.nn.silu(h @ w_gate[l]) * (h @ w_up[l])) @ w_down[l]
    return rmsnorm(x, final_g)
```

```python
import functools
import math

import numpy as np
import jax
import jax.numpy as jnp
from jax import lax
from jax.experimental import pallas as pl
from jax.experimental.pallas import tpu as pltpu

F32 = jnp.float32
BF16 = jnp.bfloat16

LANES = 128
VMEM_LIMIT_BYTES = 56 * 1024 * 1024

D_MODEL = 1024
HEAD_DIM = 64
N_HEADS = 4
GROUP_W = N_HEADS * HEAD_DIM
KV_C = 2
DILATED_GROUPS = ((128, 1), (512, 4), (2048, 16))
BAND = 128
MOBA_BLOCK = 256
MOBA_TOPK = 3
SWA_WINDOW = 128
N_BUCKETS = 32
T5_MAX_DISTANCE = 2048
D_FF = 2816
FF_CHUNK = 256
EPS = 1e-6
NEG = -1e30
SCALE = HEAD_DIM ** -0.5

QA_B, KA_B, VA_B, QC_B, KC_B, VC_B, VB_B, VD_B = 0, 2, 4, 6, 8, 10, 12, 14
NY = 16 * LANES
QB_OFF = NY
KB_OFF = QB_OFF + N_HEADS * LANES
QD_OFF = KB_OFF + N_HEADS * LANES
KD_OFF = QD_OFF + N_HEADS * LANES
FD_OFF = KD_OFF + N_HEADS * LANES
N_PROJ = FD_OFF + LANES
AUG0 = HEAD_DIM


def _params(sem):
    return pltpu.CompilerParams(dimension_semantics=sem, vmem_limit_bytes=VMEM_LIMIT_BYTES)


def _resident(shape, index_map):
    return pl.BlockSpec(shape, index_map, pipeline_mode=pl.Buffered(1))


def _dot(a, b):
    return jnp.dot(a, b, preferred_element_type=F32)


def _dot_nt(a, b):
    return lax.dot_general(a, b, (((1,), (1,)), ((), ())), preferred_element_type=F32)


def _rms(x, g):
    return x * lax.rsqrt(jnp.mean(x * x, axis=-1, keepdims=True) + EPS) * g


def _proj_column_table():
    hd = HEAD_DIM
    qa, ka, va, qb, kb, vb, qc = (np.arange(o, o + GROUP_W) for o in (0, 256, 512, 768, 1024, 1280, 1536))
    kc, vc = np.arange(1792, 1920), np.arange(1920, 2048)
    qd, kd, vd = (np.arange(o, o + GROUP_W) for o in (2048, 2304, 2560))
    fd = np.arange(2816, 2820)

    def dup_kv(c):
        return np.concatenate([c[:hd], c[:hd], c[hd:], c[hd:]])

    def widen(c):
        out = np.full((N_HEADS, LANES), -1, np.int64)
        out[:, :hd] = c.reshape(N_HEADS, hd)
        return out.reshape(-1)

    fd_cols = np.full((LANES,), -1, np.int64)
    fd_cols[:N_HEADS] = fd
    cols = np.concatenate([qa, ka, va, qc, dup_kv(kc), dup_kv(vc), vb, vd,
                           widen(qb), widen(kb), widen(qd), widen(kd), fd_cols])
    assert cols.shape == (N_PROJ,)
    return cols


def _t5_bucket_np(dist):
    n = np.maximum(dist, 0)
    max_exact = N_BUCKETS // 2
    nf = np.maximum(n, 1).astype(np.float32)
    large = max_exact + (np.log(nf / np.float32(max_exact)) / np.float32(math.log(T5_MAX_DISTANCE / max_exact))
                         * np.float32(N_BUCKETS - max_exact)).astype(np.int32)
    large = np.minimum(large, N_BUCKETS - 1)
    return np.where(n < max_exact, n, large).astype(np.int32)


def _band_bias(table, step, max_dist):
    i = np.arange(BAND)[:, None]
    j = np.arange(2 * BAND)[None, :]
    tiles = []
    for dist in (i + BAND - j, i - j):
        ok = (dist >= 0) & (dist <= max_dist)
        b = jnp.transpose(table[_t5_bucket_np(dist * step)], (2, 0, 1))
        tiles.append(jnp.where(ok[None], b, NEG))
    return jnp.stack(tiles).astype(F32)


def _moba_bias(table, n_blocks):
    i = np.arange(MOBA_BLOCK)[None, :, None]
    j = np.arange(MOBA_BLOCK)[None, None, :]
    d = np.arange(n_blocks)[:, None, None]
    dist = MOBA_BLOCK * d + i - j
    b = jnp.transpose(table[_t5_bucket_np(dist)], (3, 0, 1, 2))
    return jnp.where((dist >= 0)[None], b, NEG).astype(F32)


def _placement_constants():
    pq = np.zeros((3 * LANES, N_HEADS * LANES), np.float32)
    pk = np.zeros((3 * LANES, N_HEADS * LANES), np.float32)
    cq = np.zeros((1, N_HEADS * LANES), np.float32)
    ck = np.zeros((1, N_HEADS * LANES), np.float32)
    for h in range(N_HEADS):
        for part in range(3):
            pq[part * LANES + h, h * LANES + AUG0 + part] = 1.0
            pk[part * LANES + h, h * LANES + AUG0 + 3 + part] = -1.0
            ck[0, h * LANES + AUG0 + part] = 1.0
            cq[0, h * LANES + AUG0 + 3 + part] = 1.0
    shift = np.zeros((LANES, LANES), np.float32)
    for n in range(AUG0):
        shift[n, AUG0 + n] = 1.0
    return (jnp.asarray(pq, BF16), jnp.asarray(pk, BF16), jnp.asarray(cq), jnp.asarray(ck),
            jnp.asarray(shift, BF16))


def _inproj_kernel(x_ref, g_ref, w_ref, fb_ref, pq_ref, pk_ref, cq_ref, ck_ref,
                   y_ref, qb_ref, kb_ref, qd_ref, kd_ref, carry_ref, *, tm):
    t = pl.program_id(1)

    @pl.when(t == 0)
    def _():
        carry_ref[...] = jnp.zeros_like(carry_ref)

    h = _rms(x_ref[...], g_ref[...]).astype(BF16)

    def mm(c0, c1):
        return _dot(h, w_ref[:, c0:c1])

    for grp in range(NY // GROUP_W):
        c0 = grp * GROUP_W
        a = mm(c0, c0 + GROUP_W)
        if c0 in (QA_B * LANES, QC_B * LANES):
            a = a * SCALE
        y_ref[:, c0:c0 + GROUP_W] = a.astype(BF16)

    wide = N_HEADS * LANES
    qb_ref[...] = (mm(QB_OFF, QB_OFF + wide) * SCALE).astype(BF16)

    lane = lax.broadcasted_iota(jnp.int32, (tm, wide), 1) % LANES
    row = lax.broadcasted_iota(jnp.int32, (tm, wide), 0)
    blk = (t * tm + row) // MOBA_BLOCK
    kb = jnp.where(lane == AUG0 + blk, 1.0, mm(KB_OFF, KB_OFF + wide))
    kb_ref[...] = kb.astype(BF16)

    f = mm(FD_OFF, FD_OFF + LANES) + fb_ref[...]
    logf = jnp.minimum(f, 0.0) - jnp.log(1.0 + jnp.exp(-jnp.abs(f)))
    srow = lax.broadcasted_iota(jnp.int32, (tm, LANES), 0)
    c = logf
    k = 1
    while k < tm:
        c = c + jnp.where(srow >= k, pltpu.roll(c, k, axis=0), 0.0)
        k *= 2
    c = c + carry_ref[0:1, :]
    carry_ref[0:1, :] = c[tm - 1:tm, :]

    c_hi = c.astype(BF16)
    r1 = c - c_hi.astype(F32)
    c_mid = r1.astype(BF16)
    c_lo = (r1 - c_mid.astype(F32)).astype(BF16)
    feats = jnp.concatenate([c_hi, c_mid, c_lo], axis=1)
    qd = mm(QD_OFF, QD_OFF + wide) * SCALE + _dot(feats, pq_ref[...]) + cq_ref[...]
    kd = mm(KD_OFF, KD_OFF + wide) + _dot(feats, pk_ref[...]) + ck_ref[...]
    qd_ref[...] = qd.astype(BF16)
    kd_ref[...] = kd.astype(BF16)


def _inproj(x, g, w, fb, consts, tm=512):
    B, S, D = x.shape
    pq, pk, cq, ck, _ = consts
    wide = N_HEADS * LANES
    const = lambda b, t: (0, 0)
    tile = lambda b, t: (b, t, 0)
    out_shape = (jax.ShapeDtypeStruct((B, S, NY), BF16),) + tuple(
        jax.ShapeDtypeStruct((B, S, wide), BF16) for _ in range(4))
    return pl.pallas_call(
        functools.partial(_inproj_kernel, tm=tm),
        grid=(B, S // tm),
        in_specs=[pl.BlockSpec((None, tm, D), tile),
                  _resident((1, D), const),
                  _resident((D, N_PROJ), const),
                  _resident((1, LANES), const),
                  _resident((3 * LANES, wide), const),
                  _resident((3 * LANES, wide), const),
                  _resident((1, wide), const),
                  _resident((1, wide), const)],
        out_specs=(pl.BlockSpec((None, tm, NY), tile),) + tuple(
            pl.BlockSpec((None, tm, wide), tile) for _ in range(4)),
        out_shape=out_shape,
        scratch_shapes=[pltpu.VMEM((8, LANES), F32)],
        compiler_params=_params(("arbitrary", "arbitrary")),
        name="inproj",
    )(x, g, w, fb, pq, pk, cq, ck)


def _half_masks(rows):
    lane = lax.broadcasted_iota(jnp.int32, (rows, LANES), 1)
    return lane < HEAD_DIM


def _banded_kernel(*refs, n_tiles, with_sink, with_lse):
    refs = list(refs)
    q_ref, k_ref, v_ref, bias_ref = refs[:4]
    rest = refs[4:]
    sink_ref = rest.pop(0) if with_sink else None
    o_ref = rest.pop(0)
    lse_ref = rest.pop(0) if with_lse else None
    pair = pl.program_id(2)
    low = _half_masks(BAND)
    keep = (low.astype(F32).astype(BF16), (1.0 - low.astype(F32)).astype(BF16))

    def body(n, carry):
        r0 = pl.multiple_of(n * BAND, BAND)
        k0 = pl.multiple_of(jnp.maximum(r0 - BAND, 0), BAND)
        variant = jnp.where(n == 0, 1, 0)
        qt = q_ref[pl.ds(r0, BAND), :]
        kt = k_ref[pl.ds(k0, 2 * BAND), :]
        vt = v_ref[pl.ds(k0, 2 * BAND), :]
        outs, lses = [], []
        for hh in range(2):
            s = _dot_nt(qt * keep[hh], kt) + bias_ref[variant, hh]
            m = jnp.max(s, axis=1, keepdims=True)
            if with_sink:
                sink = sink_ref[2 * pair + hh]
                m = jnp.maximum(m, sink)
            p = jnp.exp(s - m)
            l = jnp.sum(p, axis=1, keepdims=True)
            if with_sink:
                l = l + jnp.exp(sink - m)
            outs.append(_dot(p.astype(BF16), vt) / l)
            lses.append(jnp.broadcast_to(m + jnp.log(l), (BAND, LANES)))
        o_ref[pl.ds(r0, BAND), :] = jnp.where(low, outs[0], outs[1])
        if with_lse:
            lse_ref[pl.ds(r0, BAND), :] = jnp.where(low, lses[0], lses[1])
        return carry

    lax.fori_loop(0, n_tiles, body, 0)


def _banded(y, bias, sinks, *, dil, q_blk, k_blk, v_blk, with_lse, name):
    B, S, _ = y.shape
    L = S // dil
    nyb = NY // LANES
    yv = y.reshape(B, L, dil * NY)
    col = lambda blk: (lambda b, r, p: (b, 0, r * nyb + blk + p))
    in_specs = [pl.BlockSpec((None, L, LANES), col(q_blk)),
                pl.BlockSpec((None, L, LANES), col(k_blk)),
                pl.BlockSpec((None, L, LANES), col(v_blk)),
                pl.BlockSpec((2, 2, BAND, 2 * BAND), lambda b, r, p: (0, p, 0, 0))]
    args = [yv, yv, yv, bias]
    if sinks is not None:
        in_specs.append(pl.BlockSpec(memory_space=pltpu.SMEM))
        args.append(sinks)
    o_spec = pl.BlockSpec((None, L, LANES), lambda b, r, p: (b, 0, r * 2 + p))
    o_shape = jax.ShapeDtypeStruct((B, L, dil * GROUP_W), F32)
    n_out = 2 if with_lse else 1
    outs = pl.pallas_call(
        functools.partial(_banded_kernel, n_tiles=L // BAND, with_sink=sinks is not None, with_lse=with_lse),
        grid=(B, dil, 2),
        in_specs=in_specs,
        out_specs=(o_spec,) * n_out,
        out_shape=(o_shape,) * n_out,
        compiler_params=_params(("arbitrary",) * 3),
        name=name,
    )(*args)
    return tuple(o.reshape(B, S, GROUP_W) for o in outs)


def _online_step(s, vt, m, l, acc):
    m_new = jnp.maximum(m, jnp.max(s, axis=1, keepdims=True))
    alpha = jnp.exp(m - m_new)
    p = jnp.exp(s - m_new)
    l = alpha * l + jnp.sum(p, axis=1, keepdims=True)
    acc = alpha * acc + _dot(p.astype(BF16), vt)
    return m_new, l, acc


def _flash_init(tq):
    return (jnp.full((tq, 1), NEG, F32), jnp.zeros((tq, 1), F32), jnp.zeros((tq, LANES), F32))


def _moba_kernel(q_ref, k_ref, v_ref, bias_ref, shift_ref, o_ref, kmean_ref, *, n_blocks):
    i = pl.program_id(2)
    tq = MOBA_BLOCK

    @pl.when(i == 0)
    def _():
        kmean_ref[...] = jnp.zeros_like(kmean_ref)
        for hh in range(2):
            kf = k_ref[:, hh * LANES:(hh + 1) * LANES].astype(F32)
            km = jnp.mean(kf.reshape(n_blocks, MOBA_BLOCK, LANES), axis=1)
            kmean_ref[hh, 0:n_blocks, :] = km

    lane = lax.broadcasted_iota(jnp.int32, (tq, LANES), 1)
    lane_f = lane.astype(F32)
    low = lane < HEAD_DIM
    outs = []
    for hh in range(2):
        q = q_ref[:, hh * LANES:(hh + 1) * LANES]
        km = kmean_ref[hh]
        k1 = km.astype(BF16)
        r1 = km - k1.astype(F32)
        k2 = r1.astype(BF16)
        k3 = (r1 - k2.astype(F32)).astype(BF16)
        gate = _dot_nt(q, k1) + _dot_nt(q, k2) + _dot_nt(q, k3)

        avail = lane < i
        sel = jnp.zeros((tq, LANES), jnp.bool_)
        for _ in range(MOBA_TOPK):
            cur = jnp.where(avail, gate, -jnp.inf)
            top = jnp.max(cur, axis=1, keepdims=True)
            is_top = avail & (cur == top)
            first = jnp.min(jnp.where(is_top, lane_f, float(LANES)), axis=1, keepdims=True)
            pick = lane_f == first
            sel = sel | pick
            avail = avail & jnp.logical_not(pick)
        allowed = sel | (lane == i)
        penalty = jnp.where(allowed | (lane >= n_blocks), 0.0, NEG).astype(BF16)
        q_aug = (q.astype(F32) + _dot(penalty, shift_ref[...])).astype(BF16)

        def body(n, carry, hh=hh, q_aug=q_aug):
            r0 = pl.multiple_of(n * MOBA_BLOCK, MOBA_BLOCK)
            kt = k_ref[pl.ds(r0, MOBA_BLOCK), hh * LANES:(hh + 1) * LANES]
            vt = v_ref[pl.ds(r0, MOBA_BLOCK), :]
            s = _dot_nt(q_aug, kt) + bias_ref[hh, i - n]
            return _online_step(s, vt, *carry)

        m, l, acc = lax.fori_loop(0, i + 1, body, _flash_init(tq))
        outs.append(acc / l)
    o_ref[...] = jnp.where(low, outs[0], outs[1])


def _moba(qb, kb, y, bias, shift):
    B, S, _ = qb.shape
    nb = S // MOBA_BLOCK
    assert nb <= AUG0
    return pl.pallas_call(
        functools.partial(_moba_kernel, n_blocks=nb),
        grid=(B, 2, nb),
        in_specs=[pl.BlockSpec((None, MOBA_BLOCK, 2 * LANES), lambda b, p, i: (b, i, p)),
                  pl.BlockSpec((None, S, 2 * LANES), lambda b, p, i: (b, 0, p)),
                  pl.BlockSpec((None, S, LANES), lambda b, p, i: (b, 0, VB_B + p)),
                  pl.BlockSpec((2, nb, MOBA_BLOCK, MOBA_BLOCK), lambda b, p, i: (p, 0, 0, 0)),
                  pl.BlockSpec((LANES, LANES), lambda b, p, i: (0, 0))],
        out_specs=pl.BlockSpec((None, MOBA_BLOCK, LANES), lambda b, p, i: (b, i, p)),
        out_shape=jax.ShapeDtypeStruct((B, S, GROUP_W), F32),
        scratch_shapes=[pltpu.VMEM((2, LANES, LANES), F32)],
        compiler_params=_params(("arbitrary",) * 3),
        name="moba",
    )(qb, kb, y, bias, shift)


def _fox_kernel(q_ref, k_ref, v_ref, o_ref, *, tq):
    i = pl.program_id(2)
    row = lax.broadcasted_iota(jnp.int32, (tq, tq), 0)
    col = lax.broadcasted_iota(jnp.int32, (tq, tq), 1)
    causal = row >= col
    low = _half_masks(tq)
    d0 = pl.multiple_of(i * tq, tq)
    outs = []
    for hh in range(2):
        q = q_ref[:, hh * LANES:(hh + 1) * LANES]

        def body(n, carry, hh=hh, q=q):
            r0 = pl.multiple_of(n * tq, tq)
            kt = k_ref[pl.ds(r0, tq), hh * LANES:(hh + 1) * LANES]
            vt = v_ref[pl.ds(r0, tq), :]
            return _online_step(_dot_nt(q, kt), vt, *carry)

        carry = lax.fori_loop(0, i, body, _flash_init(tq))
        kt = k_ref[pl.ds(d0, tq), hh * LANES:(hh + 1) * LANES]
        vt = v_ref[pl.ds(d0, tq), :]
        s = jnp.where(causal, _dot_nt(q, kt), NEG)
        m, l, acc = _online_step(s, vt, *carry)
        outs.append(acc / l)
    o_ref[...] = jnp.where(low, outs[0], outs[1])


def _fox(qd, kd, y, tq=256):
    B, S, _ = qd.shape
    return pl.pallas_call(
        functools.partial(_fox_kernel, tq=tq),
        grid=(B, 2, S // tq),
        in_specs=[pl.BlockSpec((None, tq, 2 * LANES), lambda b, p, i: (b, i, p)),
                  pl.BlockSpec((None, S, 2 * LANES), lambda b, p, i: (b, 0, p)),
                  pl.BlockSpec((None, S, LANES), lambda b, p, i: (b, 0, VD_B + p))],
        out_specs=pl.BlockSpec((None, tq, LANES), lambda b, p, i: (b, i, p)),
        out_shape=jax.ShapeDtypeStruct((B, S, GROUP_W), F32),
        compiler_params=_params(("arbitrary",) * 3),
        name="fox",
    )(qd, kd, y)


def _outffn_kernel(x_ref, oa1_ref, oa4_ref, oa16_ref, la1_ref, la4_ref, la16_ref, ob_ref, oc_ref, od_ref,
                   mg_ref, wo_ref, g2_ref, wg_ref, wu_ref, wd_ref, fg_ref, out_ref, *, final):
    l1, l4, l16 = la1_ref[...], la4_ref[...], la16_ref[...]
    m = jnp.maximum(jnp.maximum(l1, l4), l16)
    e1, e4, e16 = jnp.exp(l1 - m), jnp.exp(l4 - m), jnp.exp(l16 - m)
    mix_a = (e1 * oa1_ref[...] + e4 * oa4_ref[...] + e16 * oa16_ref[...]) / (e1 + e4 + e16)

    x1 = x_ref[...]
    for g, part in enumerate((mix_a, ob_ref[...], oc_ref[...], od_ref[...])):
        c0 = g * GROUP_W
        normed = _rms(part, mg_ref[:, c0:c0 + GROUP_W]).astype(BF16)
        x1 = x1 + _dot(normed, wo_ref[c0:c0 + GROUP_W, :])

    h = _rms(x1, g2_ref[...]).astype(BF16)
    ffn = jnp.zeros_like(x1)
    for c in range(D_FF // FF_CHUNK):
        c0 = c * FF_CHUNK
        gate = _dot(h, wg_ref[:, c0:c0 + FF_CHUNK])
        up = _dot(h, wu_ref[:, c0:c0 + FF_CHUNK])
        act = (gate * (1.0 / (1.0 + jnp.exp(-gate))) * up).astype(BF16)
        ffn = ffn + _dot(act, wd_ref[c0:c0 + FF_CHUNK, :])
    x2 = x1 + ffn
    if final:
        x2 = _rms(x2, fg_ref[...])
    out_ref[...] = x2


def _outffn(x, parts, mg, wo, g2, wg, wu, wd, fg, *, final, tm=512):
    B, S, D = x.shape
    tile = lambda b, t: (b, t, 0)
    const = lambda b, t: (0, 0)
    part_spec = pl.BlockSpec((None, tm, GROUP_W), tile)
    return pl.pallas_call(
        functools.partial(_outffn_kernel, final=final),
        grid=(B, S // tm),
        in_specs=[pl.BlockSpec((None, tm, D), tile)] + [part_spec] * 9 + [
            _resident((1, D), const),
            _resident((D, D), const),
            _resident((1, D), const),
            _resident((D, D_FF), const),
            _resident((D, D_FF), const),
            _resident((D_FF, D), const),
            _resident((1, D), const)],
        out_specs=pl.BlockSpec((None, tm, D), tile),
        out_shape=jax.ShapeDtypeStruct((B, S, D), F32),
        compiler_params=_params(("arbitrary", "arbitrary")),
        name="outffn",
    )(x, *parts, mg, wo, g2, wg, wu, wd, fg)


def kernel(x, norm1_g, w_in, f_bias, sinks, mix_norm_g, w_out, norm2_g, w_gate, w_up, w_down, rel_bias, final_g):
    B, S, D = x.shape
    depth = w_in.shape[0]
    max_dil = max(dil for _, dil in DILATED_GROUPS)
    assert D == D_MODEL and S % (2 * BAND * max_dil) == 0 and w_gate.shape[-1] == D_FF

    cols = _proj_column_table()
    w_proj = (jnp.take(w_in, jnp.asarray(np.maximum(cols, 0)), axis=2)
              * jnp.asarray(cols >= 0, F32)).astype(BF16)
    wo_b, wg_b, wu_b, wd_b = (w.astype(BF16) for w in (w_out, w_gate, w_up, w_down))
    fb = jnp.zeros((depth, 1, LANES), F32).at[:, 0, :N_HEADS].set(f_bias)
    consts = _placement_constants()
    table_a = rel_bias[:, :N_HEADS]
    table_b = rel_bias[:, N_HEADS:2 * N_HEADS]
    table_c = rel_bias[:, 2 * N_HEADS:]
    bias_a = [_band_bias(table_a, dil, window // dil) for window, dil in DILATED_GROUPS]
    bias_c = _band_bias(table_c, 1, SWA_WINDOW - 1)
    bias_b = _moba_bias(table_b, S // MOBA_BLOCK)

    for l in range(depth):
        y, qb, kb, qd, kd = _inproj(x, norm1_g[l][None], w_proj[l], fb[l], consts)
        a_parts = [_banded(y, bias_a[g], None, dil=dil, q_blk=QA_B, k_blk=KA_B, v_blk=VA_B,
                           with_lse=True, name=f"dilated{dil}")
                   for g, (_, dil) in enumerate(DILATED_GROUPS)]
        (o_c,) = _banded(y, bias_c, sinks[l], dil=1, q_blk=QC_B, k_blk=KC_B, v_blk=VC_B,
                         with_lse=False, name="swa")
        o_b = _moba(qb, kb, y, bias_b, consts[4])
        o_d = _fox(qd, kd, y)
        parts = [p[0] for p in a_parts] + [p[1] for p in a_parts] + [o_b, o_c, o_d]
        x = _outffn(x, parts, mix_norm_g[l][None], wo_b[l], norm2_g[l][None], wg_b[l], wu_b[l], wd_b[l],
                    final_g[None], final=(l == depth - 1))
    return x
```

```python
import functools
import math

import numpy as np
import jax
import jax.numpy as jnp
from jax import lax
from jax.experimental import pallas as pl
from jax.experimental.pallas import tpu as pltpu

F32 = jnp.float32
BF16 = jnp.bfloat16

LANES = 128
VMEM_LIMIT_BYTES = 56 * 1024 * 1024

D_MODEL = 1024
HEAD_DIM = 64
N_HEADS = 4
GROUP_W = N_HEADS * HEAD_DIM
DILATED_GROUPS = ((128, 1), (512, 4), (2048, 16))
BAND = 128
MOBA_BLOCK = 256
MOBA_TOPK = 3
SWA_WINDOW = 128
N_BUCKETS = 32
T5_MAX_DISTANCE = 2048
D_FF = 2816
FF_CHUNK = 256
EPS = 1e-6
NEG = -1e30
SCALE = HEAD_DIM ** -0.5

QA_B, KA_B, VA_B, QC_B, KC_B, VC_B, VB_B, VD_B = 0, 2, 4, 6, 8, 10, 12, 14
NY = 16 * LANES
A_COLS = 3 * GROUP_W
WIDE = N_HEADS * LANES
QB_OFF = NY
KB_OFF = QB_OFF + WIDE
QD_OFF = KB_OFF + WIDE
KD_OFF = QD_OFF + WIDE
FD_OFF = KD_OFF + WIDE
N_PROJ = FD_OFF + LANES
AUG0 = HEAD_DIM

PROJ_TM = 512
FLASH_TQ = 256
FLASH_TK = 512
BANDED_UNROLL = 4


def _params(sem):
    return pltpu.CompilerParams(dimension_semantics=sem, vmem_limit_bytes=VMEM_LIMIT_BYTES)


def _resident(shape, index_map):
    return pl.BlockSpec(shape, index_map, pipeline_mode=pl.Buffered(1))


def _dot(a, b):
    return jnp.dot(a, b, preferred_element_type=F32)


def _dot_nt(a, b):
    return lax.dot_general(a, b, (((1,), (1,)), ((), ())), preferred_element_type=F32)


def _rms(x, g):
    return x * lax.rsqrt(jnp.mean(x * x, axis=-1, keepdims=True) + EPS) * g


def _rearrange_w_in(w):
    w = w.astype(BF16)
    hd = HEAD_DIM
    sl = lambda a, b: w[..., a:b]

    def widen(c):
        c4 = c.reshape(c.shape[:-1] + (N_HEADS, hd))
        c4 = jnp.pad(c4, [(0, 0)] * (c4.ndim - 1) + [(0, LANES - hd)])
        return c4.reshape(c.shape[:-1] + (WIDE,))

    kc0, kc1 = sl(1792, 1856), sl(1856, 1920)
    vc0, vc1 = sl(1920, 1984), sl(1984, 2048)
    fd = jnp.pad(sl(2816, 2820), [(0, 0)] * (w.ndim - 1) + [(0, LANES - N_HEADS)])
    out = jnp.concatenate([
        sl(0, 768),
        sl(1536, 1792),
        kc0, kc0, kc1, kc1,
        vc0, vc0, vc1, vc1,
        sl(1280, 1536),
        sl(2560, 2816),
        widen(sl(768, 1024)), widen(sl(1024, 1280)),
        widen(sl(2048, 2304)), widen(sl(2304, 2560)),
        fd], axis=-1)
    assert out.shape[-1] == N_PROJ
    return out


def _t5_bucket_np(dist):
    n = np.maximum(dist, 0)
    max_exact = N_BUCKETS // 2
    nf = np.maximum(n, 1).astype(np.float32)
    large = max_exact + (np.log(nf / np.float32(max_exact)) / np.float32(math.log(T5_MAX_DISTANCE / max_exact))
                         * np.float32(N_BUCKETS - max_exact)).astype(np.int32)
    large = np.minimum(large, N_BUCKETS - 1)
    return np.where(n < max_exact, n, large).astype(np.int32)


def _bucket_thresholds():
    n = np.arange(1 << 16)
    bucket = _t5_bucket_np(n)
    assert (np.diff(bucket) >= 0).all() and bucket[-1] == N_BUCKETS - 1
    thr = np.searchsorted(bucket, np.arange(N_BUCKETS), side="left")
    assert (np.searchsorted(thr, n, side="right") - 1 == bucket).all()
    return tuple(int(t) for t in thr)


def _placement_constants():
    pq = np.zeros((3 * LANES, WIDE), np.float32)
    pk = np.zeros((3 * LANES, WIDE), np.float32)
    cq = np.zeros((1, WIDE), np.float32)
    ck = np.zeros((1, WIDE), np.float32)
    for h in range(N_HEADS):
        for part in range(3):
            pq[part * LANES + h, h * LANES + AUG0 + part] = 1.0
            pk[part * LANES + h, h * LANES + AUG0 + 3 + part] = -1.0
            ck[0, h * LANES + AUG0 + part] = 1.0
            cq[0, h * LANES + AUG0 + 3 + part] = 1.0
    shift = np.zeros((LANES, LANES), np.float32)
    for n in range(AUG0):
        shift[n, AUG0 + n] = 1.0
    return (jnp.asarray(pq, BF16), jnp.asarray(pk, BF16), jnp.asarray(cq), jnp.asarray(ck),
            jnp.asarray(shift, BF16))


def _bias_tiles_kernel(base_ref, table_ref, o_ref, *, rows, cols, step, max_dist, thresholds, chunk):
    t = pl.program_id(0)
    base = base_ref[t]

    def body(c, carry):
        r0 = pl.multiple_of(c * chunk, chunk)
        i = lax.broadcasted_iota(jnp.int32, (chunk, cols), 0) + r0
        j = lax.broadcasted_iota(jnp.int32, (chunk, cols), 1)
        dist = base + i - j
        ok = (dist >= 0) & (dist <= max_dist)
        scaled = dist * step
        vals = [jnp.full((chunk, cols), table_ref[0, h], F32) for h in range(N_HEADS)]
        for b in range(1, N_BUCKETS):
            reached = scaled >= thresholds[b]
            vals = [jnp.where(reached, table_ref[b, h], v) for h, v in enumerate(vals)]
        for h in range(N_HEADS):
            o_ref[h, pl.ds(r0, chunk), :] = jnp.where(ok, vals[h], NEG)
        return carry

    lax.fori_loop(0, rows // chunk, body, 0)


def _bias_tiles(table, bases, *, rows, cols, step, max_dist, name):
    n_tiles = len(bases)
    return pl.pallas_call(
        functools.partial(_bias_tiles_kernel, rows=rows, cols=cols, step=step, max_dist=max_dist,
                          thresholds=_bucket_thresholds(), chunk=32),
        grid=(n_tiles,),
        in_specs=[pl.BlockSpec(memory_space=pltpu.SMEM), pl.BlockSpec(memory_space=pltpu.SMEM)],
        out_specs=pl.BlockSpec((N_HEADS, None, rows, cols), lambda t: (0, t, 0, 0)),
        out_shape=jax.ShapeDtypeStruct((N_HEADS, n_tiles, rows, cols), F32),
        compiler_params=_params(("arbitrary",)),
        name=name,
    )(jnp.asarray(bases, jnp.int32), table)


def _band_bias(table, step, max_dist, name):
    return _bias_tiles(table, (BAND, 0), rows=BAND, cols=2 * BAND, step=step, max_dist=max_dist, name=name)


def _moba_bias(table, n_blocks):
    bases = tuple(MOBA_BLOCK * d for d in range(-1, n_blocks))
    return _bias_tiles(table, bases, rows=MOBA_BLOCK, cols=MOBA_BLOCK, step=1, max_dist=1 << 30,
                       name="moba_bias")


def _inproj_kernel(x_ref, g_ref, w_ref, fb_ref, pq_ref, pk_ref, cq_ref, ck_ref,
                   y_ref, ya4_ref, ya16_ref, qb_ref, kb_ref, qd_ref, kd_ref, carry_ref, fold_ref, *, tm):
    t = pl.program_id(1)

    @pl.when(t == 0)
    def _():
        carry_ref[...] = jnp.zeros_like(carry_ref)

    h = _rms(x_ref[...], g_ref[...]).astype(BF16)

    def mm(c0, c1):
        return _dot(h, w_ref[:, c0:c1])

    for grp in range(NY // GROUP_W):
        c0 = grp * GROUP_W
        a = mm(c0, c0 + GROUP_W)
        if c0 in (QA_B * LANES, QC_B * LANES):
            a = a * SCALE
        y_ref[:, c0:c0 + GROUP_W] = a.astype(BF16)
        if c0 < A_COLS:
            fold_ref[2 * grp] = a[:, :LANES]
            fold_ref[2 * grp + 1] = a[:, LANES:]

    for dil, ref in ((4, ya4_ref), (16, ya16_ref)):
        for r in range(dil):
            for slab in range(A_COLS // LANES):
                piece = fold_ref[slab, pl.ds(r, tm // dil, stride=dil), :]
                ref[r, :, slab * LANES:(slab + 1) * LANES] = piece.astype(BF16)

    qb_ref[...] = (mm(QB_OFF, QB_OFF + WIDE) * SCALE).astype(BF16)

    lane = lax.broadcasted_iota(jnp.int32, (tm, WIDE), 1) % LANES
    row = lax.broadcasted_iota(jnp.int32, (tm, WIDE), 0)
    blk = (t * tm + row) // MOBA_BLOCK
    kb = jnp.where(lane == AUG0 + blk, 1.0, mm(KB_OFF, KB_OFF + WIDE))
    kb_ref[...] = kb.astype(BF16)

    f = mm(FD_OFF, FD_OFF + LANES) + fb_ref[...]
    logf = jnp.minimum(f, 0.0) - jnp.log(1.0 + jnp.exp(-jnp.abs(f)))
    srow = lax.broadcasted_iota(jnp.int32, (tm, LANES), 0)
    c = logf
    k = 1
    while k < tm:
        c = c + jnp.where(srow >= k, pltpu.roll(c, k, axis=0), 0.0)
        k *= 2
    c = c + carry_ref[0:1, :]
    carry_ref[0:1, :] = c[tm - 1:tm, :]

    c_hi = c.astype(BF16)
    r1 = c - c_hi.astype(F32)
    c_mid = r1.astype(BF16)
    c_lo = (r1 - c_mid.astype(F32)).astype(BF16)
    feats = jnp.concatenate([c_hi, c_mid, c_lo], axis=1)
    qd = mm(QD_OFF, QD_OFF + WIDE) * SCALE + _dot(feats, pq_ref[...]) + cq_ref[...]
    kd = mm(KD_OFF, KD_OFF + WIDE) + _dot(feats, pk_ref[...]) + ck_ref[...]
    qd_ref[...] = qd.astype(BF16)
    kd_ref[...] = kd.astype(BF16)


def _inproj(x, g, w, fb, consts):
    B, S, D = x.shape
    tm = PROJ_TM
    pq, pk, cq, ck, _ = consts
    const = lambda b, t: (0, 0)
    tile = lambda b, t: (b, t, 0)
    fold_tile = lambda b, t: (b, 0, t, 0)
    wide_shape = jax.ShapeDtypeStruct((B, S, WIDE), BF16)
    out_shape = (jax.ShapeDtypeStruct((B, S, NY), BF16),
                 jax.ShapeDtypeStruct((B, 4, S // 4, A_COLS), BF16),
                 jax.ShapeDtypeStruct((B, 16, S // 16, A_COLS), BF16),
                 wide_shape, wide_shape, wide_shape, wide_shape)
    wide_spec = pl.BlockSpec((None, tm, WIDE), tile)
    return pl.pallas_call(
        functools.partial(_inproj_kernel, tm=tm),
        grid=(B, S // tm),
        in_specs=[pl.BlockSpec((None, tm, D), tile),
                  _resident((1, D), const),
                  _resident((D, N_PROJ), const),
                  _resident((1, LANES), const),
                  _resident((3 * LANES, WIDE), const),
                  _resident((3 * LANES, WIDE), const),
                  _resident((1, WIDE), const),
                  _resident((1, WIDE), const)],
        out_specs=(pl.BlockSpec((None, tm, NY), tile),
                   pl.BlockSpec((None, 4, tm // 4, A_COLS), fold_tile),
                   pl.BlockSpec((None, 16, tm // 16, A_COLS), fold_tile),
                   wide_spec, wide_spec, wide_spec, wide_spec),
        out_shape=out_shape,
        scratch_shapes=[pltpu.VMEM((8, LANES), F32),
                        pltpu.VMEM((A_COLS // LANES, tm, LANES), F32)],
        compiler_params=_params(("arbitrary", "arbitrary")),
        name="inproj",
    )(x, g, w, fb, pq, pk, cq, ck)


def _banded_kernel(*refs, n_tiles, with_sink, with_lse):
    refs = list(refs)
    q_ref, k_ref, v_ref, bias_ref = refs[:4]
    rest = refs[4:]
    sink_ref = rest.pop(0) if with_sink else None
    o_ref = rest.pop(0)
    lse_ref = rest.pop(0) if with_lse else None
    pair = pl.program_id(2)
    low = lax.broadcasted_iota(jnp.int32, (BAND, LANES), 1) < HEAD_DIM
    keep = (low.astype(F32).astype(BF16), (1.0 - low.astype(F32)).astype(BF16))

    def body(n, carry):
        r0 = pl.multiple_of(n * BAND, BAND)
        k0 = pl.multiple_of(jnp.maximum(r0 - BAND, 0), BAND)
        variant = jnp.where(n == 0, 1, 0)
        qt = q_ref[pl.ds(r0, BAND), :]
        kt = k_ref[pl.ds(k0, 2 * BAND), :]
        vt = v_ref[pl.ds(k0, 2 * BAND), :]
        outs, lses = [], []
        for hh in range(2):
            s = _dot_nt(qt * keep[hh], kt) + bias_ref[hh, variant]
            m = jnp.max(s, axis=1, keepdims=True)
            if with_sink:
                sink = sink_ref[2 * pair + hh]
                m = jnp.maximum(m, sink)
            p = jnp.exp(s - m)
            l = jnp.sum(p, axis=1, keepdims=True)
            if with_sink:
                l = l + jnp.exp(sink - m)
            outs.append(_dot(p.astype(BF16), vt) / l)
            lses.append(jnp.broadcast_to(m + jnp.log(l), (BAND, LANES)))
        o_ref[pl.ds(r0, BAND), :] = jnp.where(low, outs[0], outs[1])
        if with_lse:
            lse_ref[pl.ds(r0, BAND), :] = jnp.where(low, lses[0], lses[1])
        return carry

    lax.fori_loop(0, n_tiles, body, 0, unroll=min(BANDED_UNROLL, n_tiles))


def _banded(src, bias, sinks, *, q_blk, k_blk, v_blk, with_lse, name):
    B, dil, L, _ = src.shape
    col = lambda blk: (lambda b, r, p: (b, r, 0, blk + p))
    in_specs = [pl.BlockSpec((None, None, L, LANES), col(q_blk)),
                pl.BlockSpec((None, None, L, LANES), col(k_blk)),
                pl.BlockSpec((None, None, L, LANES), col(v_blk)),
                pl.BlockSpec((2, 2, BAND, 2 * BAND), lambda b, r, p: (p, 0, 0, 0))]
    args = [src, src, src, bias]
    if sinks is not None:
        in_specs.append(pl.BlockSpec(memory_space=pltpu.SMEM))
        args.append(sinks)
    o_spec = pl.BlockSpec((None, None, L, LANES), lambda b, r, p: (b, r, 0, p))
    o_shape = jax.ShapeDtypeStruct((B, dil, L, GROUP_W), F32)
    n_out = 2 if with_lse else 1
    return pl.pallas_call(
        functools.partial(_banded_kernel, n_tiles=L // BAND, with_sink=sinks is not None, with_lse=with_lse),
        grid=(B, dil, 2),
        in_specs=in_specs,
        out_specs=(o_spec,) * n_out,
        out_shape=(o_shape,) * n_out,
        compiler_params=_params(("arbitrary",) * 3),
        name=name,
    )(*args)


def _online_step(pieces, vt, m, l, acc):
    m_new = m
    for s in pieces:
        m_new = jnp.maximum(m_new, jnp.max(s, axis=1, keepdims=True))
    alpha = jnp.exp(m - m_new)
    ps = [jnp.exp(s - m_new) for s in pieces]
    l = alpha * l
    for p in ps:
        l = l + jnp.sum(p, axis=1, keepdims=True)
    p_all = ps[0] if len(ps) == 1 else jnp.concatenate(ps, axis=1)
    acc = alpha * acc + _dot(p_all.astype(BF16), vt)
    return m_new, l, acc


def _flash_init(tq):
    one = (jnp.full((tq, 1), NEG, F32), jnp.zeros((tq, 1), F32), jnp.zeros((tq, LANES), F32))
    return one + one


def _finish_pair(o_ref, carry, tq):
    low = lax.broadcasted_iota(jnp.int32, (tq, LANES), 1) < HEAD_DIM
    _, l0, acc0, _, l1, acc1 = carry
    o_ref[...] = jnp.where(low, acc0 / l0, acc1 / l1)


def _moba_kernel(q_ref, k_ref, v_ref, bias_ref, shift_ref, o_ref, kmean_ref, *, n_blocks):
    i = pl.program_id(2)
    tq, blk = FLASH_TQ, MOBA_BLOCK

    @pl.when(i == 0)
    def _():
        kmean_ref[...] = jnp.zeros_like(kmean_ref)
        for hh in range(2):
            kf = k_ref[:, hh * LANES:(hh + 1) * LANES].astype(F32)
            kmean_ref[hh, 0:n_blocks, :] = jnp.mean(kf.reshape(n_blocks, blk, LANES), axis=1)

    lane = lax.broadcasted_iota(jnp.int32, (tq, LANES), 1)
    lane_f = lane.astype(F32)
    q_aug = []
    for hh in range(2):
        q = q_ref[:, hh * LANES:(hh + 1) * LANES]
        km = kmean_ref[hh]
        k1 = km.astype(BF16)
        r1 = km - k1.astype(F32)
        k2 = r1.astype(BF16)
        k3 = (r1 - k2.astype(F32)).astype(BF16)
        gate = _dot_nt(q, k1) + _dot_nt(q, k2) + _dot_nt(q, k3)

        avail = lane < i
        sel = jnp.zeros((tq, LANES), jnp.bool_)
        for _ in range(MOBA_TOPK):
            cur = jnp.where(avail, gate, -jnp.inf)
            top = jnp.max(cur, axis=1, keepdims=True)
            is_top = avail & (cur == top)
            first = jnp.min(jnp.where(is_top, lane_f, float(LANES)), axis=1, keepdims=True)
            pick = lane_f == first
            sel = sel | pick
            avail = avail & jnp.logical_not(pick)
        allowed = sel | (lane == i)
        penalty = jnp.where(allowed | (lane >= n_blocks), 0.0, NEG).astype(BF16)
        q_aug.append((q.astype(F32) + _dot(penalty, shift_ref[...])).astype(BF16))

    def body(n, carry):
        r0 = pl.multiple_of(n * FLASH_TK, FLASH_TK)
        vt = v_ref[pl.ds(r0, FLASH_TK), :]
        d0 = i - 2 * n + 1
        new = []
        for hh in range(2):
            kt = k_ref[pl.ds(r0, FLASH_TK), hh * LANES:(hh + 1) * LANES]
            s = _dot_nt(q_aug[hh], kt)
            pieces = [s[:, :blk] + bias_ref[hh, d0], s[:, blk:] + bias_ref[hh, d0 - 1]]
            new.extend(_online_step(pieces, vt, *carry[3 * hh:3 * hh + 3]))
        return tuple(new)

    carry = lax.fori_loop(0, i // 2 + 1, body, _flash_init(tq))
    _finish_pair(o_ref, carry, tq)


def _moba(qb, kb, y, bias, shift):
    B, S, _ = qb.shape
    nb = S // MOBA_BLOCK
    assert nb <= AUG0 and FLASH_TQ == MOBA_BLOCK and FLASH_TK == 2 * MOBA_BLOCK and nb % 2 == 0
    return pl.pallas_call(
        functools.partial(_moba_kernel, n_blocks=nb),
        grid=(B, 2, nb),
        in_specs=[pl.BlockSpec((None, MOBA_BLOCK, 2 * LANES), lambda b, p, i: (b, i, p)),
                  pl.BlockSpec((None, S, 2 * LANES), lambda b, p, i: (b, 0, p)),
                  pl.BlockSpec((None, S, LANES), lambda b, p, i: (b, 0, VB_B + p)),
                  pl.BlockSpec((2, nb + 1, MOBA_BLOCK, MOBA_BLOCK), lambda b, p, i: (p, 0, 0, 0)),
                  pl.BlockSpec((LANES, LANES), lambda b, p, i: (0, 0))],
        out_specs=pl.BlockSpec((None, MOBA_BLOCK, LANES), lambda b, p, i: (b, i, p)),
        out_shape=jax.ShapeDtypeStruct((B, S, GROUP_W), F32),
        scratch_shapes=[pltpu.VMEM((2, LANES, LANES), F32)],
        compiler_params=_params(("arbitrary",) * 3),
        name="moba",
    )(qb, kb, y, bias, shift)


def _fox_kernel(q_ref, k_ref, v_ref, o_ref):
    i = pl.program_id(2)
    tq, tk = FLASH_TQ, FLASH_TK
    qs = [q_ref[:, hh * LANES:(hh + 1) * LANES] for hh in range(2)]

    def step(r0, carry, mask):
        vt = v_ref[pl.ds(r0, tk), :]
        new = []
        for hh in range(2):
            kt = k_ref[pl.ds(r0, tk), hh * LANES:(hh + 1) * LANES]
            s = _dot_nt(qs[hh], kt)
            if mask is not None:
                s = jnp.where(mask, s, NEG)
            new.extend(_online_step([s], vt, *carry[3 * hh:3 * hh + 3]))
        return tuple(new)

    n_full = (i * tq) // tk
    carry = lax.fori_loop(0, n_full, lambda n, c: step(pl.multiple_of(n * tk, tk), c, None), _flash_init(tq))
    row = lax.broadcasted_iota(jnp.int32, (tq, tk), 0)
    col = lax.broadcasted_iota(jnp.int32, (tq, tk), 1)
    causal = col <= row + (i * tq - n_full * tk)
    carry = step(pl.multiple_of(n_full * tk, tk), carry, causal)
    _finish_pair(o_ref, carry, tq)


def _fox(qd, kd, y):
    B, S, _ = qd.shape
    tq = FLASH_TQ
    assert S % FLASH_TK == 0 and FLASH_TK % tq == 0
    return pl.pallas_call(
        _fox_kernel,
        grid=(B, 2, S // tq),
        in_specs=[pl.BlockSpec((None, tq, 2 * LANES), lambda b, p, i: (b, i, p)),
                  pl.BlockSpec((None, S, 2 * LANES), lambda b, p, i: (b, 0, p)),
                  pl.BlockSpec((None, S, LANES), lambda b, p, i: (b, 0, VD_B + p))],
        out_specs=pl.BlockSpec((None, tq, LANES), lambda b, p, i: (b, i, p)),
        out_shape=jax.ShapeDtypeStruct((B, S, GROUP_W), F32),
        compiler_params=_params(("arbitrary",) * 3),
        name="fox",
    )(qd, kd, y)


def _outffn_kernel(x_ref, oa1_ref, oa4_ref, oa16_ref, la1_ref, la4_ref, la16_ref, ob_ref, oc_ref, od_ref,
                   mg_ref, wo_ref, g2_ref, wg_ref, wu_ref, wd_ref, fg_ref, out_ref, unfold_ref, *, final, tm):
    def unfold(ref, dil, slot):
        for r in range(dil):
            for s in range(GROUP_W // LANES):
                unfold_ref[slot, s, pl.ds(r, tm // dil, stride=dil), :] = ref[r, :, s * LANES:(s + 1) * LANES]
        return jnp.concatenate([unfold_ref[slot, s] for s in range(GROUP_W // LANES)], axis=1)

    l1, l4, l16 = la1_ref[...], unfold(la4_ref, 4, 0), unfold(la16_ref, 16, 1)
    o1, o4, o16 = oa1_ref[...], unfold(oa4_ref, 4, 2), unfold(oa16_ref, 16, 3)
    m = jnp.maximum(jnp.maximum(l1, l4), l16)
    e1, e4, e16 = jnp.exp(l1 - m), jnp.exp(l4 - m), jnp.exp(l16 - m)
    mix_a = (e1 * o1 + e4 * o4 + e16 * o16) / (e1 + e4 + e16)

    x1 = x_ref[...]
    for g, part in enumerate((mix_a, ob_ref[...], oc_ref[...], od_ref[...])):
        c0 = g * GROUP_W
        normed = _rms(part, mg_ref[:, c0:c0 + GROUP_W]).astype(BF16)
        x1 = x1 + _dot(normed, wo_ref[c0:c0 + GROUP_W, :])

    h = _rms(x1, g2_ref[...]).astype(BF16)
    ffn = jnp.zeros_like(x1)
    for c in range(D_FF // FF_CHUNK):
        c0 = c * FF_CHUNK
        gate = _dot(h, wg_ref[:, c0:c0 + FF_CHUNK])
        up = _dot(h, wu_ref[:, c0:c0 + FF_CHUNK])
        act = (gate * (1.0 / (1.0 + jnp.exp(-gate))) * up).astype(BF16)
        ffn = ffn + _dot(act, wd_ref[c0:c0 + FF_CHUNK, :])
    x2 = x1 + ffn
    if final:
        x2 = _rms(x2, fg_ref[...])
    out_ref[...] = x2


def _outffn(x, a_parts, ob, oc, od, mg, wo, g2, wg, wu, wd, fg, *, final):
    B, S, D = x.shape
    tm = PROJ_TM
    tile = lambda b, t: (b, t, 0)
    const = lambda b, t: (0, 0)
    flat_spec = pl.BlockSpec((None, tm, GROUP_W), tile)

    def fold_spec(dil):
        if dil == 1:
            return pl.BlockSpec((None, None, tm, GROUP_W), lambda b, t: (b, 0, t, 0))
        return pl.BlockSpec((None, dil, tm // dil, GROUP_W), lambda b, t: (b, 0, t, 0))

    dils = [dil for _, dil in DILATED_GROUPS]
    a_specs = [fold_spec(d) for d in dils] * 2
    a_args = [p[0] for p in a_parts] + [p[1] for p in a_parts]
    return pl.pallas_call(
        functools.partial(_outffn_kernel, final=final, tm=tm),
        grid=(B, S // tm),
        in_specs=[pl.BlockSpec((None, tm, D), tile)] + a_specs + [flat_spec] * 3 + [
            _resident((1, D), const),
            _resident((D, D), const),
            _resident((1, D), const),
            _resident((D, D_FF), const),
            _resident((D, D_FF), const),
            _resident((D_FF, D), const),
            _resident((1, D), const)],
        out_specs=pl.BlockSpec((None, tm, D), tile),
        out_shape=jax.ShapeDtypeStruct((B, S, D), F32),
        scratch_shapes=[pltpu.VMEM((4, GROUP_W // LANES, tm, LANES), F32)],
        compiler_params=_params(("arbitrary", "arbitrary")),
        name="outffn",
    )(x, *a_args, ob, oc, od, mg, wo, g2, wg, wu, wd, fg)


def kernel(x, norm1_g, w_in, f_bias, sinks, mix_norm_g, w_out, norm2_g, w_gate, w_up, w_down, rel_bias, final_g):
    B, S, D = x.shape
    depth = w_in.shape[0]
    max_dil = max(dil for _, dil in DILATED_GROUPS)
    assert D == D_MODEL and S % (2 * BAND * max_dil) == 0 and w_gate.shape[-1] == D_FF
    assert [dil for _, dil in DILATED_GROUPS] == [1, 4, 16] and S % PROJ_TM == 0

    w_proj = _rearrange_w_in(w_in)
    wo_b, wg_b, wu_b, wd_b = (w.astype(BF16) for w in (w_out, w_gate, w_up, w_down))
    fb = jnp.pad(f_bias, ((0, 0), (0, LANES - N_HEADS)))[:, None, :]
    consts = _placement_constants()
    table_a = rel_bias[:, :N_HEADS]
    table_b = rel_bias[:, N_HEADS:2 * N_HEADS]
    table_c = rel_bias[:, 2 * N_HEADS:]
    bias_a = [_band_bias(table_a, dil, window // dil, f"dilated{dil}_bias") for window, dil in DILATED_GROUPS]
    bias_c = _band_bias(table_c, 1, SWA_WINDOW - 1, "swa_bias")
    bias_b = _moba_bias(table_b, S // MOBA_BLOCK)

    for l in range(depth):
        y, ya4, ya16, qb, kb, qd, kd = _inproj(x, norm1_g[l][None], w_proj[l], fb[l], consts)
        y4 = y.reshape(B, 1, S, NY)
        a_parts = [_banded(src, bias_a[g], None, q_blk=qb_, k_blk=kb_, v_blk=vb_, with_lse=True,
                           name=f"dilated{dil}")
                   for g, ((_, dil), src, (qb_, kb_, vb_)) in enumerate(zip(
                       DILATED_GROUPS, (y4, ya4, ya16), ((QA_B, KA_B, VA_B), (0, 2, 4), (0, 2, 4))))]
        (o_c,) = _banded(y4, bias_c, sinks[l], q_blk=QC_B, k_blk=KC_B, v_blk=VC_B, with_lse=False, name="swa")
        o_b = _moba(qb, kb, y, bias_b, consts[4])
        o_d = _fox(qd, kd, y)
        x = _outffn(x, a_parts, o_b, o_c.reshape(B, S, GROUP_W), o_d, mix_norm_g[l][None], wo_b[l],
                    norm2_g[l][None], wg_b[l], wu_b[l], wd_b[l], final_g[None], final=(l == depth - 1))
    return x
```

```python
import functools
import math

import numpy as np
import jax
import jax.numpy as jnp
from jax import lax
from jax.experimental import pallas as pl
from jax.experimental.pallas import tpu as pltpu

F32 = jnp.float32
BF16 = jnp.bfloat16

LANES = 128
VMEM_LIMIT_BYTES = 56 * 1024 * 1024

D_MODEL = 1024
HEAD_DIM = 64
N_HEADS = 4
GROUP_W = N_HEADS * HEAD_DIM
DILATED_GROUPS = ((128, 1), (512, 4), (2048, 16))
BAND = 128
MOBA_BLOCK = 256
MOBA_TOPK = 3
SWA_WINDOW = 128
N_BUCKETS = 32
T5_MAX_DISTANCE = 2048
D_FF = 2816
FF_CHUNK = 256
EPS = 1e-6
NEG = -1e30
SCALE = HEAD_DIM ** -0.5

LOG2E = math.log2(math.e)
QA_B, KA_B, VA_B, QC_B, KC_B, VC_B = 0, 2, 4, 6, 8, 10
NY = 12 * LANES
A_COLS = 3 * GROUP_W
WIDE = N_HEADS * LANES
QB_OFF, KB_OFF, VB_OFF, QD_OFF, KD_OFF, VD_OFF = (NY + g * GROUP_W for g in range(6))
FD_OFF = NY + 6 * GROUP_W
N_PROJ = FD_OFF + LANES
AUG0 = HEAD_DIM

PROJ_TM = 512
FLASH_T = 512
FOX_TQ, FOX_TK = 512, 512
BANDED_UNROLL = 4


def _params(sem):
    return pltpu.CompilerParams(dimension_semantics=sem, vmem_limit_bytes=VMEM_LIMIT_BYTES)


def _resident(shape, index_map):
    return pl.BlockSpec(shape, index_map, pipeline_mode=pl.Buffered(1))


def _dot(a, b):
    return jnp.dot(a, b, preferred_element_type=F32)


def _dot_nt(a, b):
    return lax.dot_general(a, b, (((1,), (1,)), ((), ())), preferred_element_type=F32)


def _rms(x, g):
    return x * lax.rsqrt(jnp.mean(x * x, axis=-1, keepdims=True) + EPS) * g


def _rearrange_w_in(w):
    w = w.astype(BF16)
    sl = lambda a, b: w[..., a:b]
    kc0, kc1 = sl(1792, 1856), sl(1856, 1920)
    vc0, vc1 = sl(1920, 1984), sl(1984, 2048)
    fd = jnp.pad(sl(2816, 2820), [(0, 0)] * (w.ndim - 1) + [(0, LANES - N_HEADS)])
    out = jnp.concatenate([
        sl(0, 768),
        sl(1536, 1792),
        kc0, kc0, kc1, kc1,
        vc0, vc0, vc1, vc1,
        sl(768, 1536),
        sl(2048, 2816),
        fd], axis=-1)
    assert out.shape[-1] == N_PROJ
    return out


def _t5_bucket_np(dist):
    n = np.maximum(dist, 0)
    max_exact = N_BUCKETS // 2
    nf = np.maximum(n, 1).astype(np.float32)
    large = max_exact + (np.log(nf / np.float32(max_exact)) / np.float32(math.log(T5_MAX_DISTANCE / max_exact))
                         * np.float32(N_BUCKETS - max_exact)).astype(np.int32)
    large = np.minimum(large, N_BUCKETS - 1)
    return np.where(n < max_exact, n, large).astype(np.int32)


def _bucket_thresholds():
    n = np.arange(1 << 16)
    bucket = _t5_bucket_np(n)
    assert (np.diff(bucket) >= 0).all() and bucket[-1] == N_BUCKETS - 1
    thr = np.searchsorted(bucket, np.arange(N_BUCKETS), side="left")
    assert (np.searchsorted(thr, n, side="right") - 1 == bucket).all()
    return tuple(int(t) for t in thr)


def _placement_constants():
    pq = np.zeros((3 * LANES, WIDE), np.float32)
    pk = np.zeros((3 * LANES, WIDE), np.float32)
    cq = np.zeros((1, WIDE), np.float32)
    ck = np.zeros((1, WIDE), np.float32)
    for h in range(N_HEADS):
        for part in range(3):
            pq[part * LANES + h, h * LANES + AUG0 + part] = 1.0
            pk[part * LANES + h, h * LANES + AUG0 + 3 + part] = -1.0
            ck[0, h * LANES + AUG0 + part] = 1.0
            cq[0, h * LANES + AUG0 + 3 + part] = 1.0
    shift = np.zeros((LANES, LANES), np.float32)
    for n in range(AUG0):
        shift[n, AUG0 + n] = 1.0
    return (jnp.asarray(pq, BF16), jnp.asarray(pk, BF16), jnp.asarray(cq), jnp.asarray(ck),
            jnp.asarray(shift, BF16))


def _bias_tiles_kernel(base_ref, table_ref, o_ref, *, rows, cols, step, max_dist, scale, thresholds, chunk):
    t = pl.program_id(0)
    base = base_ref[t]

    def body(c, carry):
        r0 = pl.multiple_of(c * chunk, chunk)
        i = lax.broadcasted_iota(jnp.int32, (chunk, cols), 0) + r0
        j = lax.broadcasted_iota(jnp.int32, (chunk, cols), 1)
        dist = base + i - j
        ok = (dist >= 0) & (dist <= max_dist)
        scaled = dist * step
        vals = [jnp.full((chunk, cols), table_ref[0, h] * scale, F32) for h in range(N_HEADS)]
        for b in range(1, N_BUCKETS):
            reached = scaled >= thresholds[b]
            vals = [jnp.where(reached, table_ref[b, h] * scale, v) for h, v in enumerate(vals)]
        for h in range(N_HEADS):
            o_ref[h, pl.ds(r0, chunk), :] = jnp.where(ok, vals[h], NEG)
        return carry

    lax.fori_loop(0, rows // chunk, body, 0)


def _bias_tiles(table, bases, *, rows, cols, step, max_dist, name, scale=1.0):
    n_tiles = len(bases)
    return pl.pallas_call(
        functools.partial(_bias_tiles_kernel, rows=rows, cols=cols, step=step, max_dist=max_dist, scale=scale,
                          thresholds=_bucket_thresholds(), chunk=32),
        grid=(n_tiles,),
        in_specs=[pl.BlockSpec(memory_space=pltpu.SMEM), pl.BlockSpec(memory_space=pltpu.SMEM)],
        out_specs=pl.BlockSpec((N_HEADS, None, rows, cols), lambda t: (0, t, 0, 0)),
        out_shape=jax.ShapeDtypeStruct((N_HEADS, n_tiles, rows, cols), F32),
        compiler_params=_params(("arbitrary",)),
        name=name,
    )(jnp.asarray(bases, jnp.int32), table)


def _band_bias(table, step, max_dist, name):
    return _bias_tiles(table, (BAND, 0), rows=BAND, cols=2 * BAND, step=step, max_dist=max_dist, name=name)


def _moba_bias(table, n_blocks):
    bases = tuple(MOBA_BLOCK * d for d in range(-1, n_blocks))
    return _bias_tiles(table, bases, rows=MOBA_BLOCK, cols=MOBA_BLOCK, step=1, max_dist=1 << 30,
                       name="moba_bias", scale=LOG2E)


def _widen(a, low):
    outs = []
    for p in range(GROUP_W // LANES):
        pair = a[:, p * LANES:(p + 1) * LANES]
        outs.append(jnp.where(low, pair, 0.0))
        outs.append(jnp.where(low, pltpu.roll(pair, HEAD_DIM, axis=1), 0.0))
    return jnp.concatenate(outs, axis=1)


def _inproj_kernel(x_ref, g_ref, w_ref, fb_ref, pq_ref, pk_ref, cq_ref, ck_ref,
                   y_ref, ya4_ref, ya16_ref, qb_ref, kb_ref, vb_ref, qd_ref, kd_ref, vd_ref,
                   carry_ref, fold_ref, *, tm):
    t = pl.program_id(1)

    @pl.when(t == 0)
    def _():
        carry_ref[...] = jnp.zeros_like(carry_ref)

    h = _rms(x_ref[...], g_ref[...]).astype(BF16)

    def mm(c0, width=GROUP_W):
        return _dot(h, w_ref[:, c0:c0 + width])

    for grp in range(NY // GROUP_W):
        c0 = grp * GROUP_W
        a = mm(c0)
        if c0 in (QA_B * LANES, QC_B * LANES):
            a = a * SCALE
        y_ref[:, c0:c0 + GROUP_W] = a.astype(BF16)
        if c0 < A_COLS:
            fold_ref[2 * grp] = a[:, :LANES]
            fold_ref[2 * grp + 1] = a[:, LANES:]

    for dil, ref in ((4, ya4_ref), (16, ya16_ref)):
        for r in range(dil):
            for slab in range(A_COLS // LANES):
                piece = fold_ref[slab, pl.ds(r, tm // dil, stride=dil), :]
                ref[r, :, slab * LANES:(slab + 1) * LANES] = piece.astype(BF16)

    low = lax.broadcasted_iota(jnp.int32, (tm, LANES), 1) < HEAD_DIM
    lane = lax.broadcasted_iota(jnp.int32, (tm, WIDE), 1) % LANES
    row = lax.broadcasted_iota(jnp.int32, (tm, WIDE), 0)

    qb_ref[...] = _widen(mm(QB_OFF) * (SCALE * LOG2E), low).astype(BF16)
    vb_ref[...] = jnp.where(lane == AUG0, 1.0, _widen(mm(VB_OFF), low)).astype(BF16)
    vd_ref[...] = jnp.where(lane == AUG0, 1.0, _widen(mm(VD_OFF), low)).astype(BF16)
    blk = (t * tm + row) // MOBA_BLOCK
    kb_ref[...] = jnp.where(lane == AUG0 + blk, 1.0, _widen(mm(KB_OFF), low)).astype(BF16)

    f = mm(FD_OFF, LANES) + fb_ref[...]
    logf = jnp.minimum(f, 0.0) - jnp.log(1.0 + jnp.exp(-jnp.abs(f)))
    srow = lax.broadcasted_iota(jnp.int32, (tm, LANES), 0)
    c = logf
    k = 1
    while k < tm:
        c = c + jnp.where(srow >= k, pltpu.roll(c, k, axis=0), 0.0)
        k *= 2
    c = c + carry_ref[0:1, :]
    carry_ref[0:1, :] = c[tm - 1:tm, :]

    c2 = c * LOG2E
    c_hi = c2.astype(BF16)
    r1 = c2 - c_hi.astype(F32)
    c_mid = r1.astype(BF16)
    c_lo = (r1 - c_mid.astype(F32)).astype(BF16)
    feats = jnp.concatenate([c_hi, c_mid, c_lo], axis=1)
    qd = _widen(mm(QD_OFF) * (SCALE * LOG2E), low) + _dot(feats, pq_ref[...]) + cq_ref[...]
    kd = _widen(mm(KD_OFF), low) + _dot(feats, pk_ref[...]) + ck_ref[...]
    qd_ref[...] = qd.astype(BF16)
    kd_ref[...] = kd.astype(BF16)


def _inproj(x, g, w, fb, consts):
    B, S, D = x.shape
    tm = PROJ_TM
    pq, pk, cq, ck, _ = consts
    const = lambda b, t: (0, 0)
    tile = lambda b, t: (b, t, 0)
    fold_tile = lambda b, t: (b, 0, t, 0)
    wide_shape = jax.ShapeDtypeStruct((B, S, WIDE), BF16)
    out_shape = (jax.ShapeDtypeStruct((B, S, NY), BF16),
                 jax.ShapeDtypeStruct((B, 4, S // 4, A_COLS), BF16),
                 jax.ShapeDtypeStruct((B, 16, S // 16, A_COLS), BF16),
                 ) + (wide_shape,) * 6
    wide_spec = pl.BlockSpec((None, tm, WIDE), tile)
    return pl.pallas_call(
        functools.partial(_inproj_kernel, tm=tm),
        grid=(B, S // tm),
        in_specs=[pl.BlockSpec((None, tm, D), tile),
                  _resident((1, D), const),
                  _resident((D, N_PROJ), const),
                  _resident((1, LANES), const),
                  _resident((3 * LANES, WIDE), const),
                  _resident((3 * LANES, WIDE), const),
                  _resident((1, WIDE), const),
                  _resident((1, WIDE), const)],
        out_specs=(pl.BlockSpec((None, tm, NY), tile),
                   pl.BlockSpec((None, 4, tm // 4, A_COLS), fold_tile),
                   pl.BlockSpec((None, 16, tm // 16, A_COLS), fold_tile),
                   ) + (wide_spec,) * 6,
        out_shape=out_shape,
        scratch_shapes=[pltpu.VMEM((8, LANES), F32),
                        pltpu.VMEM((A_COLS // LANES, tm, LANES), F32)],
        compiler_params=_params(("arbitrary", "arbitrary")),
        name="inproj",
    )(x, g, w, fb, pq, pk, cq, ck)


def _banded_kernel(*refs, dil, n_tiles, with_sink, with_lse):
    refs = list(refs)
    q_ref, k_ref, v_ref, bias_ref = refs[:4]
    rest = refs[4:]
    sink_ref = rest.pop(0) if with_sink else None
    o_ref = rest.pop(0)
    lse_ref = rest.pop(0) if with_lse else None
    pair = pl.program_id(1)
    low = lax.broadcasted_iota(jnp.int32, (BAND, LANES), 1) < HEAD_DIM
    keep = (low.astype(F32).astype(BF16), (1.0 - low.astype(F32)).astype(BF16))

    def body(it, carry):
        r = it // n_tiles
        n = it % n_tiles
        r0 = pl.multiple_of(n * BAND, BAND)
        k0 = pl.multiple_of(jnp.maximum(r0 - BAND, 0), BAND)
        variant = jnp.where(n == 0, 1, 0)
        qt = q_ref[r, pl.ds(r0, BAND), :]
        kt = k_ref[r, pl.ds(k0, 2 * BAND), :]
        vt = v_ref[r, pl.ds(k0, 2 * BAND), :]
        outs, lses = [], []
        for hh in range(2):
            s = _dot_nt(qt * keep[hh], kt) + bias_ref[hh, variant]
            m = jnp.max(s, axis=1, keepdims=True)
            if with_sink:
                sink = sink_ref[2 * pair + hh]
                m = jnp.maximum(m, sink)
            p = jnp.exp(s - m)
            l = jnp.sum(p, axis=1, keepdims=True)
            if with_sink:
                l = l + jnp.exp(sink - m)
            outs.append(_dot(p.astype(BF16), vt) / l)
            lses.append(jnp.broadcast_to(m + jnp.log(l), (BAND, LANES)))
        o_ref[r, pl.ds(r0, BAND), :] = jnp.where(low, outs[0], outs[1])
        if with_lse:
            lse_ref[r, pl.ds(r0, BAND), :] = jnp.where(low, lses[0], lses[1])
        return carry

    lax.fori_loop(0, dil * n_tiles, body, 0, unroll=BANDED_UNROLL)


def _banded(src, bias, sinks, *, q_blk, k_blk, v_blk, with_lse, name):
    B, dil, L, _ = src.shape
    n_tiles = L // BAND
    assert n_tiles & (n_tiles - 1) == 0 and (dil * n_tiles) % BANDED_UNROLL == 0
    col = lambda blk: (lambda b, p: (b, 0, 0, blk + p))
    in_specs = [pl.BlockSpec((None, dil, L, LANES), col(q_blk)),
                pl.BlockSpec((None, dil, L, LANES), col(k_blk)),
                pl.BlockSpec((None, dil, L, LANES), col(v_blk)),
                pl.BlockSpec((2, 2, BAND, 2 * BAND), lambda b, p: (p, 0, 0, 0))]
    args = [src, src, src, bias]
    if sinks is not None:
        in_specs.append(pl.BlockSpec(memory_space=pltpu.SMEM))
        args.append(sinks)
    o_spec = pl.BlockSpec((None, dil, L, LANES), lambda b, p: (b, 0, 0, p))
    o_shape = jax.ShapeDtypeStruct((B, dil, L, GROUP_W), F32)
    n_out = 2 if with_lse else 1
    return pl.pallas_call(
        functools.partial(_banded_kernel, dil=dil, n_tiles=n_tiles, with_sink=sinks is not None,
                          with_lse=with_lse),
        grid=(B, 2),
        in_specs=in_specs,
        out_specs=(o_spec,) * n_out,
        out_shape=(o_shape,) * n_out,
        compiler_params=_params(("arbitrary",) * 2),
        name=name,
    )(*args)


def _online_update(scores, vts, carry):
    n_heads = len(scores)
    ms = []
    for hh in range(n_heads):
        m_new = carry[2 * hh]
        for s in scores[hh]:
            m_new = jnp.maximum(m_new, jnp.max(s, axis=1, keepdims=True))
        ms.append(m_new)
    new = []
    for hh in range(n_heads):
        m, acc = carry[2 * hh:2 * hh + 2]
        ps = [jnp.exp2(s - ms[hh]).astype(BF16) for s in scores[hh]]
        p = ps[0] if len(ps) == 1 else jnp.concatenate(ps, axis=1)
        acc = jnp.exp2(m - ms[hh]) * acc + _dot(p, vts[hh])
        new.extend((ms[hh], acc))
    return tuple(new)


def _flash_init(tq, n_heads=2):
    return (jnp.full((tq, 1), NEG, F32), jnp.zeros((tq, LANES), F32)) * n_heads


def _finish_pair(o_ref, carry, tq):
    low = lax.broadcasted_iota(jnp.int32, (tq, LANES), 1) < HEAD_DIM
    outs = [acc / acc[:, AUG0:AUG0 + 1] for acc in (carry[1], carry[3])]
    o_ref[...] = jnp.where(low, outs[0], pltpu.roll(outs[1], HEAD_DIM, axis=1))


def _moba_kernel(q_ref, k_ref, v_ref, bias_ref, shift_ref, o_ref, kmean_ref, *, n_blocks):
    i = pl.program_id(2)
    t, blk = FLASH_T, MOBA_BLOCK

    @pl.when(i == 0)
    def _():
        for hh in range(2):
            kf = k_ref[:, hh * LANES:(hh + 1) * LANES].astype(F32)
            kmean_ref[hh] = jnp.mean(kf.reshape(n_blocks, blk, LANES), axis=1)

    blk_id = lax.broadcasted_iota(jnp.int32, (n_blocks, t), 0)
    blk_f = blk_id.astype(F32)
    own = 2 * i + lax.broadcasted_iota(jnp.int32, (n_blocks, t), 1) // blk
    q_aug = []
    for hh in range(2):
        q = q_ref[:, hh * LANES:(hh + 1) * LANES]
        km = kmean_ref[hh]
        k1 = km.astype(BF16)
        r1 = km - k1.astype(F32)
        k2 = r1.astype(BF16)
        k3 = (r1 - k2.astype(F32)).astype(BF16)
        gate = _dot_nt(k1, q) + _dot_nt(k2, q) + _dot_nt(k3, q)

        avail = blk_id < own
        sel = jnp.zeros((n_blocks, t), jnp.bool_)
        for _ in range(MOBA_TOPK):
            cur = jnp.where(avail, gate, -jnp.inf)
            top = jnp.max(cur, axis=0, keepdims=True)
            is_top = avail & (cur == top)
            first = jnp.min(jnp.where(is_top, blk_f, float(n_blocks)), axis=0, keepdims=True)
            pick = blk_f == first
            sel = sel | pick
            avail = avail & jnp.logical_not(pick)
        penalty = jnp.where(sel | (blk_id == own), 0.0, NEG).astype(BF16)
        placed = lax.dot_general(penalty, shift_ref[0:n_blocks, :], (((0,), (0,)), ((), ())),
                                 preferred_element_type=F32)
        q_aug.append((q.astype(F32) + placed).astype(BF16))

    def body(n, carry):
        r0 = pl.multiple_of(n * t, t)
        d = 2 * (i - n) + 1
        scores, vts = [], []
        for hh in range(2):
            kt = k_ref[pl.ds(r0, t), hh * LANES:(hh + 1) * LANES]
            s = _dot_nt(q_aug[hh], kt)
            top = jnp.concatenate([s[:blk, :blk] + bias_ref[hh, d], s[:blk, blk:] + bias_ref[hh, d - 1]], axis=1)
            bot = jnp.concatenate([s[blk:, :blk] + bias_ref[hh, d + 1], s[blk:, blk:] + bias_ref[hh, d]], axis=1)
            scores.append([jnp.concatenate([top, bot], axis=0)])
            vts.append(v_ref[pl.ds(r0, t), hh * LANES:(hh + 1) * LANES])
        return _online_update(scores, vts, carry)

    carry = lax.fori_loop(0, i + 1, body, _flash_init(t))
    _finish_pair(o_ref, carry, t)


def _moba(qb, kb, vb, bias, shift):
    B, S, _ = qb.shape
    nb = S // MOBA_BLOCK
    t = FLASH_T
    assert nb <= AUG0 and t == 2 * MOBA_BLOCK and S % t == 0
    return pl.pallas_call(
        functools.partial(_moba_kernel, n_blocks=nb),
        grid=(B, 2, S // t),
        in_specs=[pl.BlockSpec((None, t, 2 * LANES), lambda b, p, i: (b, i, p)),
                  pl.BlockSpec((None, S, 2 * LANES), lambda b, p, i: (b, 0, p)),
                  pl.BlockSpec((None, S, 2 * LANES), lambda b, p, i: (b, 0, p)),
                  pl.BlockSpec((2, nb + 1, MOBA_BLOCK, MOBA_BLOCK), lambda b, p, i: (p, 0, 0, 0)),
                  pl.BlockSpec((LANES, LANES), lambda b, p, i: (0, 0))],
        out_specs=pl.BlockSpec((None, t, LANES), lambda b, p, i: (b, i, p)),
        out_shape=jax.ShapeDtypeStruct((B, S, GROUP_W), F32),
        scratch_shapes=[pltpu.VMEM((2, nb, LANES), F32)],
        compiler_params=_params(("arbitrary",) * 3),
        name="moba",
    )(qb, kb, vb, bias, shift)


def _fox_kernel(q_ref, k_ref, v_ref, o_ref, *, tq, tk):
    i = pl.program_id(2)
    qs = [q_ref[:, hh * LANES:(hh + 1) * LANES] for hh in range(2)]

    def step(r0, carry, mask):
        scores, vts = [], []
        for hh in range(2):
            kt = k_ref[pl.ds(r0, tk), hh * LANES:(hh + 1) * LANES]
            s = _dot_nt(qs[hh], kt)
            if mask is not None:
                s = jnp.where(mask, s, NEG)
            scores.append([s])
            vts.append(v_ref[pl.ds(r0, tk), hh * LANES:(hh + 1) * LANES])
        return _online_update(scores, vts, carry)

    n_full = (i * tq) // tk
    carry = lax.fori_loop(0, n_full, lambda n, c: step(pl.multiple_of(n * tk, tk), c, None), _flash_init(tq))
    row = lax.broadcasted_iota(jnp.int32, (tq, tk), 0)
    col = lax.broadcasted_iota(jnp.int32, (tq, tk), 1)
    causal = col <= row + (i * tq - n_full * tk)
    carry = step(pl.multiple_of(n_full * tk, tk), carry, causal)
    _finish_pair(o_ref, carry, tq)


def _fox(qd, kd, vd):
    B, S, _ = qd.shape
    tq, tk = FOX_TQ, FOX_TK
    assert S % tk == 0 and tk % tq == 0
    return pl.pallas_call(
        functools.partial(_fox_kernel, tq=tq, tk=tk),
        grid=(B, 2, S // tq),
        in_specs=[pl.BlockSpec((None, tq, 2 * LANES), lambda b, p, i: (b, i, p)),
                  pl.BlockSpec((None, S, 2 * LANES), lambda b, p, i: (b, 0, p)),
                  pl.BlockSpec((None, S, 2 * LANES), lambda b, p, i: (b, 0, p))],
        out_specs=pl.BlockSpec((None, tq, LANES), lambda b, p, i: (b, i, p)),
        out_shape=jax.ShapeDtypeStruct((B, S, GROUP_W), F32),
        compiler_params=_params(("arbitrary",) * 3),
        name="fox",
    )(qd, kd, vd)


def _outffn_kernel(x_ref, oa1_ref, oa4_ref, oa16_ref, la1_ref, la4_ref, la16_ref, ob_ref, oc_ref, od_ref,
                   mg_ref, wo_ref, g2_ref, wg_ref, wu_ref, wd_ref, fg_ref, out_ref, unfold_ref, *, final, tm):
    def unfold(ref, dil, slot):
        for r in range(dil):
            for s in range(GROUP_W // LANES):
                unfold_ref[slot, s, pl.ds(r, tm // dil, stride=dil), :] = ref[r, :, s * LANES:(s + 1) * LANES]
        return jnp.concatenate([unfold_ref[slot, s] for s in range(GROUP_W // LANES)], axis=1)

    l1, l4, l16 = la1_ref[...], unfold(la4_ref, 4, 0), unfold(la16_ref, 16, 1)
    o1, o4, o16 = oa1_ref[...], unfold(oa4_ref, 4, 2), unfold(oa16_ref, 16, 3)
    m = jnp.maximum(jnp.maximum(l1, l4), l16)
    e1, e4, e16 = jnp.exp(l1 - m), jnp.exp(l4 - m), jnp.exp(l16 - m)
    mix_a = (e1 * o1 + e4 * o4 + e16 * o16) / (e1 + e4 + e16)

    x1 = x_ref[...]
    for g, part in enumerate((mix_a, ob_ref[...], oc_ref[...], od_ref[...])):
        c0 = g * GROUP_W
        normed = _rms(part, mg_ref[:, c0:c0 + GROUP_W]).astype(BF16)
        x1 = x1 + _dot(normed, wo_ref[c0:c0 + GROUP_W, :])

    h = _rms(x1, g2_ref[...]).astype(BF16)
    ffn = jnp.zeros_like(x1)
    for c in range(D_FF // FF_CHUNK):
        c0 = c * FF_CHUNK
        gate = _dot(h, wg_ref[:, c0:c0 + FF_CHUNK])
        up = _dot(h, wu_ref[:, c0:c0 + FF_CHUNK])
        act = (gate * (1.0 / (1.0 + jnp.exp(-gate))) * up).astype(BF16)
        ffn = ffn + _dot(act, wd_ref[c0:c0 + FF_CHUNK, :])
    x2 = x1 + ffn
    if final:
        x2 = _rms(x2, fg_ref[...])
    out_ref[...] = x2


def _outffn(x, a_parts, ob, oc, od, mg, wo, g2, wg, wu, wd, fg, *, final):
    B, S, D = x.shape
    tm = PROJ_TM
    tile = lambda b, t: (b, t, 0)
    const = lambda b, t: (0, 0)
    flat_spec = pl.BlockSpec((None, tm, GROUP_W), tile)

    def fold_spec(dil):
        if dil == 1:
            return pl.BlockSpec((None, None, tm, GROUP_W), lambda b, t: (b, 0, t, 0))
        return pl.BlockSpec((None, dil, tm // dil, GROUP_W), lambda b, t: (b, 0, t, 0))

    dils = [dil for _, dil in DILATED_GROUPS]
    a_specs = [fold_spec(d) for d in dils] * 2
    a_args = [p[0] for p in a_parts] + [p[1] for p in a_parts]
    return pl.pallas_call(
        functools.partial(_outffn_kernel, final=final, tm=tm),
        grid=(B, S // tm),
        in_specs=[pl.BlockSpec((None, tm, D), tile)] + a_specs + [flat_spec] * 3 + [
            _resident((1, D), const),
            _resident((D, D), const),
            _resident((1, D), const),
            _resident((D, D_FF), const),
            _resident((D, D_FF), const),
            _resident((D_FF, D), const),
            _resident((1, D), const)],
        out_specs=pl.BlockSpec((None, tm, D), tile),
        out_shape=jax.ShapeDtypeStruct((B, S, D), F32),
        scratch_shapes=[pltpu.VMEM((4, GROUP_W // LANES, tm, LANES), F32)],
        compiler_params=_params(("arbitrary", "arbitrary")),
        name="outffn",
    )(x, *a_args, ob, oc, od, mg, wo, g2, wg, wu, wd, fg)


def kernel(x, norm1_g, w_in, f_bias, sinks, mix_norm_g, w_out, norm2_g, w_gate, w_up, w_down, rel_bias, final_g):
    B, S, D = x.shape
    depth = w_in.shape[0]
    max_dil = max(dil for _, dil in DILATED_GROUPS)
    assert D == D_MODEL and S % (2 * BAND * max_dil) == 0 and w_gate.shape[-1] == D_FF
    assert [dil for _, dil in DILATED_GROUPS] == [1, 4, 16] and S % PROJ_TM == 0

    w_proj = _rearrange_w_in(w_in)
    wo_b, wg_b, wu_b, wd_b = (w.astype(BF16) for w in (w_out, w_gate, w_up, w_down))
    fb = jnp.pad(f_bias, ((0, 0), (0, LANES - N_HEADS)))[:, None, :]
    consts = _placement_constants()
    table_a = rel_bias[:, :N_HEADS]
    table_b = rel_bias[:, N_HEADS:2 * N_HEADS]
    table_c = rel_bias[:, 2 * N_HEADS:]
    bias_a = [_band_bias(table_a, dil, window // dil, f"dilated{dil}_bias") for window, dil in DILATED_GROUPS]
    bias_c = _band_bias(table_c, 1, SWA_WINDOW - 1, "swa_bias")
    bias_b = _moba_bias(table_b, S // MOBA_BLOCK)

    for l in range(depth):
        y, ya4, ya16, qb, kb, vb, qd, kd, vd = _inproj(x, norm1_g[l][None], w_proj[l], fb[l], consts)
        y4 = y.reshape(B, 1, S, NY)
        a_parts = [_banded(src, bias_a[g], None, q_blk=qb_, k_blk=kb_, v_blk=vb_, with_lse=True,
                           name=f"dilated{dil}")
                   for g, ((_, dil), src, (qb_, kb_, vb_)) in enumerate(zip(
                       DILATED_GROUPS, (y4, ya4, ya16), ((QA_B, KA_B, VA_B), (0, 2, 4), (0, 2, 4))))]
        (o_c,) = _banded(y4, bias_c, sinks[l], q_blk=QC_B, k_blk=KC_B, v_blk=VC_B, with_lse=False, name="swa")
        o_b = _moba(qb, kb, vb, bias_b, consts[4])
        o_d = _fox(qd, kd, vd)
        x = _outffn(x, a_parts, o_b, o_c.reshape(B, S, GROUP_W), o_d, mix_norm_g[l][None], wo_b[l],
                    norm2_g[l][None], wg_b[l], wu_b[l], wd_b[l], final_g[None], final=(l == depth - 1))
    return x
```

```python
import functools
import math

import numpy as np
import jax
import jax.numpy as jnp
from jax import lax
from jax.experimental import pallas as pl
from jax.experimental.pallas import tpu as pltpu

F32 = jnp.float32
BF16 = jnp.bfloat16

LANES = 128
VMEM_LIMIT_BYTES = 56 * 1024 * 1024

D_MODEL = 1024
HEAD_DIM = 64
N_HEADS = 4
GROUP_W = N_HEADS * HEAD_DIM
DILATED_GROUPS = ((128, 1), (512, 4), (2048, 16))
BAND = 128
MOBA_BLOCK = 256
MOBA_TOPK = 3
SWA_WINDOW = 128
N_BUCKETS = 32
T5_MAX_DISTANCE = 2048
D_FF = 2816
FF_CHUNK = 256
EPS = 1e-6
NEG = -1e30
SCALE = HEAD_DIM ** -0.5

LOG2E = math.log2(math.e)
QA_B, KA_B, VA_B, QC_B, KC_B, VC_B = 0, 2, 4, 6, 8, 10
NY = 12 * LANES
A_COLS = 3 * GROUP_W
WIDE = N_HEADS * LANES
QB_OFF, KB_OFF, VB_OFF, QD_OFF, KD_OFF, VD_OFF = (NY + g * GROUP_W for g in range(6))
FD_OFF = NY + 6 * GROUP_W
N_PROJ = FD_OFF + LANES
AUG0 = HEAD_DIM

PROJ_TM = 512
FLASH_T = 512
FOX_TQ, FOX_TK = 512, 512
BANDED_UNROLL = 8


def _params(sem):
    return pltpu.CompilerParams(dimension_semantics=sem, vmem_limit_bytes=VMEM_LIMIT_BYTES)


def _resident(shape, index_map):
    return pl.BlockSpec(shape, index_map, pipeline_mode=pl.Buffered(1))


def _dot(a, b):
    return jnp.dot(a, b, preferred_element_type=F32)


def _dot_nt(a, b):
    return lax.dot_general(a, b, (((1,), (1,)), ((), ())), preferred_element_type=F32)


def _rms(x, g):
    return x * lax.rsqrt(jnp.mean(x * x, axis=-1, keepdims=True) + EPS) * g


def _rearrange_w_in(w):
    w = w.astype(BF16)
    sl = lambda a, b: w[..., a:b]
    kc0, kc1 = sl(1792, 1856), sl(1856, 1920)
    vc0, vc1 = sl(1920, 1984), sl(1984, 2048)
    fd = jnp.pad(sl(2816, 2820), [(0, 0)] * (w.ndim - 1) + [(0, LANES - N_HEADS)])
    out = jnp.concatenate([
        sl(0, 768),
        sl(1536, 1792),
        kc0, kc0, kc1, kc1,
        vc0, vc0, vc1, vc1,
        sl(768, 1536),
        sl(2048, 2816),
        fd], axis=-1)
    assert out.shape[-1] == N_PROJ
    return out


def _t5_bucket_np(dist):
    n = np.maximum(dist, 0)
    max_exact = N_BUCKETS // 2
    nf = np.maximum(n, 1).astype(np.float32)
    large = max_exact + (np.log(nf / np.float32(max_exact)) / np.float32(math.log(T5_MAX_DISTANCE / max_exact))
                         * np.float32(N_BUCKETS - max_exact)).astype(np.int32)
    large = np.minimum(large, N_BUCKETS - 1)
    return np.where(n < max_exact, n, large).astype(np.int32)


def _bucket_thresholds():
    n = np.arange(1 << 16)
    bucket = _t5_bucket_np(n)
    assert (np.diff(bucket) >= 0).all() and bucket[-1] == N_BUCKETS - 1
    thr = np.searchsorted(bucket, np.arange(N_BUCKETS), side="left")
    assert (np.searchsorted(thr, n, side="right") - 1 == bucket).all()
    return tuple(int(t) for t in thr)


def _placement_constants():
    pq = np.zeros((3 * LANES, WIDE), np.float32)
    pk = np.zeros((3 * LANES, WIDE), np.float32)
    cq = np.zeros((1, WIDE), np.float32)
    ck = np.zeros((1, WIDE), np.float32)
    for h in range(N_HEADS):
        for part in range(3):
            pq[part * LANES + h, h * LANES + AUG0 + part] = 1.0
            pk[part * LANES + h, h * LANES + AUG0 + 3 + part] = -1.0
            ck[0, h * LANES + AUG0 + part] = 1.0
            cq[0, h * LANES + AUG0 + 3 + part] = 1.0
    shift = np.zeros((LANES, LANES), np.float32)
    for n in range(AUG0):
        shift[n, AUG0 + n] = 1.0
    return (jnp.asarray(pq, BF16), jnp.asarray(pk, BF16), jnp.asarray(cq), jnp.asarray(ck),
            jnp.asarray(shift, BF16))


def _bias_tiles_kernel(base_ref, table_ref, o_ref, *, rows, cols, step, max_dist, scale, thresholds, chunk):
    t = pl.program_id(0)
    base = base_ref[t]

    def body(c, carry):
        r0 = pl.multiple_of(c * chunk, chunk)
        i = lax.broadcasted_iota(jnp.int32, (chunk, cols), 0) + r0
        j = lax.broadcasted_iota(jnp.int32, (chunk, cols), 1)
        dist = base + i - j
        ok = (dist >= 0) & (dist <= max_dist)
        scaled = dist * step
        vals = [jnp.full((chunk, cols), table_ref[0, h] * scale, F32) for h in range(N_HEADS)]
        for b in range(1, N_BUCKETS):
            reached = scaled >= thresholds[b]
            vals = [jnp.where(reached, table_ref[b, h] * scale, v) for h, v in enumerate(vals)]
        for h in range(N_HEADS):
            o_ref[h, pl.ds(r0, chunk), :] = jnp.where(ok, vals[h], NEG)
        return carry

    lax.fori_loop(0, rows // chunk, body, 0)


def _bias_tiles(table, bases, *, rows, cols, step, max_dist, name, scale=1.0):
    n_tiles = len(bases)
    return pl.pallas_call(
        functools.partial(_bias_tiles_kernel, rows=rows, cols=cols, step=step, max_dist=max_dist, scale=scale,
                          thresholds=_bucket_thresholds(), chunk=32),
        grid=(n_tiles,),
        in_specs=[pl.BlockSpec(memory_space=pltpu.SMEM), pl.BlockSpec(memory_space=pltpu.SMEM)],
        out_specs=pl.BlockSpec((N_HEADS, None, rows, cols), lambda t: (0, t, 0, 0)),
        out_shape=jax.ShapeDtypeStruct((N_HEADS, n_tiles, rows, cols), F32),
        compiler_params=_params(("arbitrary",)),
        name=name,
    )(jnp.asarray(bases, jnp.int32), table)


def _band_bias(table, step, max_dist, name):
    return _bias_tiles(table, (BAND, 0), rows=BAND, cols=2 * BAND, step=step, max_dist=max_dist, name=name)


def _moba_bias(table, n_blocks):
    bases = tuple(MOBA_BLOCK * d for d in range(-1, n_blocks))
    return _bias_tiles(table, bases, rows=MOBA_BLOCK, cols=MOBA_BLOCK, step=1, max_dist=1 << 30,
                       name="moba_bias", scale=LOG2E)


def _widen(a, low):
    outs = []
    for p in range(GROUP_W // LANES):
        pair = a[:, p * LANES:(p + 1) * LANES]
        outs.append(jnp.where(low, pair, 0.0))
        outs.append(jnp.where(low, pltpu.roll(pair, HEAD_DIM, axis=1), 0.0))
    return jnp.concatenate(outs, axis=1)


def _inproj_kernel(x_ref, g_ref, w_ref, fb_ref, pq_ref, pk_ref, cq_ref, ck_ref,
                   y_ref, ya4_ref, ya16_ref, qb_ref, kb_ref, vb_ref, qd_ref, kd_ref, vd_ref,
                   carry_ref, fold_ref, *, tm):
    t = pl.program_id(1)

    @pl.when(t == 0)
    def _():
        carry_ref[...] = jnp.zeros_like(carry_ref)

    h = _rms(x_ref[...], g_ref[...]).astype(BF16)

    def mm(c0, width=GROUP_W):
        return _dot(h, w_ref[:, c0:c0 + width])

    f = mm(FD_OFF, LANES) + fb_ref[...]
    logf = jnp.minimum(f, 0.0) - jnp.log(1.0 + jnp.exp(-jnp.abs(f)))
    srow = lax.broadcasted_iota(jnp.int32, (tm, LANES), 0)
    c = logf
    k = 1
    while k < tm:
        c = c + jnp.where(srow >= k, pltpu.roll(c, k, axis=0), 0.0)
        k *= 2
    c = c + carry_ref[0:1, :]
    carry_ref[0:1, :] = c[tm - 1:tm, :]

    c2 = c * LOG2E
    c_hi = c2.astype(BF16)
    r1 = c2 - c_hi.astype(F32)
    c_mid = r1.astype(BF16)
    c_lo = (r1 - c_mid.astype(F32)).astype(BF16)
    feats = jnp.concatenate([c_hi, c_mid, c_lo], axis=1)

    for grp in range(NY // GROUP_W):
        c0 = grp * GROUP_W
        a = mm(c0)
        if c0 in (QA_B * LANES, QC_B * LANES):
            a = a * SCALE
        y_ref[:, c0:c0 + GROUP_W] = a.astype(BF16)
        if c0 < A_COLS:
            fold_ref[2 * grp] = a[:, :LANES]
            fold_ref[2 * grp + 1] = a[:, LANES:]

    for dil, ref in ((4, ya4_ref), (16, ya16_ref)):
        for r in range(dil):
            for slab in range(A_COLS // LANES):
                piece = fold_ref[slab, pl.ds(r, tm // dil, stride=dil), :]
                ref[r, :, slab * LANES:(slab + 1) * LANES] = piece.astype(BF16)

    low = lax.broadcasted_iota(jnp.int32, (tm, LANES), 1) < HEAD_DIM
    lane = lax.broadcasted_iota(jnp.int32, (tm, WIDE), 1) % LANES
    row = lax.broadcasted_iota(jnp.int32, (tm, WIDE), 0)

    qb_ref[...] = _widen(mm(QB_OFF) * (SCALE * LOG2E), low).astype(BF16)
    vb_ref[...] = jnp.where(lane == AUG0, 1.0, _widen(mm(VB_OFF), low)).astype(BF16)
    vd_ref[...] = jnp.where(lane == AUG0, 1.0, _widen(mm(VD_OFF), low)).astype(BF16)
    blk = (t * tm + row) // MOBA_BLOCK
    kb_ref[...] = jnp.where(lane == AUG0 + blk, 1.0, _widen(mm(KB_OFF), low)).astype(BF16)

    qd =_widen(mm(QD_OFF) * (SCALE * LOG2E), low) + _dot(feats, pq_ref[...]) + cq_ref[...]
    kd = _widen(mm(KD_OFF), low) + _dot(feats, pk_ref[...]) + ck_ref[...]
    qd_ref[...] = qd.astype(BF16)
    kd_ref[...] = kd.astype(BF16)


def _inproj(x, g, w, fb, consts):
    B, S, D = x.shape
    tm = PROJ_TM
    pq, pk, cq, ck, _ = consts
    const = lambda b, t: (0, 0)
    tile = lambda b, t: (b, t, 0)
    fold_tile = lambda b, t: (b, 0, t, 0)
    wide_shape = jax.ShapeDtypeStruct((B, S, WIDE), BF16)
    out_shape = (jax.ShapeDtypeStruct((B, S, NY), BF16),
                 jax.ShapeDtypeStruct((B, 4, S // 4, A_COLS), BF16),
                 jax.ShapeDtypeStruct((B, 16, S // 16, A_COLS), BF16),
                 ) + (wide_shape,) * 6
    wide_spec = pl.BlockSpec((None, tm, WIDE), tile)
    return pl.pallas_call(
        functools.partial(_inproj_kernel, tm=tm),
        grid=(B, S // tm),
        in_specs=[pl.BlockSpec((None, tm, D), tile),
                  _resident((1, D), const),
                  _resident((D, N_PROJ), const),
                  _resident((1, LANES), const),
                  _resident((3 * LANES, WIDE), const),
                  _resident((3 * LANES, WIDE), const),
                  _resident((1, WIDE), const),
                  _resident((1, WIDE), const)],
        out_specs=(pl.BlockSpec((None, tm, NY), tile),
                   pl.BlockSpec((None, 4, tm // 4, A_COLS), fold_tile),
                   pl.BlockSpec((None, 16, tm // 16, A_COLS), fold_tile),
                   ) + (wide_spec,) * 6,
        out_shape=out_shape,
        scratch_shapes=[pltpu.VMEM((8, LANES), F32),
                        pltpu.VMEM((A_COLS // LANES, tm, LANES), F32)],
        compiler_params=_params(("arbitrary", "arbitrary")),
        name="inproj",
    )(x, g, w, fb, pq, pk, cq, ck)


def _banded_kernel(*refs, dil, n_tiles, with_sink, with_lse):
    refs = list(refs)
    q_ref, k_ref, v_ref, bias_ref = refs[:4]
    rest = refs[4:]
    sink_ref = rest.pop(0) if with_sink else None
    o_ref = rest.pop(0)
    lse_ref = rest.pop(0) if with_lse else None
    pair = pl.program_id(1)
    low = lax.broadcasted_iota(jnp.int32, (BAND, LANES), 1) < HEAD_DIM
    keep = (low.astype(F32).astype(BF16), (1.0 - low.astype(F32)).astype(BF16))

    def body(it, carry):
        r = it // n_tiles
        n = it % n_tiles
        r0 = pl.multiple_of(n * BAND, BAND)
        k0 = pl.multiple_of(jnp.maximum(r0 - BAND, 0), BAND)
        variant = jnp.where(n == 0, 1, 0)
        qt = q_ref[r, pl.ds(r0, BAND), :]
        kt = k_ref[r, pl.ds(k0, 2 * BAND), :]
        vt = v_ref[r, pl.ds(k0, 2 * BAND), :]
        outs, lses = [], []
        for hh in range(2):
            s = _dot_nt(qt * keep[hh], kt) + bias_ref[hh, variant]
            m = jnp.max(s, axis=1, keepdims=True)
            if with_sink:
                sink = sink_ref[2 * pair + hh]
                m = jnp.maximum(m, sink)
            p = jnp.exp(s - m)
            l = jnp.sum(p, axis=1, keepdims=True)
            if with_sink:
                l = l + jnp.exp(sink - m)
            outs.append(_dot(p.astype(BF16), vt) / l)
            lses.append(jnp.broadcast_to(m + jnp.log(l), (BAND, LANES)))
        o_ref[r, pl.ds(r0, BAND), :] = jnp.where(low, outs[0], outs[1])
        if with_lse:
            lse_ref[r, pl.ds(r0, BAND), :] = jnp.where(low, lses[0], lses[1])
        return carry

    lax.fori_loop(0, dil * n_tiles, body, 0, unroll=BANDED_UNROLL)


def _banded(src, bias, sinks, *, q_blk, k_blk, v_blk, with_lse, name):
    B, dil, L, _ = src.shape
    n_tiles = L // BAND
    assert n_tiles & (n_tiles - 1) == 0 and (dil * n_tiles) % BANDED_UNROLL == 0
    col = lambda blk: (lambda b, p: (b, 0, 0, blk + p))
    in_specs = [pl.BlockSpec((None, dil, L, LANES), col(q_blk)),
                pl.BlockSpec((None, dil, L, LANES), col(k_blk)),
                pl.BlockSpec((None, dil, L, LANES), col(v_blk)),
                pl.BlockSpec((2, 2, BAND, 2 * BAND), lambda b, p: (p, 0, 0, 0))]
    args = [src, src, src, bias]
    if sinks is not None:
        in_specs.append(pl.BlockSpec(memory_space=pltpu.SMEM))
        args.append(sinks)
    o_spec = pl.BlockSpec((None, dil, L, LANES), lambda b, p: (b, 0, 0, p))
    o_shape = jax.ShapeDtypeStruct((B, dil, L, GROUP_W), F32)
    n_out = 2 if with_lse else 1
    return pl.pallas_call(
        functools.partial(_banded_kernel, dil=dil, n_tiles=n_tiles, with_sink=sinks is not None,
                          with_lse=with_lse),
        grid=(B, 2),
        in_specs=in_specs,
        out_specs=(o_spec,) * n_out,
        out_shape=(o_shape,) * n_out,
        compiler_params=_params(("arbitrary",) * 2),
        name=name,
    )(*args)


def _online_update(scores, vts, carry):
    n_heads = len(scores)
    ms = []
    for hh in range(n_heads):
        m_new = carry[2 * hh]
        for s in scores[hh]:
            m_new = jnp.maximum(m_new, jnp.max(s, axis=1, keepdims=True))
        ms.append(m_new)
    new = []
    for hh in range(n_heads):
        m, acc = carry[2 * hh:2 * hh + 2]
        ps = [jnp.exp2(s - ms[hh]).astype(BF16) for s in scores[hh]]
        p = ps[0] if len(ps) == 1 else jnp.concatenate(ps, axis=1)
        acc = jnp.exp2(m - ms[hh]) * acc + _dot(p, vts[hh])
        new.extend((ms[hh], acc))
    return tuple(new)


def _flash_init(tq, n_heads=2):
    return (jnp.full((tq, 1), NEG, F32), jnp.zeros((tq, LANES), F32)) * n_heads


def _finish_pair(o_ref, carry, tq):
    low = lax.broadcasted_iota(jnp.int32, (tq, LANES), 1) < HEAD_DIM
    outs = [acc / acc[:, AUG0:AUG0 + 1] for acc in (carry[1], carry[3])]
    o_ref[...] = jnp.where(low, outs[0], pltpu.roll(outs[1], HEAD_DIM, axis=1))


def _moba_kernel(q_ref, k_ref, v_ref, bias_ref, shift_ref, o_ref, kmean_ref, *, n_blocks):
    i = pl.program_id(2)
    t, blk = FLASH_T, MOBA_BLOCK

    @pl.when(i == 0)
    def _():
        for hh in range(2):
            kf = k_ref[:, hh * LANES:(hh + 1) * LANES].astype(F32)
            kmean_ref[hh] = jnp.mean(kf.reshape(n_blocks, blk, LANES), axis=1)

    blk_id = lax.broadcasted_iota(jnp.int32, (n_blocks, t), 0)
    blk_f = blk_id.astype(F32)
    own = 2 * i + lax.broadcasted_iota(jnp.int32, (n_blocks, t), 1) // blk
    q_aug = []
    for hh in range(2):
        q = q_ref[:, hh * LANES:(hh + 1) * LANES]
        km = kmean_ref[hh]
        k1 = km.astype(BF16)
        r1 = km - k1.astype(F32)
        k2 = r1.astype(BF16)
        k3 = (r1 - k2.astype(F32)).astype(BF16)
        gate = _dot_nt(k1, q) + _dot_nt(k2, q) + _dot_nt(k3, q)

        avail = blk_id < own
        sel = jnp.zeros((n_blocks, t), jnp.bool_)
        for _ in range(MOBA_TOPK):
            cur = jnp.where(avail, gate, -jnp.inf)
            top = jnp.max(cur, axis=0, keepdims=True)
            is_top = avail & (cur == top)
            first = jnp.min(jnp.where(is_top, blk_f, float(n_blocks)), axis=0, keepdims=True)
            pick = blk_f == first
            sel = sel | pick
            avail = avail & jnp.logical_not(pick)
        penalty = jnp.where(sel | (blk_id == own), 0.0, NEG).astype(BF16)
        placed = lax.dot_general(penalty, shift_ref[0:n_blocks, :], (((0,), (0,)), ((), ())),
                                 preferred_element_type=F32)
        q_aug.append((q.astype(F32) + placed).astype(BF16))

    def body(n, carry):
        r0 = pl.multiple_of(n * t, t)
        d = 2 * (i - n) + 1
        scores, vts = [], []
        for hh in range(2):
            kt = k_ref[pl.ds(r0, t), hh * LANES:(hh + 1) * LANES]
            s = _dot_nt(q_aug[hh], kt)
            top = jnp.concatenate([s[:blk, :blk] + bias_ref[hh, d], s[:blk, blk:] + bias_ref[hh, d - 1]], axis=1)
            bot = jnp.concatenate([s[blk:, :blk] + bias_ref[hh, d + 1], s[blk:, blk:] + bias_ref[hh, d]], axis=1)
            scores.append([jnp.concatenate([top, bot], axis=0)])
            vts.append(v_ref[pl.ds(r0, t), hh * LANES:(hh + 1) * LANES])
        return _online_update(scores, vts, carry)

    carry = lax.fori_loop(0, i + 1, body, _flash_init(t))
    _finish_pair(o_ref, carry, t)


def _moba(qb, kb, vb, bias, shift):
    B, S, _ = qb.shape
    nb = S // MOBA_BLOCK
    t = FLASH_T
    assert nb <= AUG0 and t == 2 * MOBA_BLOCK and S % t == 0
    return pl.pallas_call(
        functools.partial(_moba_kernel, n_blocks=nb),
        grid=(B, 2, S // t),
        in_specs=[pl.BlockSpec((None, t, 2 * LANES), lambda b, p, i: (b, i, p)),
                  pl.BlockSpec((None, S, 2 * LANES), lambda b, p, i: (b, 0, p)),
                  pl.BlockSpec((None, S, 2 * LANES), lambda b, p, i: (b, 0, p)),
                  pl.BlockSpec((2, nb + 1, MOBA_BLOCK, MOBA_BLOCK), lambda b, p, i: (p, 0, 0, 0)),
                  pl.BlockSpec((LANES, LANES), lambda b, p, i: (0, 0))],
        out_specs=pl.BlockSpec((None, t, LANES), lambda b, p, i: (b, i, p)),
        out_shape=jax.ShapeDtypeStruct((B, S, GROUP_W), F32),
        scratch_shapes=[pltpu.VMEM((2, nb, LANES), F32)],
        compiler_params=_params(("arbitrary",) * 3),
        name="moba",
    )(qb, kb, vb, bias, shift)


def _fox_kernel(q_ref, k_ref, v_ref, o_ref, *, tq, tk):
    i = pl.program_id(2)
    qs = [q_ref[:, hh * LANES:(hh + 1) * LANES] for hh in range(2)]

    def step(r0, carry, mask):
        scores, vts = [], []
        for hh in range(2):
            kt = k_ref[pl.ds(r0, tk), hh * LANES:(hh + 1) * LANES]
            s = _dot_nt(qs[hh], kt)
            if mask is not None:
                s = jnp.where(mask, s, NEG)
            scores.append([s])
            vts.append(v_ref[pl.ds(r0, tk), hh * LANES:(hh + 1) * LANES])
        return _online_update(scores, vts, carry)

    n_full = (i * tq) // tk
    carry = lax.fori_loop(0, n_full, lambda n, c: step(pl.multiple_of(n * tk, tk), c, None), _flash_init(tq))
    row = lax.broadcasted_iota(jnp.int32, (tq, tk), 0)
    col = lax.broadcasted_iota(jnp.int32, (tq, tk), 1)
    causal = col <= row + (i * tq - n_full * tk)
    carry = step(pl.multiple_of(n_full * tk, tk), carry, causal)
    _finish_pair(o_ref, carry, tq)


def _fox(qd, kd, vd):
    B, S, _ = qd.shape
    tq, tk = FOX_TQ, FOX_TK
    assert S % tk == 0 and tk % tq == 0
    return pl.pallas_call(
        functools.partial(_fox_kernel, tq=tq, tk=tk),
        grid=(B, 2, S // tq),
        in_specs=[pl.BlockSpec((None, tq, 2 * LANES), lambda b, p, i: (b, i, p)),
                  pl.BlockSpec((None, S, 2 * LANES), lambda b, p, i: (b, 0, p)),
                  pl.BlockSpec((None, S, 2 * LANES), lambda b, p, i: (b, 0, p))],
        out_specs=pl.BlockSpec((None, tq, LANES), lambda b, p, i: (b, i, p)),
        out_shape=jax.ShapeDtypeStruct((B, S, GROUP_W), F32),
        compiler_params=_params(("arbitrary",) * 3),
        name="fox",
    )(qd, kd, vd)


def _outffn_kernel(x_ref, oa1_ref, oa4_ref, oa16_ref, la1_ref, la4_ref, la16_ref, ob_ref, oc_ref, od_ref,
                   mg_ref, wo_ref, g2_ref, wg_ref, wu_ref, wd_ref, fg_ref, out_ref, unfold_ref, *, final, tm):
    def unfold(ref, dil, slot):
        for r in range(dil):
            for s in range(GROUP_W // LANES):
                unfold_ref[slot, s, pl.ds(r, tm // dil, stride=dil), :] = ref[r, :, s * LANES:(s + 1) * LANES]
        return jnp.concatenate([unfold_ref[slot, s] for s in range(GROUP_W // LANES)], axis=1)

    l1, l4, l16 = la1_ref[...], unfold(la4_ref, 4, 0), unfold(la16_ref, 16, 1)
    o1, o4, o16 = oa1_ref[...], unfold(oa4_ref, 4, 2), unfold(oa16_ref, 16, 3)
    m = jnp.maximum(jnp.maximum(l1, l4), l16)
    e1, e4, e16 = jnp.exp(l1 - m), jnp.exp(l4 - m), jnp.exp(l16 - m)
    mix_a = (e1 * o1 + e4 * o4 + e16 * o16) / (e1 + e4 + e16)

    x1 = x_ref[...]
    for g, part in enumerate((mix_a, ob_ref[...], oc_ref[...], od_ref[...])):
        c0 = g * GROUP_W
        normed = _rms(part, mg_ref[:, c0:c0 + GROUP_W]).astype(BF16)
        x1 = x1 + _dot(normed, wo_ref[c0:c0 + GROUP_W, :])

    h = _rms(x1, g2_ref[...]).astype(BF16)
    ffn = jnp.zeros_like(x1)
    for c in range(D_FF // FF_CHUNK):
        c0 = c * FF_CHUNK
        gate = _dot(h, wg_ref[:, c0:c0 + FF_CHUNK])
        up = _dot(h, wu_ref[:, c0:c0 + FF_CHUNK])
        act = (gate * (1.0 / (1.0 + jnp.exp(-gate))) * up).astype(BF16)
        ffn = ffn + _dot(act, wd_ref[c0:c0 + FF_CHUNK, :])
    x2 = x1 + ffn
    if final:
        x2 = _rms(x2, fg_ref[...])
    out_ref[...] = x2


def _outffn(x, a_parts, ob, oc, od, mg, wo, g2, wg, wu, wd, fg, *, final):
    B, S, D = x.shape
    tm = PROJ_TM
    tile = lambda b, t: (b, t, 0)
    const = lambda b, t: (0, 0)
    flat_spec = pl.BlockSpec((None, tm, GROUP_W), tile)

    def fold_spec(dil):
        if dil == 1:
            return pl.BlockSpec((None, None, tm, GROUP_W), lambda b, t: (b, 0, t, 0))
        return pl.BlockSpec((None, dil, tm // dil, GROUP_W), lambda b, t: (b, 0, t, 0))

    dils = [dil for _, dil in DILATED_GROUPS]
    a_specs = [fold_spec(d) for d in dils] * 2
    a_args = [p[0] for p in a_parts] + [p[1] for p in a_parts]
    return pl.pallas_call(
        functools.partial(_outffn_kernel, final=final, tm=tm),
        grid=(B, S // tm),
        in_specs=[pl.BlockSpec((None, tm, D), tile)] + a_specs + [flat_spec] * 3 + [
            _resident((1, D), const),
            _resident((D, D), const),
            _resident((1, D), const),
            _resident((D, D_FF), const),
            _resident((D, D_FF), const),
            _resident((D_FF, D), const),
            _resident((1, D), const)],
        out_specs=pl.BlockSpec((None, tm, D), tile),
        out_shape=jax.ShapeDtypeStruct((B, S, D), F32),
        scratch_shapes=[pltpu.VMEM((4, GROUP_W // LANES, tm, LANES), F32)],
        compiler_params=_params(("arbitrary", "arbitrary")),
        name="outffn",
    )(x, *a_args, ob, oc, od, mg, wo, g2, wg, wu, wd, fg)


def kernel(x, norm1_g, w_in, f_bias, sinks, mix_norm_g, w_out, norm2_g, w_gate, w_up, w_down, rel_bias, final_g):
    B, S, D = x.shape
    depth = w_in.shape[0]
    max_dil = max(dil for _, dil in DILATED_GROUPS)
    assert D == D_MODEL and S % (2 * BAND * max_dil) == 0 and w_gate.shape[-1] == D_FF
    assert [dil for _, dil in DILATED_GROUPS] == [1, 4, 16] and S % PROJ_TM == 0

    w_proj = _rearrange_w_in(w_in)
    wo_b, wg_b, wu_b, wd_b = (w.astype(BF16) for w in (w_out, w_gate, w_up, w_down))
    fb = jnp.pad(f_bias, ((0, 0), (0, LANES - N_HEADS)))[:, None, :]
    consts = _placement_constants()
    table_a = rel_bias[:, :N_HEADS]
    table_b = rel_bias[:, N_HEADS:2 * N_HEADS]
    table_c = rel_bias[:, 2 * N_HEADS:]
    bias_a = [_band_bias(table_a, dil, window // dil, f"dilated{dil}_bias") for window, dil in DILATED_GROUPS]
    bias_c = _band_bias(table_c, 1, SWA_WINDOW - 1, "swa_bias")
    bias_b = _moba_bias(table_b, S // MOBA_BLOCK)

    for l in range(depth):
        y, ya4, ya16, qb, kb, vb, qd, kd, vd = _inproj(x, norm1_g[l][None], w_proj[l], fb[l], consts)
        y4 = y.reshape(B, 1, S, NY)
        a_parts = [_banded(src, bias_a[g], None, q_blk=qb_, k_blk=kb_, v_blk=vb_, with_lse=True,
                           name=f"dilated{dil}")
                   for g, ((_, dil), src, (qb_, kb_, vb_)) in enumerate(zip(
                       DILATED_GROUPS, (y4, ya4, ya16), ((QA_B, KA_B, VA_B), (0, 2, 4), (0, 2, 4))))]
        (o_c,) = _banded(y4, bias_c, sinks[l], q_blk=QC_B, k_blk=KC_B, v_blk=VC_B, with_lse=False, name="swa")
        o_b = _moba(qb, kb, vb, bias_b, consts[4])
        o_d = _fox(qd, kd, vd)
        x = _outffn(x, a_parts, o_b, o_c.reshape(B, S, GROUP_W), o_d, mix_norm_g[l][None], wo_b[l],
                    norm2_g[l][None], wg_b[l], wu_b[l], wd_b[l], final_g[None], final=(l == depth - 1))
    return x
```

```python
import functools
import math

import numpy as np
import jax
import jax.numpy as jnp
from jax import lax
from jax.experimental import pallas as pl
from jax.experimental.pallas import tpu as pltpu

F32 = jnp.float32
BF16 = jnp.bfloat16

LANES = 128
VMEM_LIMIT_BYTES = 56 * 1024 * 1024

D_MODEL = 1024
HEAD_DIM = 64
N_HEADS = 4
GROUP_W = N_HEADS * HEAD_DIM
DILATED_GROUPS = ((128, 1), (512, 4), (2048, 16))
BAND = 128
MOBA_BLOCK = 256
MOBA_TOPK = 3
SWA_WINDOW = 128
N_BUCKETS = 32
T5_MAX_DISTANCE = 2048
D_FF = 2816
FF_CHUNK = 256
EPS = 1e-6
NEG = -1e30
SCALE = HEAD_DIM ** -0.5

LOG2E = math.log2(math.e)
QA_B, KA_B, VA_B, QC_B, KC_B, VC_B = 0, 2, 4, 6, 8, 10
NY = 12 * LANES
A_COLS = 3 * GROUP_W
WIDE = N_HEADS * LANES
QB_OFF, KB_OFF, VB_OFF, QD_OFF, KD_OFF, VD_OFF = (NY + g * GROUP_W for g in range(6))
FD_OFF = NY + 6 * GROUP_W
N_PROJ = FD_OFF + LANES
AUG0 = HEAD_DIM

PROJ_TM = 512
FLASH_T = 1024
FOX_TQ, FOX_TK = 1024, 1024
BANDED_UNROLL = 16


def _params(sem):
    return pltpu.CompilerParams(dimension_semantics=sem, vmem_limit_bytes=VMEM_LIMIT_BYTES)


def _resident(shape, index_map):
    return pl.BlockSpec(shape, index_map, pipeline_mode=pl.Buffered(1))


def _dot(a, b):
    return jnp.dot(a, b, preferred_element_type=F32)


def _dot_nt(a, b):
    return lax.dot_general(a, b, (((1,), (1,)), ((), ())), preferred_element_type=F32)


def _rms(x, g):
    return x * lax.rsqrt(jnp.mean(x * x, axis=-1, keepdims=True) + EPS) * g


def _rearrange_w_in(w):
    w = w.astype(BF16)
    sl = lambda a, b: w[..., a:b]
    kc0, kc1 = sl(1792, 1856), sl(1856, 1920)
    vc0, vc1 = sl(1920, 1984), sl(1984, 2048)
    fd = jnp.pad(sl(2816, 2820), [(0, 0)] * (w.ndim - 1) + [(0, LANES - N_HEADS)])
    out = jnp.concatenate([
        sl(0, 768),
        sl(1536, 1792),
        kc0, kc0, kc1, kc1,
        vc0, vc0, vc1, vc1,
        sl(768, 1536),
        sl(2048, 2816),
        fd], axis=-1)
    assert out.shape[-1] == N_PROJ
    return out


def _t5_bucket_np(dist):
    n = np.maximum(dist, 0)
    max_exact = N_BUCKETS // 2
    nf = np.maximum(n, 1).astype(np.float32)
    large = max_exact + (np.log(nf / np.float32(max_exact)) / np.float32(math.log(T5_MAX_DISTANCE / max_exact))
                         * np.float32(N_BUCKETS - max_exact)).astype(np.int32)
    large = np.minimum(large, N_BUCKETS - 1)
    return np.where(n < max_exact, n, large).astype(np.int32)


def _bucket_thresholds():
    n = np.arange(1 << 16)
    bucket = _t5_bucket_np(n)
    assert (np.diff(bucket) >= 0).all() and bucket[-1] == N_BUCKETS - 1
    thr = np.searchsorted(bucket, np.arange(N_BUCKETS), side="left")
    assert (np.searchsorted(thr, n, side="right") - 1 == bucket).all()
    return tuple(int(t) for t in thr)


def _placement_constants():
    pq = np.zeros((3 * LANES, WIDE), np.float32)
    pk = np.zeros((3 * LANES, WIDE), np.float32)
    cq = np.zeros((1, WIDE), np.float32)
    ck = np.zeros((1, WIDE), np.float32)
    for h in range(N_HEADS):
        for part in range(3):
            pq[part * LANES + h, h * LANES + AUG0 + part] = 1.0
            pk[part * LANES + h, h * LANES + AUG0 + 3 + part] = -1.0
            ck[0, h * LANES + AUG0 + part] = 1.0
            cq[0, h * LANES + AUG0 + 3 + part] = 1.0
    shift = np.zeros((LANES, LANES), np.float32)
    for n in range(AUG0):
        shift[n, AUG0 + n] = 1.0
    return (jnp.asarray(pq, BF16), jnp.asarray(pk, BF16), jnp.asarray(cq), jnp.asarray(ck),
            jnp.asarray(shift, BF16))


def _bias_tiles_kernel(base_ref, table_ref, o_ref, *, rows, cols, step, max_dist, scale, thresholds, chunk):
    t = pl.program_id(0)
    base = base_ref[t]

    def body(c, carry):
        r0 = pl.multiple_of(c * chunk, chunk)
        i = lax.broadcasted_iota(jnp.int32, (chunk, cols), 0) + r0
        j = lax.broadcasted_iota(jnp.int32, (chunk, cols), 1)
        dist = base + i - j
        ok = (dist >= 0) & (dist <= max_dist)
        scaled = dist * step
        vals = [jnp.full((chunk, cols), table_ref[0, h] * scale, F32) for h in range(N_HEADS)]
        for b in range(1, N_BUCKETS):
            reached = scaled >= thresholds[b]
            vals = [jnp.where(reached, table_ref[b, h] * scale, v) for h, v in enumerate(vals)]
        for h in range(N_HEADS):
            o_ref[h, pl.ds(r0, chunk), :] = jnp.where(ok, vals[h], NEG)
        return carry

    lax.fori_loop(0, rows // chunk, body, 0)


def _bias_tiles(table, bases, *, rows, cols, step, max_dist, name, scale=1.0):
    n_tiles = len(bases)
    return pl.pallas_call(
        functools.partial(_bias_tiles_kernel, rows=rows, cols=cols, step=step, max_dist=max_dist, scale=scale,
                          thresholds=_bucket_thresholds(), chunk=32),
        grid=(n_tiles,),
        in_specs=[pl.BlockSpec(memory_space=pltpu.SMEM), pl.BlockSpec(memory_space=pltpu.SMEM)],
        out_specs=pl.BlockSpec((N_HEADS, None, rows, cols), lambda t: (0, t, 0, 0)),
        out_shape=jax.ShapeDtypeStruct((N_HEADS, n_tiles, rows, cols), F32),
        compiler_params=_params(("arbitrary",)),
        name=name,
    )(jnp.asarray(bases, jnp.int32), table)


def _band_bias(table, step, max_dist, name):
    return _bias_tiles(table, (BAND, 0), rows=BAND, cols=2 * BAND, step=step, max_dist=max_dist, name=name)


def _moba_bias(table, n_blocks):
    bases = tuple(MOBA_BLOCK * d for d in range(1 - FLASH_T // MOBA_BLOCK, n_blocks))
    return _bias_tiles(table, bases, rows=MOBA_BLOCK, cols=MOBA_BLOCK, step=1, max_dist=1 << 30,
                       name="moba_bias", scale=LOG2E)


def _widen(a, low):
    outs = []
    for p in range(GROUP_W // LANES):
        pair = a[:, p * LANES:(p + 1) * LANES]
        outs.append(jnp.where(low, pair, 0.0))
        outs.append(jnp.where(low, pltpu.roll(pair, HEAD_DIM, axis=1), 0.0))
    return jnp.concatenate(outs, axis=1)


def _inproj_kernel(x_ref, g_ref, w_ref, fb_ref, pq_ref, pk_ref, cq_ref, ck_ref,
                   y_ref, ya4_ref, ya16_ref, qb_ref, kb_ref, vb_ref, qd_ref, kd_ref, vd_ref,
                   carry_ref, fold_ref, *, tm):
    t = pl.program_id(1)

    @pl.when(t == 0)
    def _():
        carry_ref[...] = jnp.zeros_like(carry_ref)

    h = _rms(x_ref[...], g_ref[...]).astype(BF16)

    def mm(c0, width=GROUP_W):
        return _dot(h, w_ref[:, c0:c0 + width])

    f = mm(FD_OFF, LANES) + fb_ref[...]
    logf = jnp.minimum(f, 0.0) - jnp.log(1.0 + jnp.exp(-jnp.abs(f)))
    srow = lax.broadcasted_iota(jnp.int32, (tm, LANES), 0)
    c = logf
    k = 1
    while k < tm:
        c = c + jnp.where(srow >= k, pltpu.roll(c, k, axis=0), 0.0)
        k *= 2
    c = c + carry_ref[0:1, :]
    carry_ref[0:1, :] = c[tm - 1:tm, :]

    c2 = c * LOG2E
    c_hi = c2.astype(BF16)
    r1 = c2 - c_hi.astype(F32)
    c_mid = r1.astype(BF16)
    c_lo = (r1 - c_mid.astype(F32)).astype(BF16)
    feats = jnp.concatenate([c_hi, c_mid, c_lo], axis=1)

    for grp in range(NY // GROUP_W):
        c0 = grp * GROUP_W
        a = mm(c0)
        if c0 in (QA_B * LANES, QC_B * LANES):
            a = a * SCALE
        y_ref[:, c0:c0 + GROUP_W] = a.astype(BF16)
        if c0 < A_COLS:
            fold_ref[2 * grp] = a[:, :LANES]
            fold_ref[2 * grp + 1] = a[:, LANES:]

    for dil, ref in ((4, ya4_ref), (16, ya16_ref)):
        for r in range(dil):
            for slab in range(A_COLS // LANES):
                piece = fold_ref[slab, pl.ds(r, tm // dil, stride=dil), :]
                ref[r, :, slab * LANES:(slab + 1) * LANES] = piece.astype(BF16)

    low = lax.broadcasted_iota(jnp.int32, (tm, LANES), 1) < HEAD_DIM
    lane = lax.broadcasted_iota(jnp.int32, (tm, WIDE), 1) % LANES
    row = lax.broadcasted_iota(jnp.int32, (tm, WIDE), 0)

    qb_ref[...] = _widen(mm(QB_OFF) * (SCALE * LOG2E), low).astype(BF16)
    vb_ref[...] = jnp.where(lane == AUG0, 1.0, _widen(mm(VB_OFF), low)).astype(BF16)
    vd_ref[...] = jnp.where(lane == AUG0, 1.0, _widen(mm(VD_OFF), low)).astype(BF16)
    blk = (t * tm + row) // MOBA_BLOCK
    kb_ref[...] = jnp.where(lane == AUG0 + blk, 1.0, _widen(mm(KB_OFF), low)).astype(BF16)

    qd = _widen(mm(QD_OFF) * (SCALE * LOG2E), low) + _dot(feats, pq_ref[...]) + cq_ref[...]
    kd = _widen(mm(KD_OFF), low) + _dot(feats, pk_ref[...]) + ck_ref[...]
    qd_ref[...] = qd.astype(BF16)
    kd_ref[...] = kd.astype(BF16)


def _inproj(x, g, w, fb, consts):
    B, S, D = x.shape
    tm = PROJ_TM
    pq, pk, cq, ck, _ = consts
    const = lambda b, t: (0, 0)
    tile = lambda b, t: (b, t, 0)
    fold_tile = lambda b, t: (b, 0, t, 0)
    wide_shape = jax.ShapeDtypeStruct((B, S, WIDE), BF16)
    out_shape = (jax.ShapeDtypeStruct((B, S, NY), BF16),
                 jax.ShapeDtypeStruct((B, 4, S // 4, A_COLS), BF16),
                 jax.ShapeDtypeStruct((B, 16, S // 16, A_COLS), BF16),
                 ) + (wide_shape,) * 6
    wide_spec = pl.BlockSpec((None, tm, WIDE), tile)
    return pl.pallas_call(
        functools.partial(_inproj_kernel, tm=tm),
        grid=(B, S // tm),
        in_specs=[pl.BlockSpec((None, tm, D), tile),
                  _resident((1, D), const),
                  _resident((D, N_PROJ), const),
                  _resident((1, LANES), const),
                  _resident((3 * LANES, WIDE), const),
                  _resident((3 * LANES, WIDE), const),
                  _resident((1, WIDE), const),
                  _resident((1, WIDE), const)],
        out_specs=(pl.BlockSpec((None, tm, NY), tile),
                   pl.BlockSpec((None, 4, tm // 4, A_COLS), fold_tile),
                   pl.BlockSpec((None, 16, tm // 16, A_COLS), fold_tile),
                   ) + (wide_spec,) * 6,
        out_shape=out_shape,
        scratch_shapes=[pltpu.VMEM((8, LANES), F32),
                        pltpu.VMEM((A_COLS // LANES, tm, LANES), F32)],
        compiler_params=_params(("arbitrary", "arbitrary")),
        name="inproj",
    )(x, g, w, fb, pq, pk, cq, ck)


def _banded_kernel(*refs, dil, n_tiles, with_sink, with_lse):
    refs = list(refs)
    q_ref, k_ref, v_ref, bias_ref = refs[:4]
    rest = refs[4:]
    sink_ref = rest.pop(0) if with_sink else None
    o_ref = rest.pop(0)
    lse_ref = rest.pop(0) if with_lse else None
    pair = pl.program_id(1)
    low = lax.broadcasted_iota(jnp.int32, (BAND, LANES), 1) < HEAD_DIM
    keep = (low.astype(F32).astype(BF16), (1.0 - low.astype(F32)).astype(BF16))

    def body(it, carry):
        r = it // n_tiles
        n = it % n_tiles
        r0 = pl.multiple_of(n * BAND, BAND)
        k0 = pl.multiple_of(jnp.maximum(r0 - BAND, 0), BAND)
        variant = jnp.where(n == 0, 1, 0)
        qt = q_ref[r, pl.ds(r0, BAND), :]
        kt = k_ref[r, pl.ds(k0, 2 * BAND), :]
        vt = v_ref[r, pl.ds(k0, 2 * BAND), :]
        outs, lses = [], []
        for hh in range(2):
            s = _dot_nt(qt * keep[hh], kt) + bias_ref[hh, variant]
            m = jnp.max(s, axis=1, keepdims=True)
            if with_sink:
                sink = sink_ref[2 * pair + hh]
                m = jnp.maximum(m, sink)
            p = jnp.exp(s - m)
            l = jnp.sum(p, axis=1, keepdims=True)
            if with_sink:
                l = l + jnp.exp(sink - m)
            outs.append(_dot(p.astype(BF16), vt) / l)
            lses.append(jnp.broadcast_to(m + jnp.log(l), (BAND, LANES)))
        o_ref[r, pl.ds(r0, BAND), :] = jnp.where(low, outs[0], outs[1])
        if with_lse:
            lse_ref[r, pl.ds(r0, BAND), :] = jnp.where(low, lses[0], lses[1])
        return carry

    lax.fori_loop(0, dil * n_tiles, body, 0, unroll=BANDED_UNROLL)


def _banded(src, bias, sinks, *, q_blk, k_blk, v_blk, with_lse, name):
    B, dil, L, _ = src.shape
    n_tiles = L // BAND
    assert n_tiles & (n_tiles - 1) == 0 and (dil * n_tiles) % BANDED_UNROLL == 0
    col = lambda blk: (lambda b, p: (b, 0, 0, blk + p))
    in_specs = [pl.BlockSpec((None, dil, L, LANES), col(q_blk)),
                pl.BlockSpec((None, dil, L, LANES), col(k_blk)),
                pl.BlockSpec((None, dil, L, LANES), col(v_blk)),
                pl.BlockSpec((2, 2, BAND, 2 * BAND), lambda b, p: (p, 0, 0, 0))]
    args = [src, src, src, bias]
    if sinks is not None:
        in_specs.append(pl.BlockSpec(memory_space=pltpu.SMEM))
        args.append(sinks)
    o_spec = pl.BlockSpec((None, dil, L, LANES), lambda b, p: (b, 0, 0, p))
    o_shape = jax.ShapeDtypeStruct((B, dil, L, GROUP_W), F32)
    n_out = 2 if with_lse else 1
    return pl.pallas_call(
        functools.partial(_banded_kernel, dil=dil, n_tiles=n_tiles, with_sink=sinks is not None,
                          with_lse=with_lse),
        grid=(B, 2),
        in_specs=in_specs,
        out_specs=(o_spec,) * n_out,
        out_shape=(o_shape,) * n_out,
        compiler_params=_params(("arbitrary",) * 2),
        name=name,
    )(*args)


def _online_update(scores, vts, carry):
    n_heads = len(scores)
    ms = []
    for hh in range(n_heads):
        m_new = carry[2 * hh]
        for s in scores[hh]:
            m_new = jnp.maximum(m_new, jnp.max(s, axis=1, keepdims=True))
        ms.append(m_new)
    new = []
    for hh in range(n_heads):
        m, acc = carry[2 * hh:2 * hh + 2]
        ps = [jnp.exp2(s - ms[hh]).astype(BF16) for s in scores[hh]]
        p = ps[0] if len(ps) == 1 else jnp.concatenate(ps, axis=1)
        acc = jnp.exp2(m - ms[hh]) * acc + _dot(p, vts[hh])
        new.extend((ms[hh], acc))
    return tuple(new)


def _flash_init(tq, n_heads=2):
    return (jnp.full((tq, 1), NEG, F32), jnp.zeros((tq, LANES), F32)) * n_heads


def _finish_pair(o_ref, carry, tq):
    low = lax.broadcasted_iota(jnp.int32, (tq, LANES), 1) < HEAD_DIM
    outs = [acc / acc[:, AUG0:AUG0 + 1] for acc in (carry[1], carry[3])]
    o_ref[...] = jnp.where(low, outs[0], pltpu.roll(outs[1], HEAD_DIM, axis=1))


def _moba_kernel(q_ref, k_ref, v_ref, bias_ref, shift_ref, o_ref, kmean_ref, *, n_blocks):
    i = pl.program_id(2)
    t, blk = FLASH_T, MOBA_BLOCK
    per = t // blk

    @pl.when(i == 0)
    def _():
        for hh in range(2):
            kf = k_ref[:, hh * LANES:(hh + 1) * LANES].astype(F32)
            kmean_ref[hh] = jnp.mean(kf.reshape(n_blocks, blk, LANES), axis=1)

    blk_id = lax.broadcasted_iota(jnp.int32, (n_blocks, t), 0)
    blk_f = blk_id.astype(F32)
    own = per * i + lax.broadcasted_iota(jnp.int32, (n_blocks, t), 1) // blk
    q_aug = []
    for hh in range(2):
        q = q_ref[:, hh * LANES:(hh + 1) * LANES]
        km = kmean_ref[hh]
        k1 = km.astype(BF16)
        r1 = km - k1.astype(F32)
        k2 = r1.astype(BF16)
        k3 = (r1 - k2.astype(F32)).astype(BF16)
        gate = _dot_nt(k1, q) + _dot_nt(k2, q) + _dot_nt(k3, q)

        avail = blk_id < own
        sel = jnp.zeros((n_blocks, t), jnp.bool_)
        for _ in range(MOBA_TOPK):
            cur = jnp.where(avail, gate, -jnp.inf)
            top = jnp.max(cur, axis=0, keepdims=True)
            is_top = avail & (cur == top)
            first = jnp.min(jnp.where(is_top, blk_f, float(n_blocks)), axis=0, keepdims=True)
            pick = blk_f == first
            sel = sel | pick
            avail = avail & jnp.logical_not(pick)
        penalty = jnp.where(sel | (blk_id == own), 0.0, NEG).astype(BF16)
        placed = lax.dot_general(penalty, shift_ref[0:n_blocks, :], (((0,), (0,)), ((), ())),
                                 preferred_element_type=F32)
        q_aug.append((q.astype(F32) + placed).astype(BF16))

    def body(n, carry):
        r0 = pl.multiple_of(n * t, t)
        d0 = per * (i - n) + per - 1
        scores, vts = [], []
        for hh in range(2):
            kt = k_ref[pl.ds(r0, t), hh * LANES:(hh + 1) * LANES]
            s = _dot_nt(q_aug[hh], kt)
            rows = [jnp.concatenate([s[a * blk:(a + 1) * blk, b * blk:(b + 1) * blk] + bias_ref[hh, d0 + a - b]
                                     for b in range(per)], axis=1) for a in range(per)]
            scores.append([jnp.concatenate(rows, axis=0)])
            vts.append(v_ref[pl.ds(r0, t), hh * LANES:(hh + 1) * LANES])
        return _online_update(scores, vts, carry)

    carry = lax.fori_loop(0, i + 1, body, _flash_init(t))
    _finish_pair(o_ref, carry, t)


def _moba(qb, kb, vb, bias, shift):
    B, S, _ = qb.shape
    nb = S // MOBA_BLOCK
    t = FLASH_T
    assert nb <= AUG0 and t % MOBA_BLOCK == 0 and S % t == 0
    n_bias = nb + t // MOBA_BLOCK - 1
    return pl.pallas_call(
        functools.partial(_moba_kernel, n_blocks=nb),
        grid=(B, 2, S // t),
        in_specs=[pl.BlockSpec((None, t, 2 * LANES), lambda b, p, i: (b, i, p)),
                  pl.BlockSpec((None, S, 2 * LANES), lambda b, p, i: (b, 0, p)),
                  pl.BlockSpec((None, S, 2 * LANES), lambda b, p, i: (b, 0, p)),
                  pl.BlockSpec((2, n_bias, MOBA_BLOCK, MOBA_BLOCK), lambda b, p, i: (p, 0, 0, 0)),
                  pl.BlockSpec((LANES, LANES), lambda b, p, i: (0, 0))],
        out_specs=pl.BlockSpec((None, t, LANES), lambda b, p, i: (b, i, p)),
        out_shape=jax.ShapeDtypeStruct((B, S, GROUP_W), F32),
        scratch_shapes=[pltpu.VMEM((2, nb, LANES), F32)],
        compiler_params=_params(("arbitrary",) * 3),
        name="moba",
    )(qb, kb, vb, bias, shift)


def _fox_kernel(q_ref, k_ref, v_ref, o_ref, *, tq, tk):
    i = pl.program_id(2)
    qs = [q_ref[:, hh * LANES:(hh + 1) * LANES] for hh in range(2)]

    def step(r0, carry, mask):
        scores, vts = [], []
        for hh in range(2):
            kt = k_ref[pl.ds(r0, tk), hh * LANES:(hh + 1) * LANES]
            s = _dot_nt(qs[hh], kt)
            if mask is not None:
                s = jnp.where(mask, s, NEG)
            scores.append([s])
            vts.append(v_ref[pl.ds(r0, tk), hh * LANES:(hh + 1) * LANES])
        return _online_update(scores, vts, carry)

    n_full = (i * tq) // tk
    carry = lax.fori_loop(0, n_full, lambda n, c: step(pl.multiple_of(n * tk, tk), c, None), _flash_init(tq))
    row = lax.broadcasted_iota(jnp.int32, (tq, tk), 0)
    col = lax.broadcasted_iota(jnp.int32, (tq, tk), 1)
    causal = col <= row + (i * tq - n_full * tk)
    carry = step(pl.multiple_of(n_full * tk, tk), carry, causal)
    _finish_pair(o_ref, carry, tq)


def _fox(qd, kd, vd):
    B, S, _ = qd.shape
    tq, tk = FOX_TQ, FOX_TK
    assert S % tk == 0 and tk % tq == 0
    return pl.pallas_call(
        functools.partial(_fox_kernel, tq=tq, tk=tk),
        grid=(B, 2, S // tq),
        in_specs=[pl.BlockSpec((None, tq, 2 * LANES), lambda b, p, i: (b, i, p)),
                  pl.BlockSpec((None, S, 2 * LANES), lambda b, p, i: (b, 0, p)),
                  pl.BlockSpec((None, S, 2 * LANES), lambda b, p, i: (b, 0, p))],
        out_specs=pl.BlockSpec((None, tq, LANES), lambda b, p, i: (b, i, p)),
        out_shape=jax.ShapeDtypeStruct((B, S, GROUP_W), F32),
        compiler_params=_params(("arbitrary",) * 3),
        name="fox",
    )(qd, kd, vd)


def _outffn_kernel(x_ref, oa1_ref, oa4_ref, oa16_ref, la1_ref, la4_ref, la16_ref, ob_ref, oc_ref, od_ref,
                   mg_ref, wo_ref, g2_ref, wg_ref, wu_ref, wd_ref, fg_ref, out_ref, unfold_ref, *, final, tm):
    def unfold(ref, dil, slot):
        for r in range(dil):
            for s in range(GROUP_W // LANES):
                unfold_ref[slot, s, pl.ds(r, tm // dil, stride=dil), :] = ref[r, :, s * LANES:(s + 1) * LANES]
        return jnp.concatenate([unfold_ref[slot, s] for s in range(GROUP_W // LANES)], axis=1)

    l1, l4, l16 = la1_ref[...], unfold(la4_ref, 4, 0), unfold(la16_ref, 16, 1)
    o1, o4, o16 = oa1_ref[...], unfold(oa4_ref, 4, 2), unfold(oa16_ref, 16, 3)
    m = jnp.maximum(jnp.maximum(l1, l4), l16)
    e1, e4, e16 = jnp.exp(l1 - m), jnp.exp(l4 - m), jnp.exp(l16 - m)
    mix_a = (e1 * o1 + e4 * o4 + e16 * o16) / (e1 + e4 + e16)

    x1 = x_ref[...]
    for g, part in enumerate((mix_a, ob_ref[...], oc_ref[...], od_ref[...])):
        c0 = g * GROUP_W
        normed = _rms(part, mg_ref[:, c0:c0 + GROUP_W]).astype(BF16)
        x1 = x1 + _dot(normed, wo_ref[c0:c0 + GROUP_W, :])

    h = _rms(x1, g2_ref[...]).astype(BF16)
    ffn = jnp.zeros_like(x1)
    for c in range(D_FF // FF_CHUNK):
        c0 = c * FF_CHUNK
        gate = _dot(h, wg_ref[:, c0:c0 + FF_CHUNK])
        up = _dot(h, wu_ref[:, c0:c0 + FF_CHUNK])
        act = (gate * (1.0 / (1.0 + jnp.exp(-gate))) * up).astype(BF16)
        ffn = ffn + _dot(act, wd_ref[c0:c0 + FF_CHUNK, :])
    x2 = x1 + ffn
    if final:
        x2 = _rms(x2, fg_ref[...])
    out_ref[...] = x2


def _outffn(x, a_parts, ob, oc, od, mg, wo, g2, wg, wu, wd, fg, *, final):
    B, S, D = x.shape
    tm = PROJ_TM
    tile = lambda b, t: (b, t, 0)
    const = lambda b, t: (0, 0)
    flat_spec = pl.BlockSpec((None, tm, GROUP_W), tile)

    def fold_spec(dil):
        if dil == 1:
            return pl.BlockSpec((None, None, tm, GROUP_W), lambda b, t: (b, 0, t, 0))
        return pl.BlockSpec((None, dil, tm // dil, GROUP_W), lambda b, t: (b, 0, t, 0))

    dils = [dil for _, dil in DILATED_GROUPS]
    a_specs = [fold_spec(d) for d in dils] * 2
    a_args = [p[0] for p in a_parts] + [p[1] for p in a_parts]
    return pl.pallas_call(
        functools.partial(_outffn_kernel, final=final, tm=tm),
        grid=(B, S // tm),
        in_specs=[pl.BlockSpec((None, tm, D), tile)] + a_specs + [flat_spec] * 3 + [
            _resident((1, D), const),
            _resident((D, D), const),
            _resident((1, D), const),
            _resident((D, D_FF), const),
            _resident((D, D_FF), const),
            _resident((D_FF, D), const),
            _resident((1, D), const)],
        out_specs=pl.BlockSpec((None, tm, D), tile),
        out_shape=jax.ShapeDtypeStruct((B, S, D), F32),
        scratch_shapes=[pltpu.VMEM((4, GROUP_W // LANES, tm, LANES), F32)],
        compiler_params=_params(("arbitrary", "arbitrary")),
        name="outffn",
    )(x, *a_args, ob, oc, od, mg, wo, g2, wg, wu, wd, fg)


def kernel(x, norm1_g, w_in, f_bias, sinks, mix_norm_g, w_out, norm2_g, w_gate, w_up, w_down, rel_bias, final_g):
    B, S, D = x.shape
    depth = w_in.shape[0]
    max_dil = max(dil for _, dil in DILATED_GROUPS)
    assert D == D_MODEL and S % (2 * BAND * max_dil) == 0 and w_gate.shape[-1] == D_FF
    assert [dil for _, dil in DILATED_GROUPS] == [1, 4, 16] and S % PROJ_TM == 0

    w_proj = _rearrange_w_in(w_in)
    wo_b, wg_b, wu_b, wd_b = (w.astype(BF16) for w in (w_out, w_gate, w_up, w_down))
    fb = jnp.pad(f_bias, ((0, 0), (0, LANES - N_HEADS)))[:, None, :]
    consts = _placement_constants()
    table_a = rel_bias[:, :N_HEADS]
    table_b = rel_bias[:, N_HEADS:2 * N_HEADS]
    table_c = rel_bias[:, 2 * N_HEADS:]
    bias_a = [_band_bias(table_a, dil, window // dil, f"dilated{dil}_bias") for window, dil in DILATED_GROUPS]
    bias_c = _band_bias(table_c, 1, SWA_WINDOW - 1, "swa_bias")
    bias_b = _moba_bias(table_b, S // MOBA_BLOCK)

    for l in range(depth):
        y, ya4, ya16, qb, kb, vb, qd, kd, vd = _inproj(x, norm1_g[l][None], w_proj[l], fb[l], consts)
        y4 = y.reshape(B, 1, S, NY)
        a_parts = [_banded(src, bias_a[g], None, q_blk=qb_, k_blk=kb_, v_blk=vb_, with_lse=True,
                           name=f"dilated{dil}")
                   for g, ((_, dil), src, (qb_, kb_, vb_)) in enumerate(zip(
                       DILATED_GROUPS, (y4, ya4, ya16), ((QA_B, KA_B, VA_B), (0, 2, 4), (0, 2, 4))))]
        (o_c,) = _banded(y4, bias_c, sinks[l], q_blk=QC_B, k_blk=KC_B, v_blk=VC_B, with_lse=False, name="swa")
        o_b = _moba(qb, kb, vb, bias_b, consts[4])
        o_d = _fox(qd, kd, vd)
        x = _outffn(x, a_parts, o_b, o_c.reshape(B, S, GROUP_W), o_d, mix_norm_g[l][None], wo_b[l],
                    norm2_g[l][None], wg_b[l], wu_b[l], wd_b[l], final_g[None], final=(l == depth - 1))
    return x
```

```python
import functools
import math

import numpy as np
import jax
import jax.numpy as jnp
from jax import lax
from jax.experimental import pallas as pl
from jax.experimental.pallas import tpu as pltpu

F32 = jnp.float32
BF16 = jnp.bfloat16

LANES = 128
VMEM_LIMIT_BYTES = 56 * 1024 * 1024

D_MODEL = 1024
HEAD_DIM = 64
N_HEADS = 4
GROUP_W = N_HEADS * HEAD_DIM
DILATED_GROUPS = ((128, 1), (512, 4), (2048, 16))
BAND = 128
MOBA_BLOCK = 256
MOBA_TOPK = 3
SWA_WINDOW = 128
N_BUCKETS = 32
T5_MAX_DISTANCE = 2048
D_FF = 2816
FF_CHUNK = 256
EPS = 1e-6
NEG = -1e30
SCALE = HEAD_DIM ** -0.5

LOG2E = math.log2(math.e)
QA_B, KA_B, VA_B, QC_B, KC_B, VC_B = 0, 2, 4, 6, 8, 10
NY = 12 * LANES
A_COLS = 3 * GROUP_W
WIDE = N_HEADS * LANES
QB_OFF, KB_OFF, VB_OFF, QD_OFF, KD_OFF, VD_OFF = (NY + g * GROUP_W for g in range(6))
FD_OFF = NY + 6 * GROUP_W
N_PROJ = FD_OFF + LANES
AUG0 = HEAD_DIM

PROJ_TM = 512
FLASH_T = 1024
FOX_TQ, FOX_TK = 1024, 1024
BANDED_UNROLL = 16


def _params(sem):
    return pltpu.CompilerParams(dimension_semantics=sem, vmem_limit_bytes=VMEM_LIMIT_BYTES)


def _resident(shape, index_map):
    return pl.BlockSpec(shape, index_map, pipeline_mode=pl.Buffered(1))


def _dot(a, b):
    return jnp.dot(a, b, preferred_element_type=F32)


def _dot_nt(a, b):
    return lax.dot_general(a, b, (((1,), (1,)), ((), ())), preferred_element_type=F32)


def _rms(x, g):
    return x * lax.rsqrt(jnp.mean(x * x, axis=-1, keepdims=True) + EPS) * g


def _rearrange_w_in(w):
    w = w.astype(BF16)
    sl = lambda a, b: w[..., a:b]
    kc0, kc1 = sl(1792, 1856), sl(1856, 1920)
    vc0, vc1 = sl(1920, 1984), sl(1984, 2048)
    fd = jnp.pad(sl(2816, 2820), [(0, 0)] * (w.ndim - 1) + [(0, LANES - N_HEADS)])
    out = jnp.concatenate([
        sl(0, 768),
        sl(1536, 1792),
        kc0, kc0, kc1, kc1,
        vc0, vc0, vc1, vc1,
        sl(768, 1536),
        sl(2048, 2816),
        fd], axis=-1)
    assert out.shape[-1] == N_PROJ
    return out


def _t5_bucket_np(dist):
    n = np.maximum(dist, 0)
    max_exact = N_BUCKETS // 2
    nf = np.maximum(n, 1).astype(np.float32)
    large = max_exact + (np.log(nf / np.float32(max_exact)) / np.float32(math.log(T5_MAX_DISTANCE / max_exact))
                         * np.float32(N_BUCKETS - max_exact)).astype(np.int32)
    large = np.minimum(large, N_BUCKETS - 1)
    return np.where(n < max_exact, n, large).astype(np.int32)


def _bucket_thresholds():
    n = np.arange(1 << 16)
    bucket = _t5_bucket_np(n)
    assert (np.diff(bucket) >= 0).all() and bucket[-1] == N_BUCKETS - 1
    thr = np.searchsorted(bucket, np.arange(N_BUCKETS), side="left")
    assert (np.searchsorted(thr, n, side="right") - 1 == bucket).all()
    return tuple(int(t) for t in thr)


def _placement_constants():
    pq = np.zeros((3 * LANES, WIDE), np.float32)
    pk = np.zeros((3 * LANES, WIDE), np.float32)
    cq = np.zeros((1, WIDE), np.float32)
    ck = np.zeros((1, WIDE), np.float32)
    for h in range(N_HEADS):
        for part in range(3):
            pq[part * LANES + h, h * LANES + AUG0 + part] = 1.0
            pk[part * LANES + h, h * LANES + AUG0 + 3 + part] = -1.0
            ck[0, h * LANES + AUG0 + part] = 1.0
            cq[0, h * LANES + AUG0 + 3 + part] = 1.0
    shift = np.zeros((LANES, LANES), np.float32)
    for n in range(AUG0):
        shift[n, AUG0 + n] = 1.0
    return (jnp.asarray(pq, BF16), jnp.asarray(pk, BF16), jnp.asarray(cq), jnp.asarray(ck),
            jnp.asarray(shift, BF16))


def _bias_tiles_kernel(base_ref, table_ref, o_ref, *, rows, cols, step, max_dist, scale, thresholds, chunk):
    t = pl.program_id(0)
    base = base_ref[t]

    def body(c, carry):
        r0 = pl.multiple_of(c * chunk, chunk)
        i = lax.broadcasted_iota(jnp.int32, (chunk, cols), 0) + r0
        j = lax.broadcasted_iota(jnp.int32, (chunk, cols), 1)
        dist = base + i - j
        ok = (dist >= 0) & (dist <= max_dist)
        scaled = dist * step
        vals = [jnp.full((chunk, cols), table_ref[0, h] * scale, F32) for h in range(N_HEADS)]
        for b in range(1, N_BUCKETS):
            reached = scaled >= thresholds[b]
            vals = [jnp.where(reached, table_ref[b, h] * scale, v) for h, v in enumerate(vals)]
        for h in range(N_HEADS):
            o_ref[h, pl.ds(r0, chunk), :] = jnp.where(ok, vals[h], NEG)
        return carry

    lax.fori_loop(0, rows // chunk, body, 0)


def _bias_tiles(table, bases, *, rows, cols, step, max_dist, name, scale=1.0):
    n_tiles = len(bases)
    return pl.pallas_call(
        functools.partial(_bias_tiles_kernel, rows=rows, cols=cols, step=step, max_dist=max_dist, scale=scale,
                          thresholds=_bucket_thresholds(), chunk=32),
        grid=(n_tiles,),
        in_specs=[pl.BlockSpec(memory_space=pltpu.SMEM), pl.BlockSpec(memory_space=pltpu.SMEM)],
        out_specs=pl.BlockSpec((N_HEADS, None, rows, cols), lambda t: (0, t, 0, 0)),
        out_shape=jax.ShapeDtypeStruct((N_HEADS, n_tiles, rows, cols), F32),
        compiler_params=_params(("arbitrary",)),
        name=name,
    )(jnp.asarray(bases, jnp.int32), table)


def _band_bias(table, step, max_dist, name):
    return _bias_tiles(table, (BAND, 0), rows=BAND, cols=2 * BAND, step=step, max_dist=max_dist, name=name)


def _moba_bias(table, n_blocks):
    bases = tuple(MOBA_BLOCK * d for d in range(1 - FLASH_T // MOBA_BLOCK, n_blocks))
    return _bias_tiles(table, bases, rows=MOBA_BLOCK, cols=MOBA_BLOCK, step=1, max_dist=1 << 30,
                       name="moba_bias", scale=LOG2E)


def _widen(a, low):
    outs = []
    for p in range(GROUP_W // LANES):
        pair = a[:, p * LANES:(p + 1) * LANES]
        outs.append(jnp.where(low, pair, 0.0))
        outs.append(jnp.where(low, pltpu.roll(pair, HEAD_DIM, axis=1), 0.0))
    return jnp.concatenate(outs, axis=1)


def _inproj_kernel(x_ref, g_ref, w_ref, fb_ref, pq_ref, pk_ref, cq_ref, ck_ref,
                   y_ref, ya4_ref, ya16_ref, qb_ref, kb_ref, vb_ref, qd_ref, kd_ref, vd_ref,
                   carry_ref, fold_ref, *, tm):
    t = pl.program_id(1)

    @pl.when(t == 0)
    def _():
        carry_ref[...] = jnp.zeros_like(carry_ref)

    h = _rms(x_ref[...], g_ref[...]).astype(BF16)

    def mm(c0, width=GROUP_W):
        return _dot(h, w_ref[:, c0:c0 + width])

    f = mm(FD_OFF, LANES) + fb_ref[...]
    logf = jnp.minimum(f, 0.0) - jnp.log(1.0 + jnp.exp(-jnp.abs(f)))
    srow = lax.broadcasted_iota(jnp.int32, (tm, LANES), 0)
    c = logf
    k = 1
    while k < tm:
        c = c + jnp.where(srow >= k, pltpu.roll(c, k, axis=0), 0.0)
        k *= 2
    c = c + carry_ref[0:1, :]
    carry_ref[0:1, :] = c[tm - 1:tm, :]

    c2 = c * LOG2E
    c_hi = c2.astype(BF16)
    r1 = c2 - c_hi.astype(F32)
    c_mid = r1.astype(BF16)
    c_lo = (r1 - c_mid.astype(F32)).astype(BF16)
    feats = jnp.concatenate([c_hi, c_mid, c_lo], axis=1)

    for grp in range(NY // GROUP_W):
        c0 = grp * GROUP_W
        a = mm(c0)
        if c0 in (QA_B * LANES, QC_B * LANES):
            a = a * SCALE
        y_ref[:, c0:c0 + GROUP_W] = a.astype(BF16)
        if c0 < A_COLS:
            fold_ref[2 * grp] = a[:, :LANES]
            fold_ref[2 * grp + 1] = a[:, LANES:]

    for dil, ref in ((4, ya4_ref), (16, ya16_ref)):
        for r in range(dil):
            for slab in range(A_COLS // LANES):
                piece = fold_ref[slab, pl.ds(r, tm // dil, stride=dil), :]
                ref[r, :, slab * LANES:(slab + 1) * LANES] = piece.astype(BF16)

    low = lax.broadcasted_iota(jnp.int32, (tm, LANES), 1) < HEAD_DIM
    lane = lax.broadcasted_iota(jnp.int32, (tm, WIDE), 1) % LANES
    row = lax.broadcasted_iota(jnp.int32, (tm, WIDE), 0)

    qb_ref[...] = _widen(mm(QB_OFF) * (SCALE * LOG2E), low).astype(BF16)
    vb_ref[...] = jnp.where(lane == AUG0, 1.0, _widen(mm(VB_OFF), low)).astype(BF16)
    vd_ref[...] = jnp.where(lane == AUG0, 1.0, _widen(mm(VD_OFF), low)).astype(BF16)
    blk = (t * tm + row) // MOBA_BLOCK
    kb_ref[...] = jnp.where(lane == AUG0 + blk, 1.0, _widen(mm(KB_OFF), low)).astype(BF16)

    qd = _widen(mm(QD_OFF) * (SCALE * LOG2E), low) + _dot(feats, pq_ref[...]) + cq_ref[...]
    kd = _widen(mm(KD_OFF), low) + _dot(feats, pk_ref[...]) + ck_ref[...]
    qd_ref[...] = qd.astype(BF16)
    kd_ref[...] = kd.astype(BF16)


def _inproj(x, g, w, fb, consts):
    B, S, D = x.shape
    tm = PROJ_TM
    pq, pk, cq, ck, _ = consts
    const = lambda b, t: (0, 0)
    tile = lambda b, t: (b, t, 0)
    fold_tile = lambda b, t: (b, 0, t, 0)
    wide_shape = jax.ShapeDtypeStruct((B, S, WIDE), BF16)
    out_shape = (jax.ShapeDtypeStruct((B, S, NY), BF16),
                 jax.ShapeDtypeStruct((B, 4, S // 4, A_COLS), BF16),
                 jax.ShapeDtypeStruct((B, 16, S // 16, A_COLS), BF16),
                 ) + (wide_shape,) * 6
    wide_spec = pl.BlockSpec((None, tm, WIDE), tile)
    return pl.pallas_call(
        functools.partial(_inproj_kernel, tm=tm),
        grid=(B, S // tm),
        in_specs=[pl.BlockSpec((None, tm, D), tile),
                  _resident((1, D), const),
                  _resident((D, N_PROJ), const),
                  _resident((1, LANES), const),
                  _resident((3 * LANES, WIDE), const),
                  _resident((3 * LANES, WIDE), const),
                  _resident((1, WIDE), const),
                  _resident((1, WIDE), const)],
        out_specs=(pl.BlockSpec((None, tm, NY), tile),
                   pl.BlockSpec((None, 4, tm // 4, A_COLS), fold_tile),
                   pl.BlockSpec((None, 16, tm // 16, A_COLS), fold_tile),
                   ) + (wide_spec,) * 6,
        out_shape=out_shape,
        scratch_shapes=[pltpu.VMEM((8, LANES), F32),
                        pltpu.VMEM((A_COLS // LANES, tm, LANES), F32)],
        compiler_params=_params(("arbitrary", "arbitrary")),
        name="inproj",
    )(x, g, w, fb, pq, pk, cq, ck)


def _banded_kernel(*refs, dil, n_tiles, with_sink, with_lse):
    refs = list(refs)
    q_ref, k_ref, v_ref, bias_ref = refs[:4]
    rest = refs[4:]
    sink_ref = rest.pop(0) if with_sink else None
    o_ref = rest.pop(0)
    lse_ref = rest.pop(0) if with_lse else None
    pair = pl.program_id(1)
    low = lax.broadcasted_iota(jnp.int32, (BAND, LANES), 1) < HEAD_DIM
    keep = (low.astype(F32).astype(BF16), (1.0 - low.astype(F32)).astype(BF16))

    def body(it, carry):
        r = it // n_tiles
        n = it % n_tiles
        r0 = pl.multiple_of(n * BAND, BAND)
        k0 = pl.multiple_of(jnp.maximum(r0 - BAND, 0), BAND)
        variant = jnp.where(n == 0, 1, 0)
        qt = q_ref[r, pl.ds(r0, BAND), :]
        kt = k_ref[r, pl.ds(k0, 2 * BAND), :]
        vt = v_ref[r, pl.ds(k0, 2 * BAND), :]
        outs, lses = [], []
        for hh in range(2):
            s = _dot_nt(qt * keep[hh], kt) + bias_ref[hh, variant]
            m = jnp.max(s, axis=1, keepdims=True)
            if with_sink:
                sink = sink_ref[2 * pair + hh]
                m = jnp.maximum(m, sink)
            p = jnp.exp(s - m)
            l = jnp.sum(p, axis=1, keepdims=True)
            if with_sink:
                l = l + jnp.exp(sink - m)
            outs.append(_dot(p.astype(BF16), vt) / l)
            lses.append(jnp.broadcast_to(m + jnp.log(l), (BAND, LANES)))
        o_ref[r, pl.ds(r0, BAND), :] = jnp.where(low, outs[0], outs[1])
        if with_lse:
            lse_ref[r, pl.ds(r0, BAND), :] = jnp.where(low, lses[0], lses[1])
        return carry

    lax.fori_loop(0, dil * n_tiles, body, 0, unroll=BANDED_UNROLL)


def _banded(src, bias, sinks, *, q_blk, k_blk, v_blk, with_lse, name):
    B, dil, L, _ = src.shape
    n_tiles = L // BAND
    assert n_tiles & (n_tiles - 1) == 0 and (dil * n_tiles) % BANDED_UNROLL == 0
    col = lambda blk: (lambda b, p: (b, 0, 0, blk + p))
    in_specs = [pl.BlockSpec((None, dil, L, LANES), col(q_blk)),
                pl.BlockSpec((None, dil, L, LANES), col(k_blk)),
                pl.BlockSpec((None, dil, L, LANES), col(v_blk)),
                pl.BlockSpec((2, 2, BAND, 2 * BAND), lambda b, p: (p, 0, 0, 0))]
    args = [src, src, src, bias]
    if sinks is not None:
        in_specs.append(pl.BlockSpec(memory_space=pltpu.SMEM))
        args.append(sinks)
    o_spec = pl.BlockSpec((None, dil, L, LANES), lambda b, p: (b, 0, 0, p))
    o_shape = jax.ShapeDtypeStruct((B, dil, L, GROUP_W), F32)
    n_out = 2 if with_lse else 1
    return pl.pallas_call(
        functools.partial(_banded_kernel, dil=dil, n_tiles=n_tiles, with_sink=sinks is not None,
                          with_lse=with_lse),
        grid=(B, 2),
        in_specs=in_specs,
        out_specs=(o_spec,) * n_out,
        out_shape=(o_shape,) * n_out,
        compiler_params=_params(("arbitrary",) * 2),
        name=name,
    )(*args)


def _online_update(scores, vts, carry):
    n_heads = len(scores)
    ms = []
    for hh in range(n_heads):
        m_new = carry[2 * hh]
        for s in scores[hh]:
            m_new = jnp.maximum(m_new, jnp.max(s, axis=1, keepdims=True))
        ms.append(m_new)
    new = []
    for hh in range(n_heads):
        m, acc = carry[2 * hh:2 * hh + 2]
        ps = [jnp.exp2(s - ms[hh]).astype(BF16) for s in scores[hh]]
        p = ps[0] if len(ps) == 1 else jnp.concatenate(ps, axis=1)
        acc = jnp.exp2(m - ms[hh]) * acc + _dot(p, vts[hh])
        new.extend((ms[hh], acc))
    return tuple(new)


def _flash_init(tq, n_heads=2):
    return (jnp.full((tq, 1), NEG, F32), jnp.zeros((tq, LANES), F32)) * n_heads


def _finish_pair(o_ref, carry, tq):
    low = lax.broadcasted_iota(jnp.int32, (tq, LANES), 1) < HEAD_DIM
    outs = [acc / acc[:, AUG0:AUG0 + 1] for acc in (carry[1], carry[3])]
    o_ref[...] = jnp.where(low, outs[0], pltpu.roll(outs[1], HEAD_DIM, axis=1))


def _moba_kernel(q_ref, k_ref, v_ref, bias_ref, shift_ref, o_ref, kmean_ref, *, n_blocks):
    i = pl.program_id(2)
    t, blk = FLASH_T, MOBA_BLOCK
    per = t // blk

    @pl.when(i == 0)
    def _():
        for hh in range(2):
            kf = k_ref[:, hh * LANES:(hh + 1) * LANES].astype(F32)
            kmean_ref[hh] = jnp.mean(kf.reshape(n_blocks, blk, LANES), axis=1)

    blk_id = lax.broadcasted_iota(jnp.int32, (n_blocks, t), 0)
    blk_f = blk_id.astype(F32)
    own = per * i + lax.broadcasted_iota(jnp.int32, (n_blocks, t), 1) // blk
    q_aug = []
    for hh in range(2):
        q = q_ref[:, hh * LANES:(hh + 1) * LANES]
        km = kmean_ref[hh]
        k1 = km.astype(BF16)
        r1 = km - k1.astype(F32)
        k2 = r1.astype(BF16)
        k3 = (r1 - k2.astype(F32)).astype(BF16)
        gate = _dot_nt(k1, q) + _dot_nt(k2, q) + _dot_nt(k3, q)

        avail = blk_id < own
        sel = jnp.zeros((n_blocks, t), jnp.bool_)
        for _ in range(MOBA_TOPK):
            cur = jnp.where(avail, gate, -jnp.inf)
            top = jnp.max(cur, axis=0, keepdims=True)
            is_top = avail & (cur == top)
            first = jnp.min(jnp.where(is_top, blk_f, float(n_blocks)), axis=0, keepdims=True)
            pick = blk_f == first
            sel = sel | pick
            avail = avail & jnp.logical_not(pick)
        penalty = jnp.where(sel | (blk_id == own), 0.0, NEG).astype(BF16)
        placed = lax.dot_general(penalty, shift_ref[0:n_blocks, :], (((0,), (0,)), ((), ())),
                                 preferred_element_type=F32)
        q_aug.append((q.astype(F32) + placed).astype(BF16))

    def tile(n, q_rows, k_lo, k_hi, carry_rows):
        r0 = pl.multiple_of(n * t + k_lo * blk, blk)
        width = (k_hi - k_lo) * blk
        d0 = per * (i - n) + per - 1
        scores, vts = [], []
        for hh in range(2):
            kt = k_ref[pl.ds(r0, width), hh * LANES:(hh + 1) * LANES]
            s = _dot_nt(q_aug[hh][q_rows[0] * blk:q_rows[1] * blk], kt)
            rows = [jnp.concatenate([s[(a - q_rows[0]) * blk:(a - q_rows[0] + 1) * blk,
                                       (b - k_lo) * blk:(b - k_lo + 1) * blk] + bias_ref[hh, d0 + a - b]
                                     for b in range(k_lo, k_hi)], axis=1) for a in range(*q_rows)]
            scores.append([jnp.concatenate(rows, axis=0)])
            vts.append(v_ref[pl.ds(r0, width), hh * LANES:(hh + 1) * LANES])
        return _online_update(scores, vts, carry_rows)

    carry = lax.fori_loop(0, i, lambda n, c: tile(n, (0, per), 0, per, c), _flash_init(t))
    hp = per // 2
    carry = tile(i, (0, per), 0, hp, carry)
    low_rows = tile(i, (hp, per), hp, per, tuple(c[hp * blk:] for c in carry))
    carry = tuple(jnp.concatenate([c[:hp * blk], lo], axis=0) for c, lo in zip(carry, low_rows))
    _finish_pair(o_ref, carry, t)


def _moba(qb, kb, vb, bias, shift):
    B, S, _ = qb.shape
    nb = S // MOBA_BLOCK
    t = FLASH_T
    assert nb <= AUG0 and t % (2 * MOBA_BLOCK) == 0 and S % t == 0
    n_bias = nb + t // MOBA_BLOCK - 1
    return pl.pallas_call(
        functools.partial(_moba_kernel, n_blocks=nb),
        grid=(B, 2, S // t),
        in_specs=[pl.BlockSpec((None, t, 2 * LANES), lambda b, p, i: (b, i, p)),
                  pl.BlockSpec((None, S, 2 * LANES), lambda b, p, i: (b, 0, p)),
                  pl.BlockSpec((None, S, 2 * LANES), lambda b, p, i: (b, 0, p)),
                  pl.BlockSpec((2, n_bias, MOBA_BLOCK, MOBA_BLOCK), lambda b, p, i: (p, 0, 0, 0)),
                  pl.BlockSpec((LANES, LANES), lambda b, p, i: (0, 0))],
        out_specs=pl.BlockSpec((None, t, LANES), lambda b, p, i: (b, i, p)),
        out_shape=jax.ShapeDtypeStruct((B, S, GROUP_W), F32),
        scratch_shapes=[pltpu.VMEM((2, nb, LANES), F32)],
        compiler_params=_params(("arbitrary",) * 3),
        name="moba",
    )(qb, kb, vb, bias, shift)


def _fox_kernel(q_ref, k_ref, v_ref, o_ref, *, tq, tk):
    i = pl.program_id(2)
    qs = [q_ref[:, hh * LANES:(hh + 1) * LANES] for hh in range(2)]

    def step(n, carry):
        r0 = pl.multiple_of(n * tk, tk)
        scores, vts = [], []
        for hh in range(2):
            kt = k_ref[pl.ds(r0, tk), hh * LANES:(hh + 1) * LANES]
            scores.append([_dot_nt(qs[hh], kt)])
            vts.append(v_ref[pl.ds(r0, tk), hh * LANES:(hh + 1) * LANES])
        return _online_update(scores, vts, carry)

    carry = lax.fori_loop(0, i, step, _flash_init(tq))
    half = tq // 2
    d0 = pl.multiple_of(i * tq, tq)

    def diag(q_lo, k0, carry_rows):
        rows = tq - q_lo
        row = lax.broadcasted_iota(jnp.int32, (rows, half), 0)
        col = lax.broadcasted_iota(jnp.int32, (rows, half), 1)
        scores, vts = [], []
        for hh in range(2):
            kt = k_ref[pl.ds(k0, half), hh * LANES:(hh + 1) * LANES]
            s = jnp.where(col <= row, _dot_nt(q_ref[q_lo:tq, hh * LANES:(hh + 1) * LANES], kt), NEG)
            scores.append([s])
            vts.append(v_ref[pl.ds(k0, half), hh * LANES:(hh + 1) * LANES])
        return _online_update(scores, vts, carry_rows)

    carry = diag(0, d0, carry)
    low_rows = diag(half, d0 + half, tuple(c[half:] for c in carry))
    carry = tuple(jnp.concatenate([c[:half], lo], axis=0) for c, lo in zip(carry, low_rows))
    _finish_pair(o_ref, carry, tq)


def _fox(qd, kd, vd):
    B, S, _ = qd.shape
    tq, tk = FOX_TQ, FOX_TK
    assert S % tk == 0 and tk == tq
    return pl.pallas_call(
        functools.partial(_fox_kernel, tq=tq, tk=tk),
        grid=(B, 2, S // tq),
        in_specs=[pl.BlockSpec((None, tq, 2 * LANES), lambda b, p, i: (b, i, p)),
                  pl.BlockSpec((None, S, 2 * LANES), lambda b, p, i: (b, 0, p)),
                  pl.BlockSpec((None, S, 2 * LANES), lambda b, p, i: (b, 0, p))],
        out_specs=pl.BlockSpec((None, tq, LANES), lambda b, p, i: (b, i, p)),
        out_shape=jax.ShapeDtypeStruct((B, S, GROUP_W), F32),
        compiler_params=_params(("arbitrary",) * 3),
        name="fox",
    )(qd, kd, vd)


def _outffn_kernel(x_ref, oa1_ref, oa4_ref, oa16_ref, la1_ref, la4_ref, la16_ref, ob_ref, oc_ref, od_ref,
                   mg_ref, wo_ref, g2_ref, wg_ref, wu_ref, wd_ref, fg_ref, out_ref, unfold_ref, *, final, tm):
    def unfold(ref, dil, slot):
        for r in range(dil):
            for s in range(GROUP_W // LANES):
                unfold_ref[slot, s, pl.ds(r, tm // dil, stride=dil), :] = ref[r, :, s * LANES:(s + 1) * LANES]
        return jnp.concatenate([unfold_ref[slot, s] for s in range(GROUP_W // LANES)], axis=1)

    l1, l4, l16 = la1_ref[...], unfold(la4_ref, 4, 0), unfold(la16_ref, 16, 1)
    o1, o4, o16 = oa1_ref[...], unfold(oa4_ref, 4, 2), unfold(oa16_ref, 16, 3)
    m = jnp.maximum(jnp.maximum(l1, l4), l16)
    e1, e4, e16 = jnp.exp(l1 - m), jnp.exp(l4 - m), jnp.exp(l16 - m)
    mix_a = (e1 * o1 + e4 * o4 + e16 * o16) / (e1 + e4 + e16)

    x1 = x_ref[...]
    for g, part in enumerate((mix_a, ob_ref[...], oc_ref[...], od_ref[...])):
        c0 = g * GROUP_W
        normed = _rms(part, mg_ref[:, c0:c0 + GROUP_W]).astype(BF16)
        x1 = x1 + _dot(normed, wo_ref[c0:c0 + GROUP_W, :])

    h = _rms(x1, g2_ref[...]).astype(BF16)
    ffn = jnp.zeros_like(x1)
    for c in range(D_FF // FF_CHUNK):
        c0 = c * FF_CHUNK
        gate = _dot(h, wg_ref[:, c0:c0 + FF_CHUNK])
        up = _dot(h, wu_ref[:, c0:c0 + FF_CHUNK])
        act = (gate * (1.0 / (1.0 + jnp.exp(-gate))) * up).astype(BF16)
        ffn = ffn + _dot(act, wd_ref[c0:c0 + FF_CHUNK, :])
    x2 = x1 + ffn
    if final:
        x2 = _rms(x2, fg_ref[...])
    out_ref[...] = x2


def _outffn(x, a_parts, ob, oc, od, mg, wo, g2, wg, wu, wd, fg, *, final):
    B, S, D = x.shape
    tm = PROJ_TM
    tile = lambda b, t: (b, t, 0)
    const = lambda b, t: (0, 0)
    flat_spec = pl.BlockSpec((None, tm, GROUP_W), tile)

    def fold_spec(dil):
        if dil == 1:
            return pl.BlockSpec((None, None, tm, GROUP_W), lambda b, t: (b, 0, t, 0))
        return pl.BlockSpec((None, dil, tm // dil, GROUP_W), lambda b, t: (b, 0, t, 0))

    dils = [dil for _, dil in DILATED_GROUPS]
    a_specs = [fold_spec(d) for d in dils] * 2
    a_args = [p[0] for p in a_parts] + [p[1] for p in a_parts]
    return pl.pallas_call(
        functools.partial(_outffn_kernel, final=final, tm=tm),
        grid=(B, S // tm),
        in_specs=[pl.BlockSpec((None, tm, D), tile)] + a_specs + [flat_spec] * 3 + [
            _resident((1, D), const),
            _resident((D, D), const),
            _resident((1, D), const),
            _resident((D, D_FF), const),
            _resident((D, D_FF), const),
            _resident((D_FF, D), const),
            _resident((1, D), const)],
        out_specs=pl.BlockSpec((None, tm, D), tile),
        out_shape=jax.ShapeDtypeStruct((B, S, D), F32),
        scratch_shapes=[pltpu.VMEM((4, GROUP_W // LANES, tm, LANES), F32)],
        compiler_params=_params(("arbitrary", "arbitrary")),
        name="outffn",
    )(x, *a_args, ob, oc, od, mg, wo, g2, wg, wu, wd, fg)


def kernel(x, norm1_g, w_in, f_bias, sinks, mix_norm_g, w_out, norm2_g, w_gate, w_up, w_down, rel_bias, final_g):
    B, S, D = x.shape
    depth = w_in.shape[0]
    max_dil = max(dil for _, dil in DILATED_GROUPS)
    assert D == D_MODEL and S % (2 * BAND * max_dil) == 0 and w_gate.shape[-1] == D_FF
    assert [dil for _, dil in DILATED_GROUPS] == [1, 4, 16] and S % PROJ_TM == 0

    w_proj = _rearrange_w_in(w_in)
    wo_b, wg_b, wu_b, wd_b = (w.astype(BF16) for w in (w_out, w_gate, w_up, w_down))
    fb = jnp.pad(f_bias, ((0, 0), (0, LANES - N_HEADS)))[:, None, :]
    consts = _placement_constants()
    table_a = rel_bias[:, :N_HEADS]
    table_b = rel_bias[:, N_HEADS:2 * N_HEADS]
    table_c = rel_bias[:, 2 * N_HEADS:]
    bias_a = [_band_bias(table_a, dil, window // dil, f"dilated{dil}_bias") for window, dil in DILATED_GROUPS]
    bias_c = _band_bias(table_c, 1, SWA_WINDOW - 1, "swa_bias")
    bias_b = _moba_bias(table_b, S // MOBA_BLOCK)

    for l in range(depth):
        y, ya4, ya16, qb, kb, vb, qd, kd, vd = _inproj(x, norm1_g[l][None], w_proj[l], fb[l], consts)
        y4 = y.reshape(B, 1, S, NY)
        a_parts = [_banded(src, bias_a[g], None, q_blk=qb_, k_blk=kb_, v_blk=vb_, with_lse=True,
                           name=f"dilated{dil}")
                   for g, ((_, dil), src, (qb_, kb_, vb_)) in enumerate(zip(
                       DILATED_GROUPS, (y4, ya4, ya16), ((QA_B, KA_B, VA_B), (0, 2, 4), (0, 2, 4))))]
        (o_c,) = _banded(y4, bias_c, sinks[l], q_blk=QC_B, k_blk=KC_B, v_blk=VC_B, with_lse=False, name="swa")
        o_b = _moba(qb, kb, vb, bias_b, consts[4])
        o_d = _fox(qd, kd, vd)
        x = _outffn(x, a_parts, o_b, o_c.reshape(B, S, GROUP_W), o_d, mix_norm_g[l][None], wo_b[l],
                    norm2_g[l][None], wg_b[l], wu_b[l], wd_b[l], final_g[None], final=(l == depth - 1))
    return x
```

```python
import functools
import math

import numpy as np
import jax
import jax.numpy as jnp
from jax import lax
from jax.experimental import pallas as pl
from jax.experimental.pallas import tpu as pltpu

F32 = jnp.float32
BF16 = jnp.bfloat16

LANES = 128
VMEM_LIMIT_BYTES = 56 * 1024 * 1024

D_MODEL = 1024
HEAD_DIM = 64
N_HEADS = 4
GROUP_W = N_HEADS * HEAD_DIM
DILATED_GROUPS = ((128, 1), (512, 4), (2048, 16))
BAND = 128
MOBA_BLOCK = 256
MOBA_TOPK = 3
SWA_WINDOW = 128
N_BUCKETS = 32
T5_MAX_DISTANCE = 2048
D_FF = 2816
FF_CHUNK = 256
EPS = 1e-6
NEG = -1e30
SCALE = HEAD_DIM ** -0.5

LOG2E = math.log2(math.e)
QA_B, KA_B, VA_B, QC_B, KC_B, VC_B = 0, 2, 4, 6, 8, 10
NY = 12 * LANES
A_COLS = 3 * GROUP_W
WIDE = N_HEADS * LANES
QB_OFF, KB_OFF, VB_OFF, QD_OFF, KD_OFF, VD_OFF = (NY + g * GROUP_W for g in range(6))
FD_OFF = NY + 6 * GROUP_W
N_PROJ = FD_OFF + LANES
AUG0 = HEAD_DIM

PROJ_TM = 512
FLASH_T = 1024
FOX_TQ, FOX_TK = 1024, 1024
BANDED_UNROLL = 16


def _params(sem):
    return pltpu.CompilerParams(dimension_semantics=sem, vmem_limit_bytes=VMEM_LIMIT_BYTES)


def _resident(shape, index_map):
    return pl.BlockSpec(shape, index_map, pipeline_mode=pl.Buffered(1))


def _dot(a, b):
    return jnp.dot(a, b, preferred_element_type=F32)


def _dot_nt(a, b):
    return lax.dot_general(a, b, (((1,), (1,)), ((), ())), preferred_element_type=F32)


def _rms(x, g):
    return x * lax.rsqrt(jnp.mean(x * x, axis=-1, keepdims=True) + EPS) * g


def _rearrange_w_in(w):
    w = w.astype(BF16)
    sl = lambda a, b: w[..., a:b]
    kc0, kc1 = sl(1792, 1856), sl(1856, 1920)
    vc0, vc1 = sl(1920, 1984), sl(1984, 2048)
    fd = jnp.pad(sl(2816, 2820), [(0, 0)] * (w.ndim - 1) + [(0, LANES - N_HEADS)])
    out = jnp.concatenate([
        sl(0, 768),
        sl(1536, 1792),
        kc0, kc0, kc1, kc1,
        vc0, vc0, vc1, vc1,
        sl(768, 1536),
        sl(2048, 2816),
        fd], axis=-1)
    assert out.shape[-1] == N_PROJ
    return out


def _t5_bucket_np(dist):
    n = np.maximum(dist, 0)
    max_exact = N_BUCKETS // 2
    nf = np.maximum(n, 1).astype(np.float32)
    large = max_exact + (np.log(nf / np.float32(max_exact)) / np.float32(math.log(T5_MAX_DISTANCE / max_exact))
                         * np.float32(N_BUCKETS - max_exact)).astype(np.int32)
    large = np.minimum(large, N_BUCKETS - 1)
    return np.where(n < max_exact, n, large).astype(np.int32)


def _bucket_thresholds():
    n = np.arange(1 << 16)
    bucket = _t5_bucket_np(n)
    assert (np.diff(bucket) >= 0).all() and bucket[-1] == N_BUCKETS - 1
    thr = np.searchsorted(bucket, np.arange(N_BUCKETS), side="left")
    assert (np.searchsorted(thr, n, side="right") - 1 == bucket).all()
    return tuple(int(t) for t in thr)


def _placement_constants():
    pq = np.zeros((LANES, WIDE), np.float32)
    pk = np.zeros((LANES, WIDE), np.float32)
    cq = np.zeros((1, WIDE), np.float32)
    ck = np.zeros((1, WIDE), np.float32)
    for h in range(N_HEADS):
        for part in range(3):
            pq[part * N_HEADS + h, h * LANES + AUG0 + part] = 1.0
            pk[part * N_HEADS + h, h * LANES + AUG0 + 3 + part] = -1.0
            ck[0, h * LANES + AUG0 + part] = 1.0
            cq[0, h * LANES + AUG0 + 3 + part] = 1.0
    shift = np.zeros((LANES, LANES), np.float32)
    for n in range(AUG0):
        shift[n, AUG0 + n] = 1.0
    return (jnp.asarray(pq, BF16), jnp.asarray(pk, BF16), jnp.asarray(cq), jnp.asarray(ck),
            jnp.asarray(shift, BF16))


def _bias_tiles_kernel(base_ref, table_ref, o_ref, *, rows, cols, step, max_dist, scale, thresholds, chunk):
    t = pl.program_id(0)
    base = base_ref[t]

    def body(c, carry):
        r0 = pl.multiple_of(c * chunk, chunk)
        i = lax.broadcasted_iota(jnp.int32, (chunk, cols), 0) + r0
        j = lax.broadcasted_iota(jnp.int32, (chunk, cols), 1)
        dist = base + i - j
        ok = (dist >= 0) & (dist <= max_dist)
        scaled = dist * step
        vals = [jnp.full((chunk, cols), table_ref[0, h] * scale, F32) for h in range(N_HEADS)]
        for b in range(1, N_BUCKETS):
            reached = scaled >= thresholds[b]
            vals = [jnp.where(reached, table_ref[b, h] * scale, v) for h, v in enumerate(vals)]
        for h in range(N_HEADS):
            o_ref[h, pl.ds(r0, chunk), :] = jnp.where(ok, vals[h], NEG)
        return carry

    lax.fori_loop(0, rows // chunk, body, 0)


def _bias_tiles(table, bases, *, rows, cols, step, max_dist, name, scale=1.0):
    n_tiles = len(bases)
    return pl.pallas_call(
        functools.partial(_bias_tiles_kernel, rows=rows, cols=cols, step=step, max_dist=max_dist, scale=scale,
                          thresholds=_bucket_thresholds(), chunk=32),
        grid=(n_tiles,),
        in_specs=[pl.BlockSpec(memory_space=pltpu.SMEM), pl.BlockSpec(memory_space=pltpu.SMEM)],
        out_specs=pl.BlockSpec((N_HEADS, None, rows, cols), lambda t: (0, t, 0, 0)),
        out_shape=jax.ShapeDtypeStruct((N_HEADS, n_tiles, rows, cols), F32),
        compiler_params=_params(("arbitrary",)),
        name=name,
    )(jnp.asarray(bases, jnp.int32), table)


def _band_bias(table, step, max_dist, name):
    return _bias_tiles(table, (BAND, 0), rows=BAND, cols=2 * BAND, step=step, max_dist=max_dist, name=name)


def _moba_bias(table, n_blocks):
    bases = tuple(MOBA_BLOCK * d for d in range(1 - FLASH_T // MOBA_BLOCK, n_blocks))
    return _bias_tiles(table, bases, rows=MOBA_BLOCK, cols=MOBA_BLOCK, step=1, max_dist=1 << 30,
                       name="moba_bias", scale=LOG2E)


def _widen(a, low):
    outs = []
    for p in range(GROUP_W // LANES):
        pair = a[:, p * LANES:(p + 1) * LANES]
        outs.append(jnp.where(low, pair, 0.0))
        outs.append(jnp.where(low, pltpu.roll(pair, HEAD_DIM, axis=1), 0.0))
    return jnp.concatenate(outs, axis=1)


def _inproj_kernel(x_ref, g_ref, w_ref, fb_ref, pq_ref, pk_ref, cq_ref, ck_ref,
                   y_ref, ya4_ref, ya16_ref, qb_ref, kb_ref, vb_ref, qd_ref, kd_ref, vd_ref,
                   carry_ref, fold_ref, *, tm):
    t = pl.program_id(1)

    @pl.when(t == 0)
    def _():
        carry_ref[...] = jnp.zeros_like(carry_ref)

    h = _rms(x_ref[...], g_ref[...]).astype(BF16)

    def mm(c0, width=GROUP_W):
        return _dot(h, w_ref[:, c0:c0 + width])

    f = mm(FD_OFF, LANES) + fb_ref[...]
    logf = jnp.minimum(f, 0.0) - jnp.log(1.0 + jnp.exp(-jnp.abs(f)))
    srow = lax.broadcasted_iota(jnp.int32, (tm, LANES), 0)
    c = logf
    k = 1
    while k < tm:
        c = c + jnp.where(srow >= k, pltpu.roll(c, k, axis=0), 0.0)
        k *= 2
    c = c + carry_ref[0:1, :]
    carry_ref[0:1, :] = c[tm - 1:tm, :]

    c2 = c * LOG2E
    c_hi = c2.astype(BF16)
    r1 = c2 - c_hi.astype(F32)
    c_mid = r1.astype(BF16)
    c_lo = (r1 - c_mid.astype(F32)).astype(BF16)
    head_lane = lax.broadcasted_iota(jnp.int32, (tm, LANES), 1) < N_HEADS
    pieces = [jnp.where(head_lane, piece.astype(F32), 0.0) for piece in (c_hi, c_mid, c_lo)]
    feats = (pieces[0] + pltpu.roll(pieces[1], N_HEADS, axis=1)
             + pltpu.roll(pieces[2], 2 * N_HEADS, axis=1)).astype(BF16)

    for grp in range(NY // GROUP_W):
        c0 = grp * GROUP_W
        a = mm(c0)
        if c0 in (QA_B * LANES, QC_B * LANES):
            a = a * SCALE
        y_ref[:, c0:c0 + GROUP_W] = a.astype(BF16)
        if c0 < A_COLS:
            fold_ref[2 * grp] = a[:, :LANES]
            fold_ref[2 * grp + 1] = a[:, LANES:]

    for dil, ref in ((4, ya4_ref), (16, ya16_ref)):
        for r in range(dil):
            for slab in range(A_COLS // LANES):
                piece = fold_ref[slab, pl.ds(r, tm // dil, stride=dil), :]
                ref[r, :, slab * LANES:(slab + 1) * LANES] = piece.astype(BF16)

    low = lax.broadcasted_iota(jnp.int32, (tm, LANES), 1) < HEAD_DIM
    lane = lax.broadcasted_iota(jnp.int32, (tm, WIDE), 1) % LANES
    row = lax.broadcasted_iota(jnp.int32, (tm, WIDE), 0)

    qb_ref[...] = _widen(mm(QB_OFF) * (SCALE * LOG2E), low).astype(BF16)
    vb_ref[...] = jnp.where(lane == AUG0, 1.0, _widen(mm(VB_OFF), low)).astype(BF16)
    vd_ref[...] = jnp.where(lane == AUG0, 1.0, _widen(mm(VD_OFF), low)).astype(BF16)
    blk = (t * tm + row) // MOBA_BLOCK
    kb_ref[...] = jnp.where(lane == AUG0 + blk, 1.0, _widen(mm(KB_OFF), low)).astype(BF16)

    qd = _widen(mm(QD_OFF) * (SCALE * LOG2E), low) + _dot(feats, pq_ref[...]) + cq_ref[...]
    kd = _widen(mm(KD_OFF), low) + _dot(feats, pk_ref[...]) + ck_ref[...]
    qd_ref[...] = qd.astype(BF16)
    kd_ref[...] = kd.astype(BF16)


def _inproj(x, g, w, fb, consts):
    B, S, D = x.shape
    tm = PROJ_TM
    pq, pk, cq, ck, _ = consts
    const = lambda b, t: (0, 0)
    tile = lambda b, t: (b, t, 0)
    fold_tile = lambda b, t: (b, 0, t, 0)
    wide_shape = jax.ShapeDtypeStruct((B, S, WIDE), BF16)
    out_shape = (jax.ShapeDtypeStruct((B, S, NY), BF16),
                 jax.ShapeDtypeStruct((B, 4, S // 4, A_COLS), BF16),
                 jax.ShapeDtypeStruct((B, 16, S // 16, A_COLS), BF16),
                 ) + (wide_shape,) * 6
    wide_spec = pl.BlockSpec((None, tm, WIDE), tile)
    return pl.pallas_call(
        functools.partial(_inproj_kernel, tm=tm),
        grid=(B, S // tm),
        in_specs=[pl.BlockSpec((None, tm, D), tile),
                  _resident((1, D), const),
                  _resident((D, N_PROJ), const),
                  _resident((1, LANES), const),
                  _resident((LANES, WIDE), const),
                  _resident((LANES, WIDE), const),
                  _resident((1, WIDE), const),
                  _resident((1, WIDE), const)],
        out_specs=(pl.BlockSpec((None, tm, NY), tile),
                   pl.BlockSpec((None, 4, tm // 4, A_COLS), fold_tile),
                   pl.BlockSpec((None, 16, tm // 16, A_COLS), fold_tile),
                   ) + (wide_spec,) * 6,
        out_shape=out_shape,
        scratch_shapes=[pltpu.VMEM((8, LANES), F32),
                        pltpu.VMEM((A_COLS // LANES, tm, LANES), F32)],
        compiler_params=_params(("arbitrary", "arbitrary")),
        name="inproj",
    )(x, g, w, fb, pq, pk, cq, ck)


def _banded_kernel(*refs, dil, n_tiles, with_sink):
    refs = list(refs)
    q_ref, k_ref, v_ref, bias_ref = refs[:4]
    rest = refs[4:]
    sink_ref = rest.pop(0) if with_sink else None
    o_ref, m_ref, l_ref = rest
    pair = pl.program_id(1)
    low = lax.broadcasted_iota(jnp.int32, (BAND, LANES), 1) < HEAD_DIM
    keep = (low.astype(F32).astype(BF16), (1.0 - low.astype(F32)).astype(BF16))

    def body(it, carry):
        r = it // n_tiles
        n = it % n_tiles
        r0 = pl.multiple_of(n * BAND, BAND)
        k0 = pl.multiple_of(jnp.maximum(r0 - BAND, 0), BAND)
        variant = jnp.where(n == 0, 1, 0)
        qt = q_ref[r, pl.ds(r0, BAND), :]
        kt = k_ref[r, pl.ds(k0, 2 * BAND), :]
        vt = v_ref[r, pl.ds(k0, 2 * BAND), :]
        accs, ms, ls = [], [], []
        for hh in range(2):
            s = _dot_nt(qt * keep[hh], kt) + bias_ref[hh, variant]
            m = jnp.max(s, axis=1, keepdims=True)
            if with_sink:
                sink = sink_ref[2 * pair + hh]
                m = jnp.maximum(m, sink)
            p = jnp.exp(s - m)
            l = jnp.sum(p, axis=1, keepdims=True)
            if with_sink:
                l = l + jnp.exp(sink - m)
            accs.append(_dot(p.astype(BF16), vt))
            ms.append(jnp.broadcast_to(m, (BAND, LANES)))
            ls.append(jnp.broadcast_to(l, (BAND, LANES)))
        o_ref[r, pl.ds(r0, BAND), :] = jnp.where(low, accs[0], accs[1])
        m_ref[r, pl.ds(r0, BAND), :] = jnp.where(low, ms[0], ms[1])
        l_ref[r, pl.ds(r0, BAND), :] = jnp.where(low, ls[0], ls[1])
        return carry

    lax.fori_loop(0, dil * n_tiles, body, 0, unroll=BANDED_UNROLL)


def _banded(src, bias, sinks, *, q_blk, k_blk, v_blk, name):
    B, dil, L, _ = src.shape
    n_tiles = L // BAND
    assert n_tiles & (n_tiles - 1) == 0 and (dil * n_tiles) % BANDED_UNROLL == 0
    col = lambda blk: (lambda b, p: (b, 0, 0, blk + p))
    in_specs = [pl.BlockSpec((None, dil, L, LANES), col(q_blk)),
                pl.BlockSpec((None, dil, L, LANES), col(k_blk)),
                pl.BlockSpec((None, dil, L, LANES), col(v_blk)),
                pl.BlockSpec((2, 2, BAND, 2 * BAND), lambda b, p: (p, 0, 0, 0))]
    args = [src, src, src, bias]
    if sinks is not None:
        in_specs.append(pl.BlockSpec(memory_space=pltpu.SMEM))
        args.append(sinks)
    o_spec = pl.BlockSpec((None, dil, L, LANES), lambda b, p: (b, 0, 0, p))
    o_shape = jax.ShapeDtypeStruct((B, dil, L, GROUP_W), F32)
    n_out = 3
    return pl.pallas_call(
        functools.partial(_banded_kernel, dil=dil, n_tiles=n_tiles, with_sink=sinks is not None),
        grid=(B, 2),
        in_specs=in_specs,
        out_specs=(o_spec,) * n_out,
        out_shape=(o_shape,) * n_out,
        compiler_params=_params(("arbitrary",) * 2),
        name=name,
    )(*args)


def _online_update(scores, vts, carry):
    n_heads = len(scores)
    ms = []
    for hh in range(n_heads):
        m_new = carry[2 * hh]
        for s in scores[hh]:
            m_new = jnp.maximum(m_new, jnp.max(s, axis=1, keepdims=True))
        ms.append(m_new)
    new = []
    for hh in range(n_heads):
        m, acc = carry[2 * hh:2 * hh + 2]
        ps = [jnp.exp2(s - ms[hh]).astype(BF16) for s in scores[hh]]
        p = ps[0] if len(ps) == 1 else jnp.concatenate(ps, axis=1)
        acc = jnp.exp2(m - ms[hh]) * acc + _dot(p, vts[hh])
        new.extend((ms[hh], acc))
    return tuple(new)


def _flash_init(tq, n_heads=2):
    return (jnp.full((tq, 1), NEG, F32), jnp.zeros((tq, LANES), F32)) * n_heads


def _finish_pair(o_ref, carry, tq):
    low = lax.broadcasted_iota(jnp.int32, (tq, LANES), 1) < HEAD_DIM
    outs = [acc / acc[:, AUG0:AUG0 + 1] for acc in (carry[1], carry[3])]
    o_ref[...] = jnp.where(low, outs[0], pltpu.roll(outs[1], HEAD_DIM, axis=1))


def _moba_kernel(q_ref, k_ref, v_ref, bias_ref, shift_ref, o_ref, kmean_ref, *, n_blocks):
    i = pl.program_id(2)
    t, blk = FLASH_T, MOBA_BLOCK
    per = t // blk

    @pl.when(i == 0)
    def _():
        for hh in range(2):
            kf = k_ref[:, hh * LANES:(hh + 1) * LANES].astype(F32)
            kmean_ref[hh] = jnp.mean(kf.reshape(n_blocks, blk, LANES), axis=1)

    blk_id = lax.broadcasted_iota(jnp.int32, (n_blocks, t), 0)
    blk_f = blk_id.astype(F32)
    own = per * i + lax.broadcasted_iota(jnp.int32, (n_blocks, t), 1) // blk
    q_aug = []
    for hh in range(2):
        q = q_ref[:, hh * LANES:(hh + 1) * LANES]
        km = kmean_ref[hh]
        k1 = km.astype(BF16)
        r1 = km - k1.astype(F32)
        k2 = r1.astype(BF16)
        k3 = (r1 - k2.astype(F32)).astype(BF16)
        gate = _dot_nt(k1, q) + _dot_nt(k2, q) + _dot_nt(k3, q)

        avail = blk_id < own
        sel = jnp.zeros((n_blocks, t), jnp.bool_)
        for _ in range(MOBA_TOPK):
            cur = jnp.where(avail, gate, -jnp.inf)
            top = jnp.max(cur, axis=0, keepdims=True)
            is_top = avail & (cur == top)
            first = jnp.min(jnp.where(is_top, blk_f, float(n_blocks)), axis=0, keepdims=True)
            pick = blk_f == first
            sel = sel | pick
            avail = avail & jnp.logical_not(pick)
        penalty = jnp.where(sel | (blk_id == own), 0.0, NEG).astype(BF16)
        placed = lax.dot_general(penalty, shift_ref[0:n_blocks, :], (((0,), (0,)), ((), ())),
                                 preferred_element_type=F32)
        q_aug.append((q.astype(F32) + placed).astype(BF16))

    def tile(n, q_rows, k_lo, k_hi, carry_rows):
        r0 = pl.multiple_of(n * t + k_lo * blk, blk)
        width = (k_hi - k_lo) * blk
        d0 = per * (i - n) + per - 1
        scores, vts = [], []
        for hh in range(2):
            kt = k_ref[pl.ds(r0, width), hh * LANES:(hh + 1) * LANES]
            s = _dot_nt(q_aug[hh][q_rows[0] * blk:q_rows[1] * blk], kt)
            rows = [jnp.concatenate([s[(a - q_rows[0]) * blk:(a - q_rows[0] + 1) * blk,
                                       (b - k_lo) * blk:(b - k_lo + 1) * blk] + bias_ref[hh, d0 + a - b]
                                     for b in range(k_lo, k_hi)], axis=1) for a in range(*q_rows)]
            scores.append([jnp.concatenate(rows, axis=0)])
            vts.append(v_ref[pl.ds(r0, width), hh * LANES:(hh + 1) * LANES])
        return _online_update(scores, vts, carry_rows)

    carry = lax.fori_loop(0, i, lambda n, c: tile(n, (0, per), 0, per, c), _flash_init(t))
    hp = per // 2
    carry = tile(i, (0, per), 0, hp, carry)
    low_rows = tile(i, (hp, per), hp, per, tuple(c[hp * blk:] for c in carry))
    carry = tuple(jnp.concatenate([c[:hp * blk], lo], axis=0) for c, lo in zip(carry, low_rows))
    _finish_pair(o_ref, carry, t)


def _moba(qb, kb, vb, bias, shift):
    B, S, _ = qb.shape
    nb = S // MOBA_BLOCK
    t = FLASH_T
    assert nb <= AUG0 and t % (2 * MOBA_BLOCK) == 0 and S % t == 0
    n_bias = nb + t // MOBA_BLOCK - 1
    return pl.pallas_call(
        functools.partial(_moba_kernel, n_blocks=nb),
        grid=(B, 2, S // t),
        in_specs=[pl.BlockSpec((None, t, 2 * LANES), lambda b, p, i: (b, i, p)),
                  pl.BlockSpec((None, S, 2 * LANES), lambda b, p, i: (b, 0, p)),
                  pl.BlockSpec((None, S, 2 * LANES), lambda b, p, i: (b, 0, p)),
                  pl.BlockSpec((2, n_bias, MOBA_BLOCK, MOBA_BLOCK), lambda b, p, i: (p, 0, 0, 0)),
                  pl.BlockSpec((LANES, LANES), lambda b, p, i: (0, 0))],
        out_specs=pl.BlockSpec((None, t, LANES), lambda b, p, i: (b, i, p)),
        out_shape=jax.ShapeDtypeStruct((B, S, GROUP_W), F32),
        scratch_shapes=[pltpu.VMEM((2, nb, LANES), F32)],
        compiler_params=_params(("arbitrary",) * 3),
        name="moba",
    )(qb, kb, vb, bias, shift)


def _fox_kernel(q_ref, k_ref, v_ref, o_ref, *, tq, tk):
    i = pl.program_id(2)
    qs = [q_ref[:, hh * LANES:(hh + 1) * LANES] for hh in range(2)]

    def step(n, carry):
        r0 = pl.multiple_of(n * tk, tk)
        scores, vts = [], []
        for hh in range(2):
            kt = k_ref[pl.ds(r0, tk), hh * LANES:(hh + 1) * LANES]
            scores.append([_dot_nt(qs[hh], kt)])
            vts.append(v_ref[pl.ds(r0, tk), hh * LANES:(hh + 1) * LANES])
        return _online_update(scores, vts, carry)

    carry = lax.fori_loop(0, i, step, _flash_init(tq))
    half = tq // 2
    d0 = pl.multiple_of(i * tq, tq)

    def diag(q_lo, k0, carry_rows):
        rows = tq - q_lo
        row = lax.broadcasted_iota(jnp.int32, (rows, half), 0)
        col = lax.broadcasted_iota(jnp.int32, (rows, half), 1)
        scores, vts = [], []
        for hh in range(2):
            kt = k_ref[pl.ds(k0, half), hh * LANES:(hh + 1) * LANES]
            s = jnp.where(col <= row, _dot_nt(q_ref[q_lo:tq, hh * LANES:(hh + 1) * LANES], kt), NEG)
            scores.append([s])
            vts.append(v_ref[pl.ds(k0, half), hh * LANES:(hh + 1) * LANES])
        return _online_update(scores, vts, carry_rows)

    carry = diag(0, d0, carry)
    low_rows = diag(half, d0 + half, tuple(c[half:] for c in carry))
    carry = tuple(jnp.concatenate([c[:half], lo], axis=0) for c, lo in zip(carry, low_rows))
    _finish_pair(o_ref, carry, tq)


def _fox(qd, kd, vd):
    B, S, _ = qd.shape
    tq, tk = FOX_TQ, FOX_TK
    assert S % tk == 0 and tk == tq
    return pl.pallas_call(
        functools.partial(_fox_kernel, tq=tq, tk=tk),
        grid=(B, 2, S // tq),
        in_specs=[pl.BlockSpec((None, tq, 2 * LANES), lambda b, p, i: (b, i, p)),
                  pl.BlockSpec((None, S, 2 * LANES), lambda b, p, i: (b, 0, p)),
                  pl.BlockSpec((None, S, 2 * LANES), lambda b, p, i: (b, 0, p))],
        out_specs=pl.BlockSpec((None, tq, LANES), lambda b, p, i: (b, i, p)),
        out_shape=jax.ShapeDtypeStruct((B, S, GROUP_W), F32),
        compiler_params=_params(("arbitrary",) * 3),
        name="fox",
    )(qd, kd, vd)


def _outffn_kernel(x_ref, a1_ref, a4_ref, a16_ref, m1_ref, m4_ref, m16_ref, l1_ref, l4_ref, l16_ref,
                   ob_ref, oc_ref, lc_ref, od_ref,
                   mg_ref, wo_ref, g2_ref, wg_ref, wu_ref, wd_ref, fg_ref, out_ref, unfold_ref, *, final, tm):
    def unfold(ref, dil, slot):
        for r in range(dil):
            for s in range(GROUP_W // LANES):
                unfold_ref[slot, s, pl.ds(r, tm // dil, stride=dil), :] = ref[r, :, s * LANES:(s + 1) * LANES]
        return jnp.concatenate([unfold_ref[slot, s] for s in range(GROUP_W // LANES)], axis=1)

    m1, m4, m16 = m1_ref[...], unfold(m4_ref, 4, 0), unfold(m16_ref, 16, 1)
    l1, l4, l16 = l1_ref[...], unfold(l4_ref, 4, 2), unfold(l16_ref, 16, 3)
    a1, a4, a16 = a1_ref[...], unfold(a4_ref, 4, 4), unfold(a16_ref, 16, 5)
    m = jnp.maximum(jnp.maximum(m1, m4), m16)
    e1, e4, e16 = jnp.exp(m1 - m), jnp.exp(m4 - m), jnp.exp(m16 - m)
    mix_a = (e1 * a1 + e4 * a4 + e16 * a16) / (e1 * l1 + e4 * l4 + e16 * l16)
    mix_c = oc_ref[...] / lc_ref[...]

    x1 = x_ref[...]
    for g, part in enumerate((mix_a, ob_ref[...], mix_c, od_ref[...])):
        c0 = g * GROUP_W
        normed = _rms(part, mg_ref[:, c0:c0 + GROUP_W]).astype(BF16)
        x1 = x1 + _dot(normed, wo_ref[c0:c0 + GROUP_W, :])

    h = _rms(x1, g2_ref[...]).astype(BF16)
    ffn = jnp.zeros_like(x1)
    for c in range(D_FF // FF_CHUNK):
        c0 = c * FF_CHUNK
        gate = _dot(h, wg_ref[:, c0:c0 + FF_CHUNK])
        up = _dot(h, wu_ref[:, c0:c0 + FF_CHUNK])
        act = (gate * (1.0 / (1.0 + jnp.exp(-gate))) * up).astype(BF16)
        ffn = ffn + _dot(act, wd_ref[c0:c0 + FF_CHUNK, :])
    x2 = x1 + ffn
    if final:
        x2 = _rms(x2, fg_ref[...])
    out_ref[...] = x2


def _outffn(x, a_parts, ob, c_parts, od, mg, wo, g2, wg, wu, wd, fg, *, final):
    B, S, D = x.shape
    tm = PROJ_TM
    tile = lambda b, t: (b, t, 0)
    const = lambda b, t: (0, 0)
    flat_spec = pl.BlockSpec((None, tm, GROUP_W), tile)

    def fold_spec(dil):
        if dil == 1:
            return pl.BlockSpec((None, None, tm, GROUP_W), lambda b, t: (b, 0, t, 0))
        return pl.BlockSpec((None, dil, tm // dil, GROUP_W), lambda b, t: (b, 0, t, 0))

    dils = [dil for _, dil in DILATED_GROUPS]
    a_specs = [fold_spec(d) for d in dils] * 3
    a_args = [p[k] for k in range(3) for p in a_parts]
    return pl.pallas_call(
        functools.partial(_outffn_kernel, final=final, tm=tm),
        grid=(B, S // tm),
        in_specs=[pl.BlockSpec((None, tm, D), tile)] + a_specs + [flat_spec] + [fold_spec(1)] * 2 + [flat_spec] + [
            _resident((1, D), const),
            _resident((D, D), const),
            _resident((1, D), const),
            _resident((D, D_FF), const),
            _resident((D, D_FF), const),
            _resident((D_FF, D), const),
            _resident((1, D), const)],
        out_specs=pl.BlockSpec((None, tm, D), tile),
        out_shape=jax.ShapeDtypeStruct((B, S, D), F32),
        scratch_shapes=[pltpu.VMEM((6, GROUP_W // LANES, tm, LANES), F32)],
        compiler_params=_params(("arbitrary", "arbitrary")),
        name="outffn",
    )(x, *a_args, ob, c_parts[0], c_parts[2], od, mg, wo, g2, wg, wu, wd, fg)


def kernel(x, norm1_g, w_in, f_bias, sinks, mix_norm_g, w_out, norm2_g, w_gate, w_up, w_down, rel_bias, final_g):
    B, S, D = x.shape
    depth = w_in.shape[0]
    max_dil = max(dil for _, dil in DILATED_GROUPS)
    assert D == D_MODEL and S % (2 * BAND * max_dil) == 0 and w_gate.shape[-1] == D_FF
    assert [dil for _, dil in DILATED_GROUPS] == [1, 4, 16] and S % PROJ_TM == 0

    w_proj = _rearrange_w_in(w_in)
    wo_b, wg_b, wu_b, wd_b = (w.astype(BF16) for w in (w_out, w_gate, w_up, w_down))
    fb = jnp.pad(f_bias, ((0, 0), (0, LANES - N_HEADS)))[:, None, :]
    consts = _placement_constants()
    table_a = rel_bias[:, :N_HEADS]
    table_b = rel_bias[:, N_HEADS:2 * N_HEADS]
    table_c = rel_bias[:, 2 * N_HEADS:]
    bias_a = [_band_bias(table_a, dil, window // dil, f"dilated{dil}_bias") for window, dil in DILATED_GROUPS]
    bias_c = _band_bias(table_c, 1, SWA_WINDOW - 1, "swa_bias")
    bias_b = _moba_bias(table_b, S // MOBA_BLOCK)

    for l in range(depth):
        y, ya4, ya16, qb, kb, vb, qd, kd, vd = _inproj(x, norm1_g[l][None], w_proj[l], fb[l], consts)
        y4 = y.reshape(B, 1, S, NY)
        a_parts = [_banded(src, bias_a[g], None, q_blk=qb_, k_blk=kb_, v_blk=vb_, name=f"dilated{dil}")
                   for g, ((_, dil), src, (qb_, kb_, vb_)) in enumerate(zip(
                       DILATED_GROUPS, (y4, ya4, ya16), ((QA_B, KA_B, VA_B), (0, 2, 4), (0, 2, 4))))]
        c_parts = _banded(y4, bias_c, sinks[l], q_blk=QC_B, k_blk=KC_B, v_blk=VC_B, name="swa")
        o_b = _moba(qb, kb, vb, bias_b, consts[4])
        o_d = _fox(qd, kd, vd)
        x = _outffn(x, a_parts, o_b, c_parts, o_d, mix_norm_g[l][None], wo_b[l],
                    norm2_g[l][None], wg_b[l], wu_b[l], wd_b[l], final_g[None], final=(l == depth - 1))
    return x
```

```python
import functools
import math

import numpy as np
import jax
import jax.numpy as jnp
from jax import lax
from jax.experimental import pallas as pl
from jax.experimental.pallas import tpu as pltpu

F32 = jnp.float32
BF16 = jnp.bfloat16

LANES = 128
VMEM_LIMIT_BYTES = 56 * 1024 * 1024

D_MODEL = 1024
HEAD_DIM = 64
N_HEADS = 4
GROUP_W = N_HEADS * HEAD_DIM
DILATED_GROUPS = ((128, 1), (512, 4), (2048, 16))
BAND = 128
MOBA_BLOCK = 256
MOBA_TOPK = 3
SWA_WINDOW = 128
N_BUCKETS = 32
T5_MAX_DISTANCE = 2048
D_FF = 2816
FF_CHUNK = 256
EPS = 1e-6
NEG = -1e30
SCALE = HEAD_DIM ** -0.5

LOG2E = math.log2(math.e)
QA_B, KA_B, VA_B, QC_B, KC_B, VC_B = 0, 2, 4, 6, 8, 10
NY = 12 * LANES
A_COLS = 3 * GROUP_W
WIDE = N_HEADS * LANES
QB_OFF, KB_OFF, VB_OFF, QD_OFF, KD_OFF, VD_OFF = (NY + g * GROUP_W for g in range(6))
FD_OFF = NY + 6 * GROUP_W
N_PROJ = FD_OFF + LANES
AUG0 = HEAD_DIM

PROJ_TM = 512
FLASH_T = 1024
FOX_TQ, FOX_TK = 1024, 1024
BANDED_UNROLL = 16


def _params(sem):
    return pltpu.CompilerParams(dimension_semantics=sem, vmem_limit_bytes=VMEM_LIMIT_BYTES)


def _resident(shape, index_map):
    return pl.BlockSpec(shape, index_map, pipeline_mode=pl.Buffered(1))


def _dot(a, b):
    return jnp.dot(a, b, preferred_element_type=F32)


def _dot_nt(a, b):
    return lax.dot_general(a, b, (((1,), (1,)), ((), ())), preferred_element_type=F32)


def _rms(x, g):
    return x * lax.rsqrt(jnp.mean(x * x, axis=-1, keepdims=True) + EPS) * g


def _rearrange_w_in(w):
    w = w.astype(BF16)
    sl = lambda a, b: w[..., a:b]
    kc0, kc1 = sl(1792, 1856), sl(1856, 1920)
    vc0, vc1 = sl(1920, 1984), sl(1984, 2048)
    fd = jnp.pad(sl(2816, 2820), [(0, 0)] * (w.ndim - 1) + [(0, LANES - N_HEADS)])
    out = jnp.concatenate([
        sl(0, 768),
        sl(1536, 1792),
        kc0, kc0, kc1, kc1,
        vc0, vc0, vc1, vc1,
        sl(768, 1536),
        sl(2048, 2816),
        fd], axis=-1)
    assert out.shape[-1] == N_PROJ
    return out


def _t5_bucket_np(dist):
    n = np.maximum(dist, 0)
    max_exact = N_BUCKETS // 2
    nf = np.maximum(n, 1).astype(np.float32)
    large = max_exact + (np.log(nf / np.float32(max_exact)) / np.float32(math.log(T5_MAX_DISTANCE / max_exact))
                         * np.float32(N_BUCKETS - max_exact)).astype(np.int32)
    large = np.minimum(large, N_BUCKETS - 1)
    return np.where(n < max_exact, n, large).astype(np.int32)


def _bucket_thresholds():
    n = np.arange(1 << 16)
    bucket = _t5_bucket_np(n)
    assert (np.diff(bucket) >= 0).all() and bucket[-1] == N_BUCKETS - 1
    thr = np.searchsorted(bucket, np.arange(N_BUCKETS), side="left")
    assert (np.searchsorted(thr, n, side="right") - 1 == bucket).all()
    return tuple(int(t) for t in thr)


def _placement_constants():
    pq = np.zeros((LANES, WIDE), np.float32)
    pk = np.zeros((LANES, WIDE), np.float32)
    cq = np.zeros((1, WIDE), np.float32)
    ck = np.zeros((1, WIDE), np.float32)
    for h in range(N_HEADS):
        for part in range(3):
            pq[part * N_HEADS + h, h * LANES + AUG0 + part] = 1.0
            pk[part * N_HEADS + h, h * LANES + AUG0 + 3 + part] = -1.0
            ck[0, h * LANES + AUG0 + part] = 1.0
            cq[0, h * LANES + AUG0 + 3 + part] = 1.0
    shift = np.zeros((LANES, LANES), np.float32)
    for n in range(AUG0):
        shift[n, AUG0 + n] = 1.0
    return (jnp.asarray(pq, BF16), jnp.asarray(pk, BF16), jnp.asarray(cq), jnp.asarray(ck),
            jnp.asarray(shift, BF16))


def _bias_tiles_kernel(base_ref, table_ref, o_ref, *, rows, cols, step, max_dist, scale, thresholds, chunk):
    t = pl.program_id(0)
    base = base_ref[t]

    def body(c, carry):
        r0 = pl.multiple_of(c * chunk, chunk)
        i = lax.broadcasted_iota(jnp.int32, (chunk, cols), 0) + r0
        j = lax.broadcasted_iota(jnp.int32, (chunk, cols), 1)
        dist = base + i - j
        ok = (dist >= 0) & (dist <= max_dist)
        scaled = dist * step
        vals = [jnp.full((chunk, cols), table_ref[0, h] * scale, F32) for h in range(N_HEADS)]
        for b in range(1, N_BUCKETS):
            reached = scaled >= thresholds[b]
            vals = [jnp.where(reached, table_ref[b, h] * scale, v) for h, v in enumerate(vals)]
        for h in range(N_HEADS):
            o_ref[h, pl.ds(r0, chunk), :] = jnp.where(ok, vals[h], NEG)
        return carry

    lax.fori_loop(0, rows // chunk, body, 0)


def _bias_tiles(table, bases, *, rows, cols, step, max_dist, name, scale=1.0):
    n_tiles = len(bases)
    return pl.pallas_call(
        functools.partial(_bias_tiles_kernel, rows=rows, cols=cols, step=step, max_dist=max_dist, scale=scale,
                          thresholds=_bucket_thresholds(), chunk=32),
        grid=(n_tiles,),
        in_specs=[pl.BlockSpec(memory_space=pltpu.SMEM), pl.BlockSpec(memory_space=pltpu.SMEM)],
        out_specs=pl.BlockSpec((N_HEADS, None, rows, cols), lambda t: (0, t, 0, 0)),
        out_shape=jax.ShapeDtypeStruct((N_HEADS, n_tiles, rows, cols), F32),
        compiler_params=_params(("arbitrary",)),
        name=name,
    )(jnp.asarray(bases, jnp.int32), table)


def _band_bias(table, step, max_dist, name):
    return _bias_tiles(table, (BAND, 0), rows=BAND, cols=2 * BAND, step=step, max_dist=max_dist, name=name)


def _moba_bias(table, n_blocks):
    bases = tuple(MOBA_BLOCK * d for d in range(1 - FLASH_T // MOBA_BLOCK, n_blocks))
    return _bias_tiles(table, bases, rows=MOBA_BLOCK, cols=MOBA_BLOCK, step=1, max_dist=1 << 30,
                       name="moba_bias", scale=LOG2E)


def _widen(a, low):
    outs = []
    for p in range(GROUP_W // LANES):
        pair = a[:, p * LANES:(p + 1) * LANES]
        outs.append(jnp.where(low, pair, 0.0))
        outs.append(jnp.where(low, pltpu.roll(pair, HEAD_DIM, axis=1), 0.0))
    return jnp.concatenate(outs, axis=1)


def _inproj_kernel(x_ref, g_ref, w_ref, fb_ref, pq_ref, pk_ref, cq_ref, ck_ref,
                   y_ref, ya4_ref, ya16_ref, qb_ref, kb_ref, vb_ref, qd_ref, kd_ref, vd_ref,
                   carry_ref, fold_ref, *, tm):
    t = pl.program_id(1)

    @pl.when(t == 0)
    def _():
        carry_ref[...] = jnp.zeros_like(carry_ref)

    h = _rms(x_ref[...], g_ref[...]).astype(BF16)

    def mm(c0, width=GROUP_W):
        return _dot(h, w_ref[:, c0:c0 + width])

    f = mm(FD_OFF, LANES) + fb_ref[...]
    logf = jnp.minimum(f, 0.0) - jnp.log(1.0 + jnp.exp(-jnp.abs(f)))
    srow = lax.broadcasted_iota(jnp.int32, (tm, LANES), 0)
    c = logf
    k = 1
    while k < tm:
        c = c + jnp.where(srow >= k, pltpu.roll(c, k, axis=0), 0.0)
        k *= 2
    c = c + carry_ref[0:1, :]
    carry_ref[0:1, :] = c[tm - 1:tm, :]

    c2 = c * LOG2E
    c_hi = c2.astype(BF16)
    r1 = c2 - c_hi.astype(F32)
    c_mid = r1.astype(BF16)
    c_lo = (r1 - c_mid.astype(F32)).astype(BF16)
    head_lane = lax.broadcasted_iota(jnp.int32, (tm, LANES), 1) < N_HEADS
    pieces = [jnp.where(head_lane, piece.astype(F32), 0.0) for piece in (c_hi, c_mid, c_lo)]
    feats = (pieces[0] + pltpu.roll(pieces[1], N_HEADS, axis=1)
             + pltpu.roll(pieces[2], 2 * N_HEADS, axis=1)).astype(BF16)

    for grp in range(NY // GROUP_W):
        c0 = grp * GROUP_W
        a = mm(c0)
        if c0 in (QA_B * LANES, QC_B * LANES):
            a = a * SCALE
        y_ref[:, c0:c0 + GROUP_W] = a.astype(BF16)
        if c0 < A_COLS:
            fold_ref[2 * grp] = a[:, :LANES]
            fold_ref[2 * grp + 1] = a[:, LANES:]

    for dil, ref in ((4, ya4_ref), (16, ya16_ref)):
        for r in range(dil):
            for slab in range(A_COLS // LANES):
                piece = fold_ref[slab, pl.ds(r, tm // dil, stride=dil), :]
                ref[r, :, slab * LANES:(slab + 1) * LANES] = piece.astype(BF16)

    low = lax.broadcasted_iota(jnp.int32, (tm, LANES), 1) < HEAD_DIM
    lane = lax.broadcasted_iota(jnp.int32, (tm, WIDE), 1) % LANES
    row = lax.broadcasted_iota(jnp.int32, (tm, WIDE), 0)

    qb_ref[...] = _widen(mm(QB_OFF) * (SCALE * LOG2E), low).astype(BF16)
    vb_ref[...] = jnp.where(lane == AUG0, 1.0, _widen(mm(VB_OFF), low)).astype(BF16)
    vd_ref[...] = jnp.where(lane == AUG0, 1.0, _widen(mm(VD_OFF), low)).astype(BF16)
    blk = (t * tm + row) // MOBA_BLOCK
    kb_ref[...] = jnp.where(lane == AUG0 + blk, 1.0, _widen(mm(KB_OFF), low)).astype(BF16)

    qd = _widen(mm(QD_OFF) * (SCALE * LOG2E), low) + _dot(feats, pq_ref[...]) + cq_ref[...]
    kd = _widen(mm(KD_OFF), low) + _dot(feats, pk_ref[...]) + ck_ref[...]
    qd_ref[...] = qd.astype(BF16)
    kd_ref[...] = kd.astype(BF16)


def _inproj(x, g, w, fb, consts, layer):
    B, S, D = x.shape
    tm = PROJ_TM
    pq, pk, cq, ck, _ = consts
    const = lambda b, t: (0, 0)
    of_layer = lambda b, t: (layer, 0, 0)
    tile = lambda b, t: (b, t, 0)
    fold_tile = lambda b, t: (b, 0, t, 0)
    wide_shape = jax.ShapeDtypeStruct((B, S, WIDE), BF16)
    out_shape = (jax.ShapeDtypeStruct((B, S, NY), BF16),
                 jax.ShapeDtypeStruct((B, 4, S // 4, A_COLS), BF16),
                 jax.ShapeDtypeStruct((B, 16, S // 16, A_COLS), BF16),
                 ) + (wide_shape,) * 6
    wide_spec = pl.BlockSpec((None, tm, WIDE), tile)
    return pl.pallas_call(
        functools.partial(_inproj_kernel, tm=tm),
        grid=(B, S // tm),
        in_specs=[pl.BlockSpec((None, tm, D), tile),
                  _resident((None, 1, D), of_layer),
                  _resident((None, D, N_PROJ), of_layer),
                  _resident((None, 1, LANES), of_layer),
                  _resident((LANES, WIDE), const),
                  _resident((LANES, WIDE), const),
                  _resident((1, WIDE), const),
                  _resident((1, WIDE), const)],
        out_specs=(pl.BlockSpec((None, tm, NY), tile),
                   pl.BlockSpec((None, 4, tm // 4, A_COLS), fold_tile),
                   pl.BlockSpec((None, 16, tm // 16, A_COLS), fold_tile),
                   ) + (wide_spec,) * 6,
        out_shape=out_shape,
        scratch_shapes=[pltpu.VMEM((8, LANES), F32),
                        pltpu.VMEM((A_COLS // LANES, tm, LANES), F32)],
        compiler_params=_params(("arbitrary", "arbitrary")),
        name="inproj",
    )(x, g, w, fb, pq, pk, cq, ck)


def _banded_kernel(*refs, dil, n_tiles, with_sink):
    refs = list(refs)
    q_ref, k_ref, v_ref, bias_ref = refs[:4]
    rest = refs[4:]
    sink_ref = rest.pop(0) if with_sink else None
    o_ref, m_ref, l_ref = rest
    pair = pl.program_id(1)
    low = lax.broadcasted_iota(jnp.int32, (BAND, LANES), 1) < HEAD_DIM
    keep = (low.astype(F32).astype(BF16), (1.0 - low.astype(F32)).astype(BF16))

    def body(it, carry):
        r = it // n_tiles
        n = it % n_tiles
        r0 = pl.multiple_of(n * BAND, BAND)
        k0 = pl.multiple_of(jnp.maximum(r0 - BAND, 0), BAND)
        variant = jnp.where(n == 0, 1, 0)
        qt = q_ref[r, pl.ds(r0, BAND), :]
        kt = k_ref[r, pl.ds(k0, 2 * BAND), :]
        vt = v_ref[r, pl.ds(k0, 2 * BAND), :]
        accs, ms, ls = [], [], []
        for hh in range(2):
            s = _dot_nt(qt * keep[hh], kt) + bias_ref[hh, variant]
            m = jnp.max(s, axis=1, keepdims=True)
            if with_sink:
                sink = sink_ref[2 * pair + hh]
                m = jnp.maximum(m, sink)
            p = jnp.exp(s - m)
            l = jnp.sum(p, axis=1, keepdims=True)
            if with_sink:
                l = l + jnp.exp(sink - m)
            accs.append(_dot(p.astype(BF16), vt))
            ms.append(jnp.broadcast_to(m, (BAND, LANES)))
            ls.append(jnp.broadcast_to(l, (BAND, LANES)))
        o_ref[r, pl.ds(r0, BAND), :] = jnp.where(low, accs[0], accs[1])
        m_ref[r, pl.ds(r0, BAND), :] = jnp.where(low, ms[0], ms[1])
        l_ref[r, pl.ds(r0, BAND), :] = jnp.where(low, ls[0], ls[1])
        return carry

    lax.fori_loop(0, dil * n_tiles, body, 0, unroll=BANDED_UNROLL)


def _banded(src, bias, sinks, *, q_blk, k_blk, v_blk, name):
    B, dil, L, _ = src.shape
    n_tiles = L // BAND
    assert n_tiles & (n_tiles - 1) == 0 and (dil * n_tiles) % BANDED_UNROLL == 0
    col = lambda blk: (lambda b, p: (b, 0, 0, blk + p))
    in_specs = [pl.BlockSpec((None, dil, L, LANES), col(q_blk)),
                pl.BlockSpec((None, dil, L, LANES), col(k_blk)),
                pl.BlockSpec((None, dil, L, LANES), col(v_blk)),
                pl.BlockSpec((2, 2, BAND, 2 * BAND), lambda b, p: (p, 0, 0, 0))]
    args = [src, src, src, bias]
    if sinks is not None:
        in_specs.append(pl.BlockSpec(memory_space=pltpu.SMEM))
        args.append(sinks)
    o_spec = pl.BlockSpec((None, dil, L, LANES), lambda b, p: (b, 0, 0, p))
    o_shape = jax.ShapeDtypeStruct((B, dil, L, GROUP_W), F32)
    n_out = 3
    return pl.pallas_call(
        functools.partial(_banded_kernel, dil=dil, n_tiles=n_tiles, with_sink=sinks is not None),
        grid=(B, 2),
        in_specs=in_specs,
        out_specs=(o_spec,) * n_out,
        out_shape=(o_shape,) * n_out,
        compiler_params=_params(("arbitrary",) * 2),
        name=name,
    )(*args)


def _online_update(scores, vts, carry):
    n_heads = len(scores)
    ms = []
    for hh in range(n_heads):
        m_new = carry[2 * hh]
        for s in scores[hh]:
            m_new = jnp.maximum(m_new, jnp.max(s, axis=1, keepdims=True))
        ms.append(m_new)
    new = []
    for hh in range(n_heads):
        m, acc = carry[2 * hh:2 * hh + 2]
        ps = [jnp.exp2(s - ms[hh]).astype(BF16) for s in scores[hh]]
        p = ps[0] if len(ps) == 1 else jnp.concatenate(ps, axis=1)
        acc = jnp.exp2(m - ms[hh]) * acc + _dot(p, vts[hh])
        new.extend((ms[hh], acc))
    return tuple(new)


def _flash_init(tq, n_heads=2):
    return (jnp.full((tq, 1), NEG, F32), jnp.zeros((tq, LANES), F32)) * n_heads


def _finish_pair(o_ref, carry, tq):
    low = lax.broadcasted_iota(jnp.int32, (tq, LANES), 1) < HEAD_DIM
    outs = [acc / acc[:, AUG0:AUG0 + 1] for acc in (carry[1], carry[3])]
    o_ref[...] = jnp.where(low, outs[0], pltpu.roll(outs[1], HEAD_DIM, axis=1))


def _moba_kernel(q_ref, k_ref, v_ref, bias_ref, shift_ref, o_ref, qaug_ref, *, n_blocks):
    i = pl.program_id(2)
    t, blk = FLASH_T, MOBA_BLOCK
    per = t // blk
    seq = n_blocks * blk

    @pl.when(i == 0)
    def _():
        blk_id = lax.broadcasted_iota(jnp.int32, (n_blocks, seq), 0)
        blk_f = blk_id.astype(F32)
        own = lax.broadcasted_iota(jnp.int32, (n_blocks, seq), 1) // blk
        for hh in range(2):
            q = q_ref[:, hh * LANES:(hh + 1) * LANES]
            kf = k_ref[:, hh * LANES:(hh + 1) * LANES].astype(F32)
            km = jnp.mean(kf.reshape(n_blocks, blk, LANES), axis=1)
            k1 = km.astype(BF16)
            r1 = km - k1.astype(F32)
            k2 = r1.astype(BF16)
            k3 = (r1 - k2.astype(F32)).astype(BF16)
            gate = _dot_nt(k1, q) + _dot_nt(k2, q) + _dot_nt(k3, q)

            avail = blk_id < own
            sel = jnp.zeros((n_blocks, seq), jnp.bool_)
            for _ in range(MOBA_TOPK):
                cur = jnp.where(avail, gate, -jnp.inf)
                top = jnp.max(cur, axis=0, keepdims=True)
                is_top = avail & (cur == top)
                first = jnp.min(jnp.where(is_top, blk_f, float(n_blocks)), axis=0, keepdims=True)
                pick = blk_f == first
                sel = sel | pick
                avail = avail & jnp.logical_not(pick)
            penalty = jnp.where(sel | (blk_id == own), 0.0, NEG).astype(BF16)
            placed = lax.dot_general(penalty, shift_ref[0:n_blocks, :], (((0,), (0,)), ((), ())),
                                     preferred_element_type=F32)
            qaug_ref[:, hh * LANES:(hh + 1) * LANES] = (q.astype(F32) + placed).astype(BF16)

    q0 = pl.multiple_of(i * t, t)
    q_aug = [qaug_ref[pl.ds(q0, t), hh * LANES:(hh + 1) * LANES] for hh in range(2)]

    def tile(n, q_rows, k_lo, k_hi, carry_rows):
        r0 = pl.multiple_of(n * t + k_lo * blk, blk)
        width = (k_hi - k_lo) * blk
        d0 = per * (i - n) + per - 1
        scores, vts = [], []
        for hh in range(2):
            kt = k_ref[pl.ds(r0, width), hh * LANES:(hh + 1) * LANES]
            s = _dot_nt(q_aug[hh][q_rows[0] * blk:q_rows[1] * blk], kt)
            rows = [jnp.concatenate([s[(a - q_rows[0]) * blk:(a - q_rows[0] + 1) * blk,
                                       (b - k_lo) * blk:(b - k_lo + 1) * blk] + bias_ref[hh, d0 + a - b]
                                     for b in range(k_lo, k_hi)], axis=1) for a in range(*q_rows)]
            scores.append([jnp.concatenate(rows, axis=0)])
            vts.append(v_ref[pl.ds(r0, width), hh * LANES:(hh + 1) * LANES])
        return _online_update(scores, vts, carry_rows)

    carry = lax.fori_loop(0, i, lambda n, c: tile(n, (0, per), 0, per, c), _flash_init(t))
    hp = per // 2
    carry = tile(i, (0, per), 0, hp, carry)
    low_rows = tile(i, (hp, per), hp, per, tuple(c[hp * blk:] for c in carry))
    carry = tuple(jnp.concatenate([c[:hp * blk], lo], axis=0) for c, lo in zip(carry, low_rows))
    _finish_pair(o_ref, carry, t)


def _moba(qb, kb, vb, bias, shift):
    B, S, _ = qb.shape
    nb = S // MOBA_BLOCK
    t = FLASH_T
    assert nb <= AUG0 and t % (2 * MOBA_BLOCK) == 0 and S % t == 0
    n_bias = nb + t // MOBA_BLOCK - 1
    return pl.pallas_call(
        functools.partial(_moba_kernel, n_blocks=nb),
        grid=(B, 2, S // t),
        in_specs=[pl.BlockSpec((None, S, 2 * LANES), lambda b, p, i: (b, 0, p)),
                  pl.BlockSpec((None, S, 2 * LANES), lambda b, p, i: (b, 0, p)),
                  pl.BlockSpec((None, S, 2 * LANES), lambda b, p, i: (b, 0, p)),
                  pl.BlockSpec((2, n_bias, MOBA_BLOCK, MOBA_BLOCK), lambda b, p, i: (p, 0, 0, 0)),
                  pl.BlockSpec((LANES, LANES), lambda b, p, i: (0, 0))],
        out_specs=pl.BlockSpec((None, t, LANES), lambda b, p, i: (b, i, p)),
        out_shape=jax.ShapeDtypeStruct((B, S, GROUP_W), F32),
        scratch_shapes=[pltpu.VMEM((S, 2 * LANES), BF16)],
        compiler_params=_params(("arbitrary",) * 3),
        name="moba",
    )(qb, kb, vb, bias, shift)


def _fox_kernel(q_ref, k_ref, v_ref, o_ref, *, tq, tk):
    i = pl.program_id(2)
    qs = [q_ref[:, hh * LANES:(hh + 1) * LANES] for hh in range(2)]

    def step(n, carry):
        r0 = pl.multiple_of(n * tk, tk)
        scores, vts = [], []
        for hh in range(2):
            kt = k_ref[pl.ds(r0, tk), hh * LANES:(hh + 1) * LANES]
            scores.append([_dot_nt(qs[hh], kt)])
            vts.append(v_ref[pl.ds(r0, tk), hh * LANES:(hh + 1) * LANES])
        return _online_update(scores, vts, carry)

    carry = lax.fori_loop(0, i, step, _flash_init(tq))
    half = tq // 2
    d0 = pl.multiple_of(i * tq, tq)

    def diag(q_lo, k0, carry_rows):
        rows = tq - q_lo
        row = lax.broadcasted_iota(jnp.int32, (rows, half), 0)
        col = lax.broadcasted_iota(jnp.int32, (rows, half), 1)
        scores, vts = [], []
        for hh in range(2):
            kt = k_ref[pl.ds(k0, half), hh * LANES:(hh + 1) * LANES]
            s = jnp.where(col <= row, _dot_nt(q_ref[q_lo:tq, hh * LANES:(hh + 1) * LANES], kt), NEG)
            scores.append([s])
            vts.append(v_ref[pl.ds(k0, half), hh * LANES:(hh + 1) * LANES])
        return _online_update(scores, vts, carry_rows)

    carry = diag(0, d0, carry)
    low_rows = diag(half, d0 + half, tuple(c[half:] for c in carry))
    carry = tuple(jnp.concatenate([c[:half], lo], axis=0) for c, lo in zip(carry, low_rows))
    _finish_pair(o_ref, carry, tq)


def _fox(qd, kd, vd):
    B, S, _ = qd.shape
    tq, tk = FOX_TQ, FOX_TK
    assert S % tk == 0 and tk == tq
    return pl.pallas_call(
        functools.partial(_fox_kernel, tq=tq, tk=tk),
        grid=(B, 2, S // tq),
        in_specs=[pl.BlockSpec((None, tq, 2 * LANES), lambda b, p, i: (b, i, p)),
                  pl.BlockSpec((None, S, 2 * LANES), lambda b, p, i: (b, 0, p)),
                  pl.BlockSpec((None, S, 2 * LANES), lambda b, p, i: (b, 0, p))],
        out_specs=pl.BlockSpec((None, tq, LANES), lambda b, p, i: (b, i, p)),
        out_shape=jax.ShapeDtypeStruct((B, S, GROUP_W), F32),
        compiler_params=_params(("arbitrary",) * 3),
        name="fox",
    )(qd, kd, vd)


def _outffn_kernel(x_ref, a1_ref, a4_ref, a16_ref, m1_ref, m4_ref, m16_ref, l1_ref, l4_ref, l16_ref,
                   ob_ref, oc_ref, lc_ref, od_ref,
                   mg_ref, wo_ref, g2_ref, wg_ref, wu_ref, wd_ref, fg_ref, out_ref, unfold_ref, *, final, tm):
    def unfold(ref, dil, slot):
        for r in range(dil):
            for s in range(GROUP_W // LANES):
                unfold_ref[slot, s, pl.ds(r, tm // dil, stride=dil), :] = ref[r, :, s * LANES:(s + 1) * LANES]
        return jnp.concatenate([unfold_ref[slot, s] for s in range(GROUP_W // LANES)], axis=1)

    m1, m4, m16 = m1_ref[...], unfold(m4_ref, 4, 0), unfold(m16_ref, 16, 1)
    l1, l4, l16 = l1_ref[...], unfold(l4_ref, 4, 2), unfold(l16_ref, 16, 3)
    a1, a4, a16 = a1_ref[...], unfold(a4_ref, 4, 4), unfold(a16_ref, 16, 5)
    m = jnp.maximum(jnp.maximum(m1, m4), m16)
    e1, e4, e16 = jnp.exp(m1 - m), jnp.exp(m4 - m), jnp.exp(m16 - m)
    mix_a = (e1 * a1 + e4 * a4 + e16 * a16) / (e1 * l1 + e4 * l4 + e16 * l16)
    mix_c = oc_ref[...] / lc_ref[...]

    x1 = x_ref[...]
    for g, part in enumerate((mix_a, ob_ref[...], mix_c, od_ref[...])):
        c0 = g * GROUP_W
        normed = _rms(part, mg_ref[:, c0:c0 + GROUP_W]).astype(BF16)
        x1 = x1 + _dot(normed, wo_ref[c0:c0 + GROUP_W, :])

    h = _rms(x1, g2_ref[...]).astype(BF16)
    ffn = jnp.zeros_like(x1)
    for c in range(D_FF // FF_CHUNK):
        c0 = c * FF_CHUNK
        gate = _dot(h, wg_ref[:, c0:c0 + FF_CHUNK])
        up = _dot(h, wu_ref[:, c0:c0 + FF_CHUNK])
        act = (gate * (1.0 / (1.0 + jnp.exp(-gate))) * up).astype(BF16)
        ffn = ffn + _dot(act, wd_ref[c0:c0 + FF_CHUNK, :])
    x2 = x1 + ffn
    if final:
        x2 = _rms(x2, fg_ref[...])
    out_ref[...] = x2


def _outffn(x, a_parts, ob, c_parts, od, mg, wo, g2, wg, wu, wd, fg, *, layer, final):
    B, S, D = x.shape
    tm = PROJ_TM
    tile = lambda b, t: (b, t, 0)
    const = lambda b, t: (0, 0)
    of_layer = lambda b, t: (layer, 0, 0)
    flat_spec = pl.BlockSpec((None, tm, GROUP_W), tile)

    def fold_spec(dil):
        if dil == 1:
            return pl.BlockSpec((None, None, tm, GROUP_W), lambda b, t: (b, 0, t, 0))
        return pl.BlockSpec((None, dil, tm // dil, GROUP_W), lambda b, t: (b, 0, t, 0))

    dils = [dil for _, dil in DILATED_GROUPS]
    a_specs = [fold_spec(d) for d in dils] * 3
    a_args = [p[k] for k in range(3) for p in a_parts]
    return pl.pallas_call(
        functools.partial(_outffn_kernel, final=final, tm=tm),
        grid=(B, S // tm),
        in_specs=[pl.BlockSpec((None, tm, D), tile)] + a_specs + [flat_spec] + [fold_spec(1)] * 2 + [flat_spec] + [
            _resident((None, 1, D), of_layer),
            _resident((None, D, D), of_layer),
            _resident((None, 1, D), of_layer),
            _resident((None, D, D_FF), of_layer),
            _resident((None, D, D_FF), of_layer),
            _resident((None, D_FF, D), of_layer),
            _resident((1, D), const)],
        out_specs=pl.BlockSpec((None, tm, D), tile),
        out_shape=jax.ShapeDtypeStruct((B, S, D), F32),
        scratch_shapes=[pltpu.VMEM((6, GROUP_W // LANES, tm, LANES), F32)],
        compiler_params=_params(("arbitrary", "arbitrary")),
        name="outffn",
    )(x, *a_args, ob, c_parts[0], c_parts[2], od, mg, wo, g2, wg, wu, wd, fg)


def kernel(x, norm1_g, w_in, f_bias, sinks, mix_norm_g, w_out, norm2_g, w_gate, w_up, w_down, rel_bias, final_g):
    B, S, D = x.shape
    depth = w_in.shape[0]
    max_dil = max(dil for _, dil in DILATED_GROUPS)
    assert D == D_MODEL and S % (2 * BAND * max_dil) == 0 and w_gate.shape[-1] == D_FF
    assert [dil for _, dil in DILATED_GROUPS] == [1, 4, 16] and S % PROJ_TM == 0

    w_proj = _rearrange_w_in(w_in)
    wo_b, wg_b, wu_b, wd_b = (w.astype(BF16) for w in (w_out, w_gate, w_up, w_down))
    fb = jnp.pad(f_bias, ((0, 0), (0, LANES - N_HEADS)))[:, None, :]
    g1, mg, g2 = norm1_g[:, None, :], mix_norm_g[:, None, :], norm2_g[:, None, :]
    consts = _placement_constants()
    table_a = rel_bias[:, :N_HEADS]
    table_b = rel_bias[:, N_HEADS:2 * N_HEADS]
    table_c = rel_bias[:, 2 * N_HEADS:]
    bias_a = [_band_bias(table_a, dil, window // dil, f"dilated{dil}_bias") for window, dil in DILATED_GROUPS]
    bias_c = _band_bias(table_c, 1, SWA_WINDOW - 1, "swa_bias")
    bias_b = _moba_bias(table_b, S // MOBA_BLOCK)

    for l in range(depth):
        y, ya4, ya16, qb, kb, vb, qd, kd, vd = _inproj(x, g1, w_proj, fb, consts, l)
        y4 = y.reshape(B, 1, S, NY)
        a_parts = [_banded(src, bias_a[g], None, q_blk=qb_, k_blk=kb_, v_blk=vb_, name=f"dilated{dil}")
                   for g, ((_, dil), src, (qb_, kb_, vb_)) in enumerate(zip(
                       DILATED_GROUPS, (y4, ya4, ya16), ((QA_B, KA_B, VA_B), (0, 2, 4), (0, 2, 4))))]
        c_parts = _banded(y4, bias_c, sinks[l], q_blk=QC_B, k_blk=KC_B, v_blk=VC_B, name="swa")
        o_b = _moba(qb, kb, vb, bias_b, consts[4])
        o_d = _fox(qd, kd, vd)
        x = _outffn(x, a_parts, o_b, c_parts, o_d, mg, wo_b, g2, wg_b, wu_b, wd_b, final_g[None],
                    layer=l, final=(l == depth - 1))
    return x
```

```python
import functools
import math

import numpy as np
import jax
import jax.numpy as jnp
from jax import lax
from jax.experimental import pallas as pl
from jax.experimental.pallas import tpu as pltpu

F32 = jnp.float32
BF16 = jnp.bfloat16

LANES = 128
VMEM_LIMIT_BYTES = 56 * 1024 * 1024

D_MODEL = 1024
HEAD_DIM = 64
N_HEADS = 4
GROUP_W = N_HEADS * HEAD_DIM
DILATED_GROUPS = ((128, 1), (512, 4), (2048, 16))
BAND = 128
MOBA_BLOCK = 256
MOBA_TOPK = 3
SWA_WINDOW = 128
N_BUCKETS = 32
T5_MAX_DISTANCE = 2048
D_FF = 2816
FF_CHUNK = 256
EPS = 1e-6
NEG = -1e30
SCALE = HEAD_DIM ** -0.5

LOG2E = math.log2(math.e)
QA_B, KA_B, VA_B, QC_B, KC_B, VC_B = 0, 2, 4, 6, 8, 10
NY = 12 * LANES
A_COLS = 3 * GROUP_W
WIDE = N_HEADS * LANES
QB_OFF, KB_OFF, VB_OFF, QD_OFF, KD_OFF, VD_OFF = (NY + g * GROUP_W for g in range(6))
FD_OFF = NY + 6 * GROUP_W
N_PROJ = FD_OFF + LANES
AUG0 = HEAD_DIM

PROJ_TM = 512
FLASH_T = 1024
FOX_TQ, FOX_TK = 1024, 1024
BANDED_UNROLL = 16


def _params(sem):
    return pltpu.CompilerParams(dimension_semantics=sem, vmem_limit_bytes=VMEM_LIMIT_BYTES)


def _resident(shape, index_map):
    return pl.BlockSpec(shape, index_map, pipeline_mode=pl.Buffered(1))


def _dot(a, b):
    return jnp.dot(a, b, preferred_element_type=F32)


def _dot_nt(a, b):
    return lax.dot_general(a, b, (((1,), (1,)), ((), ())), preferred_element_type=F32)


def _rms(x, g):
    return x * lax.rsqrt(jnp.mean(x * x, axis=-1, keepdims=True) + EPS) * g


def _rearrange_w_in(w):
    w = w.astype(BF16)
    sl = lambda a, b: w[..., a:b]
    kc0, kc1 = sl(1792, 1856), sl(1856, 1920)
    vc0, vc1 = sl(1920, 1984), sl(1984, 2048)
    fd = jnp.pad(sl(2816, 2820), [(0, 0)] * (w.ndim - 1) + [(0, LANES - N_HEADS)])
    out = jnp.concatenate([
        sl(0, 768),
        sl(1536, 1792),
        kc0, kc0, kc1, kc1,
        vc0, vc0, vc1, vc1,
        sl(768, 1536),
        sl(2048, 2816),
        fd], axis=-1)
    assert out.shape[-1] == N_PROJ
    return out


def _t5_bucket_np(dist):
    n = np.maximum(dist, 0)
    max_exact = N_BUCKETS // 2
    nf = np.maximum(n, 1).astype(np.float32)
    large = max_exact + (np.log(nf / np.float32(max_exact)) / np.float32(math.log(T5_MAX_DISTANCE / max_exact))
                         * np.float32(N_BUCKETS - max_exact)).astype(np.int32)
    large = np.minimum(large, N_BUCKETS - 1)
    return np.where(n < max_exact, n, large).astype(np.int32)


def _bucket_thresholds():
    n = np.arange(1 << 16)
    bucket = _t5_bucket_np(n)
    assert (np.diff(bucket) >= 0).all() and bucket[-1] == N_BUCKETS - 1
    thr = np.searchsorted(bucket, np.arange(N_BUCKETS), side="left")
    assert (np.searchsorted(thr, n, side="right") - 1 == bucket).all()
    return tuple(int(t) for t in thr)


def _placement_constants():
    pq = np.zeros((LANES, WIDE), np.float32)
    pk = np.zeros((LANES, WIDE), np.float32)
    cq = np.zeros((1, WIDE), np.float32)
    ck = np.zeros((1, WIDE), np.float32)
    for h in range(N_HEADS):
        for part in range(3):
            pq[part * N_HEADS + h, h * LANES + AUG0 + part] = 1.0
            pk[part * N_HEADS + h, h * LANES + AUG0 + 3 + part] = -1.0
            ck[0, h * LANES + AUG0 + part] = 1.0
            cq[0, h * LANES + AUG0 + 3 + part] = 1.0
    shift = np.zeros((LANES, LANES), np.float32)
    for n in range(AUG0):
        shift[n, AUG0 + n] = 1.0
    return (jnp.asarray(pq, BF16), jnp.asarray(pk, BF16), jnp.asarray(cq), jnp.asarray(ck),
            jnp.asarray(shift, BF16))


def _bias_tiles_kernel(base_ref, table_ref, o_ref, *, rows, cols, step, max_dist, scale, thresholds, chunk):
    t = pl.program_id(0)
    base = base_ref[t]

    def body(c, carry):
        r0 = pl.multiple_of(c * chunk, chunk)
        i = lax.broadcasted_iota(jnp.int32, (chunk, cols), 0) + r0
        j = lax.broadcasted_iota(jnp.int32, (chunk, cols), 1)
        dist = base + i - j
        ok = (dist >= 0) & (dist <= max_dist)
        scaled = dist * step
        vals = [jnp.full((chunk, cols), table_ref[0, h] * scale, F32) for h in range(N_HEADS)]
        for b in range(1, N_BUCKETS):
            reached = scaled >= thresholds[b]
            vals = [jnp.where(reached, table_ref[b, h] * scale, v) for h, v in enumerate(vals)]
        for h in range(N_HEADS):
            o_ref[h, pl.ds(r0, chunk), :] = jnp.where(ok, vals[h], NEG)
        return carry

    d_min, d_max = base - (cols - 1), base + rows - 1
    masked = d_max < 0
    last_bucket = (d_min * step >= thresholds[N_BUCKETS - 1]) & (d_max <= max_dist)

    @pl.when(masked | last_bucket)
    def _():
        for h in range(N_HEADS):
            fill = jnp.where(masked, NEG, table_ref[N_BUCKETS - 1, h] * scale)
            o_ref[h] = jnp.full((rows, cols), fill, F32)

    @pl.when(jnp.logical_not(masked | last_bucket))
    def _():
        lax.fori_loop(0, rows // chunk, body, 0)


def _bias_tiles(table, bases, *, rows, cols, step, max_dist, name, scale=1.0):
    n_tiles = len(bases)
    return pl.pallas_call(
        functools.partial(_bias_tiles_kernel, rows=rows, cols=cols, step=step, max_dist=max_dist, scale=scale,
                          thresholds=_bucket_thresholds(), chunk=32),
        grid=(n_tiles,),
        in_specs=[pl.BlockSpec(memory_space=pltpu.SMEM), pl.BlockSpec(memory_space=pltpu.SMEM)],
        out_specs=pl.BlockSpec((N_HEADS, None, rows, cols), lambda t: (0, t, 0, 0)),
        out_shape=jax.ShapeDtypeStruct((N_HEADS, n_tiles, rows, cols), F32),
        compiler_params=_params(("arbitrary",)),
        name=name,
    )(jnp.asarray(bases, jnp.int32), table)


def _band_bias(table, step, max_dist, name):
    return _bias_tiles(table, (BAND, 0), rows=BAND, cols=2 * BAND, step=step, max_dist=max_dist, name=name)


def _moba_bias(table, n_blocks):
    bases = tuple(MOBA_BLOCK * d for d in range(1 - FLASH_T // MOBA_BLOCK, n_blocks))
    return _bias_tiles(table, bases, rows=MOBA_BLOCK, cols=MOBA_BLOCK, step=1, max_dist=1 << 30,
                       name="moba_bias", scale=LOG2E)


def _widen(a, low):
    outs = []
    for p in range(GROUP_W // LANES):
        pair = a[:, p * LANES:(p + 1) * LANES]
        outs.append(jnp.where(low, pair, 0.0))
        outs.append(jnp.where(low, pltpu.roll(pair, HEAD_DIM, axis=1), 0.0))
    return jnp.concatenate(outs, axis=1)


def _inproj_kernel(x_ref, g_ref, w_ref, fb_ref, pq_ref, pk_ref, cq_ref, ck_ref,
                   y_ref, ya4_ref, ya16_ref, qb_ref, kb_ref, vb_ref, qd_ref, kd_ref, vd_ref,
                   carry_ref, fold_ref, *, tm):
    t = pl.program_id(1)

    @pl.when(t == 0)
    def _():
        carry_ref[...] = jnp.zeros_like(carry_ref)

    h = _rms(x_ref[...], g_ref[...]).astype(BF16)

    def mm(c0, width=GROUP_W):
        return _dot(h, w_ref[:, c0:c0 + width])

    f = mm(FD_OFF, LANES) + fb_ref[...]
    logf = jnp.minimum(f, 0.0) - jnp.log(1.0 + jnp.exp(-jnp.abs(f)))
    srow = lax.broadcasted_iota(jnp.int32, (tm, LANES), 0)
    c = logf
    k = 1
    while k < tm:
        c = c + jnp.where(srow >= k, pltpu.roll(c, k, axis=0), 0.0)
        k *= 2
    c = c + carry_ref[0:1, :]
    carry_ref[0:1, :] = c[tm - 1:tm, :]

    c2 = c * LOG2E
    c_hi = c2.astype(BF16)
    r1 = c2 - c_hi.astype(F32)
    c_mid = r1.astype(BF16)
    c_lo = (r1 - c_mid.astype(F32)).astype(BF16)
    head_lane = lax.broadcasted_iota(jnp.int32, (tm, LANES), 1) < N_HEADS
    pieces = [jnp.where(head_lane, piece.astype(F32), 0.0) for piece in (c_hi, c_mid, c_lo)]
    feats = (pieces[0] + pltpu.roll(pieces[1], N_HEADS, axis=1)
             + pltpu.roll(pieces[2], 2 * N_HEADS, axis=1)).astype(BF16)

    for grp in range(NY // GROUP_W):
        c0 = grp * GROUP_W
        a = mm(c0)
        if c0 in (QA_B * LANES, QC_B * LANES):
            a = a * SCALE
        y_ref[:, c0:c0 + GROUP_W] = a.astype(BF16)
        if c0 < A_COLS:
            fold_ref[2 * grp] = a[:, :LANES]
            fold_ref[2 * grp + 1] = a[:, LANES:]

    for dil, ref in ((4, ya4_ref), (16, ya16_ref)):
        for r in range(dil):
            for slab in range(A_COLS // LANES):
                piece = fold_ref[slab, pl.ds(r, tm // dil, stride=dil), :]
                ref[r, :, slab * LANES:(slab + 1) * LANES] = piece.astype(BF16)

    low = lax.broadcasted_iota(jnp.int32, (tm, LANES), 1) < HEAD_DIM
    lane = lax.broadcasted_iota(jnp.int32, (tm, WIDE), 1) % LANES
    row = lax.broadcasted_iota(jnp.int32, (tm, WIDE), 0)

    qb_ref[...] = _widen(mm(QB_OFF) * (SCALE * LOG2E), low).astype(BF16)
    vb_ref[...] = jnp.where(lane == AUG0, 1.0, _widen(mm(VB_OFF), low)).astype(BF16)
    vd_ref[...] = jnp.where(lane == AUG0, 1.0, _widen(mm(VD_OFF), low)).astype(BF16)
    blk = (t * tm + row) // MOBA_BLOCK
    kb_ref[...] = jnp.where(lane == AUG0 + blk, 1.0, _widen(mm(KB_OFF), low)).astype(BF16)

    qd = _widen(mm(QD_OFF) * (SCALE * LOG2E), low) + _dot(feats, pq_ref[...]) + cq_ref[...]
    kd = _widen(mm(KD_OFF), low) + _dot(feats, pk_ref[...]) + ck_ref[...]
    qd_ref[...] = qd.astype(BF16)
    kd_ref[...] = kd.astype(BF16)


def _inproj(x, g, w, fb, consts, layer):
    B, S, D = x.shape
    tm = PROJ_TM
    pq, pk, cq, ck, _ = consts
    const = lambda b, t: (0, 0)
    of_layer = lambda b, t: (layer, 0, 0)
    tile = lambda b, t: (b, t, 0)
    fold_tile = lambda b, t: (b, 0, t, 0)
    wide_shape = jax.ShapeDtypeStruct((B, S, WIDE), BF16)
    out_shape = (jax.ShapeDtypeStruct((B, S, NY), BF16),
                 jax.ShapeDtypeStruct((B, 4, S // 4, A_COLS), BF16),
                 jax.ShapeDtypeStruct((B, 16, S // 16, A_COLS), BF16),
                 ) + (wide_shape,) * 6
    wide_spec = pl.BlockSpec((None, tm, WIDE), tile)
    return pl.pallas_call(
        functools.partial(_inproj_kernel, tm=tm),
        grid=(B, S // tm),
        in_specs=[pl.BlockSpec((None, tm, D), tile),
                  _resident((None, 1, D), of_layer),
                  _resident((None, D, N_PROJ), of_layer),
                  _resident((None, 1, LANES), of_layer),
                  _resident((LANES, WIDE), const),
                  _resident((LANES, WIDE), const),
                  _resident((1, WIDE), const),
                  _resident((1, WIDE), const)],
        out_specs=(pl.BlockSpec((None, tm, NY), tile),
                   pl.BlockSpec((None, 4, tm // 4, A_COLS), fold_tile),
                   pl.BlockSpec((None, 16, tm // 16, A_COLS), fold_tile),
                   ) + (wide_spec,) * 6,
        out_shape=out_shape,
        scratch_shapes=[pltpu.VMEM((8, LANES), F32),
                        pltpu.VMEM((A_COLS // LANES, tm, LANES), F32)],
        compiler_params=_params(("arbitrary", "arbitrary")),
        name="inproj",
    )(x, g, w, fb, pq, pk, cq, ck)


def _banded_kernel(*refs, dil, n_tiles, with_sink):
    refs = list(refs)
    q_ref, k_ref, v_ref, bias_ref = refs[:4]
    rest = refs[4:]
    sink_ref = rest.pop(0) if with_sink else None
    o_ref, m_ref, l_ref = rest
    pair = pl.program_id(1)
    low = lax.broadcasted_iota(jnp.int32, (BAND, LANES), 1) < HEAD_DIM
    keep = (low.astype(F32).astype(BF16), (1.0 - low.astype(F32)).astype(BF16))

    def body(it, carry):
        r = it // n_tiles
        n = it % n_tiles
        r0 = pl.multiple_of(n * BAND, BAND)
        k0 = pl.multiple_of(jnp.maximum(r0 - BAND, 0), BAND)
        variant = jnp.where(n == 0, 1, 0)
        qt = q_ref[r, pl.ds(r0, BAND), :]
        kt = k_ref[r, pl.ds(k0, 2 * BAND), :]
        vt = v_ref[r, pl.ds(k0, 2 * BAND), :]
        accs, ms, ls = [], [], []
        for hh in range(2):
            s = _dot_nt(qt * keep[hh], kt) + bias_ref[hh, variant]
            m = jnp.max(s, axis=1, keepdims=True)
            if with_sink:
                sink = sink_ref[2 * pair + hh]
                m = jnp.maximum(m, sink)
            p = jnp.exp(s - m)
            l = jnp.sum(p, axis=1, keepdims=True)
            if with_sink:
                l = l + jnp.exp(sink - m)
            accs.append(_dot(p.astype(BF16), vt))
            ms.append(jnp.broadcast_to(m, (BAND, LANES)))
            ls.append(jnp.broadcast_to(l, (BAND, LANES)))
        o_ref[r, pl.ds(r0, BAND), :] = jnp.where(low, accs[0], accs[1])
        m_ref[r, pl.ds(r0, BAND), :] = jnp.where(low, ms[0], ms[1])
        l_ref[r, pl.ds(r0, BAND), :] = jnp.where(low, ls[0], ls[1])
        return carry

    lax.fori_loop(0, dil * n_tiles, body, 0, unroll=BANDED_UNROLL)


def _banded(src, bias, sinks, *, q_blk, k_blk, v_blk, name):
    B, dil, L, _ = src.shape
    n_tiles = L // BAND
    assert n_tiles & (n_tiles - 1) == 0 and (dil * n_tiles) % BANDED_UNROLL == 0
    col = lambda blk: (lambda b, p: (b, 0, 0, blk + p))
    in_specs = [pl.BlockSpec((None, dil, L, LANES), col(q_blk)),
                pl.BlockSpec((None, dil, L, LANES), col(k_blk)),
                pl.BlockSpec((None, dil, L, LANES), col(v_blk)),
                pl.BlockSpec((2, 2, BAND, 2 * BAND), lambda b, p: (p, 0, 0, 0))]
    args = [src, src, src, bias]
    if sinks is not None:
        in_specs.append(pl.BlockSpec(memory_space=pltpu.SMEM))
        args.append(sinks)
    o_spec = pl.BlockSpec((None, dil, L, LANES), lambda b, p: (b, 0, 0, p))
    o_shape = jax.ShapeDtypeStruct((B, dil, L, GROUP_W), F32)
    n_out = 3
    return pl.pallas_call(
        functools.partial(_banded_kernel, dil=dil, n_tiles=n_tiles, with_sink=sinks is not None),
        grid=(B, 2),
        in_specs=in_specs,
        out_specs=(o_spec,) * n_out,
        out_shape=(o_shape,) * n_out,
        compiler_params=_params(("arbitrary",) * 2),
        name=name,
    )(*args)


def _online_update(scores, vts, carry):
    n_heads = len(scores)
    ms = []
    for hh in range(n_heads):
        m_new = carry[2 * hh]
        for s in scores[hh]:
            m_new = jnp.maximum(m_new, jnp.max(s, axis=1, keepdims=True))
        ms.append(m_new)
    new = []
    for hh in range(n_heads):
        m, acc = carry[2 * hh:2 * hh + 2]
        ps = [jnp.exp2(s - ms[hh]).astype(BF16) for s in scores[hh]]
        p = ps[0] if len(ps) == 1 else jnp.concatenate(ps, axis=1)
        acc = jnp.exp2(m - ms[hh]) * acc + _dot(p, vts[hh])
        new.extend((ms[hh], acc))
    return tuple(new)


def _flash_init(tq, n_heads=2):
    return (jnp.full((tq, 1), NEG, F32), jnp.zeros((tq, LANES), F32)) * n_heads


def _finish_pair(o_ref, carry, tq):
    low = lax.broadcasted_iota(jnp.int32, (tq, LANES), 1) < HEAD_DIM
    outs = [acc / acc[:, AUG0:AUG0 + 1] for acc in (carry[1], carry[3])]
    o_ref[...] = jnp.where(low, outs[0], pltpu.roll(outs[1], HEAD_DIM, axis=1))


def _moba_kernel(q_ref, k_ref, v_ref, bias_ref, shift_ref, o_ref, qaug_ref, *, n_blocks):
    i = pl.program_id(2)
    t, blk = FLASH_T, MOBA_BLOCK
    per = t // blk
    seq = n_blocks * blk

    @pl.when(i == 0)
    def _():
        blk_id = lax.broadcasted_iota(jnp.int32, (n_blocks, seq), 0)
        blk_f = blk_id.astype(F32)
        own = lax.broadcasted_iota(jnp.int32, (n_blocks, seq), 1) // blk
        for hh in range(2):
            q = q_ref[:, hh * LANES:(hh + 1) * LANES]
            kf = k_ref[:, hh * LANES:(hh + 1) * LANES].astype(F32)
            km = jnp.mean(kf.reshape(n_blocks, blk, LANES), axis=1)
            k1 = km.astype(BF16)
            r1 = km - k1.astype(F32)
            k2 = r1.astype(BF16)
            k3 = (r1 - k2.astype(F32)).astype(BF16)
            gate = _dot_nt(k1, q) + _dot_nt(k2, q) + _dot_nt(k3, q)

            avail = blk_id < own
            sel = jnp.zeros((n_blocks, seq), jnp.bool_)
            for _ in range(MOBA_TOPK):
                cur = jnp.where(avail, gate, -jnp.inf)
                top = jnp.max(cur, axis=0, keepdims=True)
                is_top = avail & (cur == top)
                first = jnp.min(jnp.where(is_top, blk_f, float(n_blocks)), axis=0, keepdims=True)
                pick = blk_f == first
                sel = sel | pick
                avail = avail & jnp.logical_not(pick)
            penalty = jnp.where(sel | (blk_id == own), 0.0, NEG).astype(BF16)
            placed = lax.dot_general(penalty, shift_ref[0:n_blocks, :], (((0,), (0,)), ((), ())),
                                     preferred_element_type=F32)
            qaug_ref[:, hh * LANES:(hh + 1) * LANES] = (q.astype(F32) + placed).astype(BF16)

    q0 = pl.multiple_of(i * t, t)
    q_aug = [qaug_ref[pl.ds(q0, t), hh * LANES:(hh + 1) * LANES] for hh in range(2)]

    def tile(n, q_rows, k_lo, k_hi, carry_rows):
        r0 = pl.multiple_of(n * t + k_lo * blk, blk)
        width = (k_hi - k_lo) * blk
        d0 = per * (i - n) + per - 1
        scores, vts = [], []
        for hh in range(2):
            kt = k_ref[pl.ds(r0, width), hh * LANES:(hh + 1) * LANES]
            s = _dot_nt(q_aug[hh][q_rows[0] * blk:q_rows[1] * blk], kt)
            rows = [jnp.concatenate([s[(a - q_rows[0]) * blk:(a - q_rows[0] + 1) * blk,
                                       (b - k_lo) * blk:(b - k_lo + 1) * blk] + bias_ref[hh, d0 + a - b]
                                     for b in range(k_lo, k_hi)], axis=1) for a in range(*q_rows)]
            scores.append([jnp.concatenate(rows, axis=0)])
            vts.append(v_ref[pl.ds(r0, width), hh * LANES:(hh + 1) * LANES])
        return _online_update(scores, vts, carry_rows)

    carry = lax.fori_loop(0, i, lambda n, c: tile(n, (0, per), 0, per, c), _flash_init(t))
    hp = per // 2
    carry = tile(i, (0, per), 0, hp, carry)
    low_rows = tile(i, (hp, per), hp, per, tuple(c[hp * blk:] for c in carry))
    carry = tuple(jnp.concatenate([c[:hp * blk], lo], axis=0) for c, lo in zip(carry, low_rows))
    _finish_pair(o_ref, carry, t)


def _moba(qb, kb, vb, bias, shift):
    B, S, _ = qb.shape
    nb = S // MOBA_BLOCK
    t = FLASH_T
    assert nb <= AUG0 and t % (2 * MOBA_BLOCK) == 0 and S % t == 0
    n_bias = nb + t // MOBA_BLOCK - 1
    return pl.pallas_call(
        functools.partial(_moba_kernel, n_blocks=nb),
        grid=(B, 2, S // t),
        in_specs=[pl.BlockSpec((None, S, 2 * LANES), lambda b, p, i: (b, 0, p)),
                  pl.BlockSpec((None, S, 2 * LANES), lambda b, p, i: (b, 0, p)),
                  pl.BlockSpec((None, S, 2 * LANES), lambda b, p, i: (b, 0, p)),
                  pl.BlockSpec((2, n_bias, MOBA_BLOCK, MOBA_BLOCK), lambda b, p, i: (p, 0, 0, 0)),
                  pl.BlockSpec((LANES, LANES), lambda b, p, i: (0, 0))],
        out_specs=pl.BlockSpec((None, t, LANES), lambda b, p, i: (b, i, p)),
        out_shape=jax.ShapeDtypeStruct((B, S, GROUP_W), F32),
        scratch_shapes=[pltpu.VMEM((S, 2 * LANES), BF16)],
        compiler_params=_params(("arbitrary",) * 3),
        name="moba",
    )(qb, kb, vb, bias, shift)


def _fox_kernel(q_ref, k_ref, v_ref, o_ref, *, tq, tk):
    i = pl.program_id(2)
    qs = [q_ref[:, hh * LANES:(hh + 1) * LANES] for hh in range(2)]

    def step(n, carry):
        r0 = pl.multiple_of(n * tk, tk)
        scores, vts = [], []
        for hh in range(2):
            kt = k_ref[pl.ds(r0, tk), hh * LANES:(hh + 1) * LANES]
            scores.append([_dot_nt(qs[hh], kt)])
            vts.append(v_ref[pl.ds(r0, tk), hh * LANES:(hh + 1) * LANES])
        return _online_update(scores, vts, carry)

    carry = lax.fori_loop(0, i, step, _flash_init(tq))
    half = tq // 2
    d0 = pl.multiple_of(i * tq, tq)

    def diag(q_lo, k0, carry_rows):
        rows = tq - q_lo
        row = lax.broadcasted_iota(jnp.int32, (rows, half), 0)
        col = lax.broadcasted_iota(jnp.int32, (rows, half), 1)
        scores, vts = [], []
        for hh in range(2):
            kt = k_ref[pl.ds(k0, half), hh * LANES:(hh + 1) * LANES]
            s = jnp.where(col <= row, _dot_nt(q_ref[q_lo:tq, hh * LANES:(hh + 1) * LANES], kt), NEG)
            scores.append([s])
            vts.append(v_ref[pl.ds(k0, half), hh * LANES:(hh + 1) * LANES])
        return _online_update(scores, vts, carry_rows)

    carry = diag(0, d0, carry)
    low_rows = diag(half, d0 + half, tuple(c[half:] for c in carry))
    carry = tuple(jnp.concatenate([c[:half], lo], axis=0) for c, lo in zip(carry, low_rows))
    _finish_pair(o_ref, carry, tq)


def _fox(qd, kd, vd):
    B, S, _ = qd.shape
    tq, tk = FOX_TQ, FOX_TK
    assert S % tk == 0 and tk == tq
    return pl.pallas_call(
        functools.partial(_fox_kernel, tq=tq, tk=tk),
        grid=(B, 2, S // tq),
        in_specs=[pl.BlockSpec((None, tq, 2 * LANES), lambda b, p, i: (b, i, p)),
                  pl.BlockSpec((None, S, 2 * LANES), lambda b, p, i: (b, 0, p)),
                  pl.BlockSpec((None, S, 2 * LANES), lambda b, p, i: (b, 0, p))],
        out_specs=pl.BlockSpec((None, tq, LANES), lambda b, p, i: (b, i, p)),
        out_shape=jax.ShapeDtypeStruct((B, S, GROUP_W), F32),
        compiler_params=_params(("arbitrary",) * 3),
        name="fox",
    )(qd, kd, vd)


def _outffn_kernel(x_ref, a1_ref, a4_ref, a16_ref, m1_ref, m4_ref, m16_ref, l1_ref, l4_ref, l16_ref,
                   ob_ref, oc_ref, lc_ref, od_ref,
                   mg_ref, wo_ref, g2_ref, wg_ref, wu_ref, wd_ref, fg_ref, out_ref, unfold_ref, *, final, tm):
    def unfold(ref, dil, slot):
        for r in range(dil):
            for s in range(GROUP_W // LANES):
                unfold_ref[slot, s, pl.ds(r, tm // dil, stride=dil), :] = ref[r, :, s * LANES:(s + 1) * LANES]
        return jnp.concatenate([unfold_ref[slot, s] for s in range(GROUP_W // LANES)], axis=1)

    m1, m4, m16 = m1_ref[...], unfold(m4_ref, 4, 0), unfold(m16_ref, 16, 1)
    l1, l4, l16 = l1_ref[...], unfold(l4_ref, 4, 2), unfold(l16_ref, 16, 3)
    a1, a4, a16 = a1_ref[...], unfold(a4_ref, 4, 4), unfold(a16_ref, 16, 5)
    m = jnp.maximum(jnp.maximum(m1, m4), m16)
    e1, e4, e16 = jnp.exp(m1 - m), jnp.exp(m4 - m), jnp.exp(m16 - m)
    mix_a = (e1 * a1 + e4 * a4 + e16 * a16) / (e1 * l1 + e4 * l4 + e16 * l16)
    mix_c = oc_ref[...] / lc_ref[...]

    x1 = x_ref[...]
    for g, part in enumerate((mix_a, ob_ref[...], mix_c, od_ref[...])):
        c0 = g * GROUP_W
        normed = _rms(part, mg_ref[:, c0:c0 + GROUP_W]).astype(BF16)
        x1 = x1 + _dot(normed, wo_ref[c0:c0 + GROUP_W, :])

    h = _rms(x1, g2_ref[...]).astype(BF16)
    ffn = jnp.zeros_like(x1)
    for c in range(D_FF // FF_CHUNK):
        c0 = c * FF_CHUNK
        gate = _dot(h, wg_ref[:, c0:c0 + FF_CHUNK])
        up = _dot(h, wu_ref[:, c0:c0 + FF_CHUNK])
        act = (gate * (1.0 / (1.0 + jnp.exp(-gate))) * up).astype(BF16)
        ffn = ffn + _dot(act, wd_ref[c0:c0 + FF_CHUNK, :])
    x2 = x1 + ffn
    if final:
        x2 = _rms(x2, fg_ref[...])
    out_ref[...] = x2


def _outffn(x, a_parts, ob, c_parts, od, mg, wo, g2, wg, wu, wd, fg, *, layer, final):
    B, S, D = x.shape
    tm = PROJ_TM
    tile = lambda b, t: (b, t, 0)
    const = lambda b, t: (0, 0)
    of_layer = lambda b, t: (layer, 0, 0)
    flat_spec = pl.BlockSpec((None, tm, GROUP_W), tile)

    def fold_spec(dil):
        if dil == 1:
            return pl.BlockSpec((None, None, tm, GROUP_W), lambda b, t: (b, 0, t, 0))
        return pl.BlockSpec((None, dil, tm // dil, GROUP_W), lambda b, t: (b, 0, t, 0))

    dils = [dil for _, dil in DILATED_GROUPS]
    a_specs = [fold_spec(d) for d in dils] * 3
    a_args = [p[k] for k in range(3) for p in a_parts]
    return pl.pallas_call(
        functools.partial(_outffn_kernel, final=final, tm=tm),
        grid=(B, S // tm),
        in_specs=[pl.BlockSpec((None, tm, D), tile)] + a_specs + [flat_spec] + [fold_spec(1)] * 2 + [flat_spec] + [
            _resident((None, 1, D), of_layer),
            _resident((None, D, D), of_layer),
            _resident((None, 1, D), of_layer),
            _resident((None, D, D_FF), of_layer),
            _resident((None, D, D_FF), of_layer),
            _resident((None, D_FF, D), of_layer),
            _resident((1, D), const)],
        out_specs=pl.BlockSpec((None, tm, D), tile),
        out_shape=jax.ShapeDtypeStruct((B, S, D), F32),
        scratch_shapes=[pltpu.VMEM((6, GROUP_W // LANES, tm, LANES), F32)],
        compiler_params=_params(("arbitrary", "arbitrary")),
        name="outffn",
    )(x, *a_args, ob, c_parts[0], c_parts[2], od, mg, wo, g2, wg, wu, wd, fg)


def kernel(x, norm1_g, w_in, f_bias, sinks, mix_norm_g, w_out, norm2_g, w_gate, w_up, w_down, rel_bias, final_g):
    B, S, D = x.shape
    depth = w_in.shape[0]
    max_dil = max(dil for _, dil in DILATED_GROUPS)
    assert D == D_MODEL and S % (2 * BAND * max_dil) == 0 and w_gate.shape[-1] == D_FF
    assert [dil for _, dil in DILATED_GROUPS] == [1, 4, 16] and S % PROJ_TM == 0

    w_proj = _rearrange_w_in(w_in)
    wo_b, wg_b, wu_b, wd_b = (w.astype(BF16) for w in (w_out, w_gate, w_up, w_down))
    fb = jnp.pad(f_bias, ((0, 0), (0, LANES - N_HEADS)))[:, None, :]
    g1, mg, g2 = norm1_g[:, None, :], mix_norm_g[:, None, :], norm2_g[:, None, :]
    consts = _placement_constants()
    table_a = rel_bias[:, :N_HEADS]
    table_b = rel_bias[:, N_HEADS:2 * N_HEADS]
    table_c = rel_bias[:, 2 * N_HEADS:]
    bias_a = [_band_bias(table_a, dil, window // dil, f"dilated{dil}_bias") for window, dil in DILATED_GROUPS]
    bias_c = _band_bias(table_c, 1, SWA_WINDOW - 1, "swa_bias")
    bias_b = _moba_bias(table_b, S // MOBA_BLOCK)

    for l in range(depth):
        y, ya4, ya16, qb, kb, vb, qd, kd, vd = _inproj(x, g1, w_proj, fb, consts, l)
        y4 = y.reshape(B, 1, S, NY)
        a_parts = [_banded(src, bias_a[g], None, q_blk=qb_, k_blk=kb_, v_blk=vb_, name=f"dilated{dil}")
                   for g, ((_, dil), src, (qb_, kb_, vb_)) in enumerate(zip(
                       DILATED_GROUPS, (y4, ya4, ya16), ((QA_B, KA_B, VA_B), (0, 2, 4), (0, 2, 4))))]
        c_parts = _banded(y4, bias_c, sinks[l], q_blk=QC_B, k_blk=KC_B, v_blk=VC_B, name="swa")
        o_b = _moba(qb, kb, vb, bias_b, consts[4])
        o_d = _fox(qd, kd, vd)
        x = _outffn(x, a_parts, o_b, c_parts, o_d, mg, wo_b, g2, wg_b, wu_b, wd_b, final_g[None],
                    layer=l, final=(l == depth - 1))
    return x
```

```python
import functools
import math

import numpy as np
import jax
import jax.numpy as jnp
from jax import lax
from jax.experimental import pallas as pl
from jax.experimental.pallas import tpu as pltpu

F32 = jnp.float32
BF16 = jnp.bfloat16

LANES = 128
VMEM_LIMIT_BYTES = 56 * 1024 * 1024

D_MODEL = 1024
HEAD_DIM = 64
N_HEADS = 4
GROUP_W = N_HEADS * HEAD_DIM
KV_HEADS_C = 2
DILATED_GROUPS = ((128, 1), (512, 4), (2048, 16))
BAND = 128
MOBA_BLOCK = 256
MOBA_TOPK = 3
SWA_WINDOW = 128
N_BUCKETS = 32
T5_MAX_DISTANCE = 2048
D_FF = 2816
FF_CHUNK = 256
EPS = 1e-6
NEG = -1e30
SCALE = HEAD_DIM ** -0.5

LOG2E = math.log2(math.e)
QA_B, KA_B, VA_B, QC_B, KC_B, VC_B = 0, 2, 4, 6, 8, 10
NY = 12 * LANES
A_COLS = 3 * GROUP_W
WIDE = N_HEADS * LANES
QB_OFF, KB_OFF, VB_OFF, QD_OFF, KD_OFF, VD_OFF = (NY + g * GROUP_W for g in range(6))
FD_OFF = NY + 6 * GROUP_W
N_PROJ = FD_OFF + LANES
AUG0 = HEAD_DIM

PROJ_TM = 512
FLASH_T = 1024
FOX_TQ, FOX_TK = 1024, 1024
BANDED_UNROLL = 16
BIAS_CHUNK = 32
UNBOUNDED_DISTANCE = 1 << 16


def _params(sem):
    return pltpu.CompilerParams(dimension_semantics=sem, vmem_limit_bytes=VMEM_LIMIT_BYTES)


def _resident(shape, index_map):
    return pl.BlockSpec(shape, index_map, pipeline_mode=pl.Buffered(1))


def _dot(a, b):
    return jnp.dot(a, b, preferred_element_type=F32)


def _dot_nt(a, b):
    return lax.dot_general(a, b, (((1,), (1,)), ((), ())), preferred_element_type=F32)


def _rms(x, g):
    return x * lax.rsqrt(jnp.mean(x * x, axis=-1, keepdims=True) + EPS) * g


def _rearrange_w_in(w):
    kv_w = KV_HEADS_C * HEAD_DIM
    sizes = [GROUP_W] * 6 + [GROUP_W, kv_w, kv_w] + [GROUP_W] * 3 + [N_HEADS]
    starts = np.concatenate([[0], np.cumsum(sizes)])
    assert w.shape[-1] == starts[-1]
    w = w.astype(BF16)
    qa, ka, va, qb, kb, vb, qc, kc, vc, qd, kd, vd, fd = (w[..., a:b] for a, b in zip(starts[:-1], starts[1:]))

    def per_query_head(kv):
        heads = [kv[..., h * HEAD_DIM:(h + 1) * HEAD_DIM] for h in range(KV_HEADS_C)]
        return [h for h in heads for _ in range(N_HEADS // KV_HEADS_C)]

    fd = jnp.pad(fd, [(0, 0)] * (w.ndim - 1) + [(0, LANES - N_HEADS)])
    out = jnp.concatenate([qa, ka, va, qc, *per_query_head(kc), *per_query_head(vc),
                           qb, kb, vb, qd, kd, vd, fd], axis=-1)
    assert out.shape[-1] == N_PROJ
    return out


def _t5_bucket_np(dist):
    n = np.maximum(dist, 0)
    max_exact = N_BUCKETS // 2
    nf = np.maximum(n, 1).astype(np.float32)
    large = max_exact + (np.log(nf / np.float32(max_exact)) / np.float32(math.log(T5_MAX_DISTANCE / max_exact))
                         * np.float32(N_BUCKETS - max_exact)).astype(np.int32)
    large = np.minimum(large, N_BUCKETS - 1)
    return np.where(n < max_exact, n, large).astype(np.int32)


def _bucket_thresholds():
    n = np.arange(UNBOUNDED_DISTANCE)
    bucket = _t5_bucket_np(n)
    assert (np.diff(bucket) >= 0).all() and bucket[-1] == N_BUCKETS - 1
    thr = np.searchsorted(bucket, np.arange(N_BUCKETS), side="left")
    assert (np.searchsorted(thr, n, side="right") - 1 == bucket).all()
    return tuple(int(t) for t in thr)


def _placement_constants():
    pq = np.zeros((LANES, WIDE), np.float32)
    pk = np.zeros((LANES, WIDE), np.float32)
    cq = np.zeros((1, WIDE), np.float32)
    ck = np.zeros((1, WIDE), np.float32)
    for h in range(N_HEADS):
        for part in range(3):
            pq[part * N_HEADS + h, h * LANES + AUG0 + part] = 1.0
            pk[part * N_HEADS + h, h * LANES + AUG0 + 3 + part] = -1.0
            ck[0, h * LANES + AUG0 + part] = 1.0
            cq[0, h * LANES + AUG0 + 3 + part] = 1.0
    shift = np.zeros((LANES, LANES), np.float32)
    for n in range(AUG0):
        shift[n, AUG0 + n] = 1.0
    return (jnp.asarray(pq, BF16), jnp.asarray(pk, BF16), jnp.asarray(cq), jnp.asarray(ck),
            jnp.asarray(shift, BF16))


def _bias_tiles_kernel(base_ref, table_ref, o_ref, *, rows, cols, step, max_dist, scale, thresholds, chunk):
    t = pl.program_id(0)
    base = base_ref[t]

    def body(c, carry):
        r0 = pl.multiple_of(c * chunk, chunk)
        i = lax.broadcasted_iota(jnp.int32, (chunk, cols), 0) + r0
        j = lax.broadcasted_iota(jnp.int32, (chunk, cols), 1)
        dist = base + i - j
        ok = (dist >= 0) & (dist <= max_dist)
        scaled = dist * step
        vals = [jnp.full((chunk, cols), table_ref[0, h] * scale, F32) for h in range(N_HEADS)]
        for b in range(1, N_BUCKETS):
            reached = scaled >= thresholds[b]
            vals = [jnp.where(reached, table_ref[b, h] * scale, v) for h, v in enumerate(vals)]
        for h in range(N_HEADS):
            o_ref[h, pl.ds(r0, chunk), :] = jnp.where(ok, vals[h], NEG)
        return carry

    d_min, d_max = base - (cols - 1), base + rows - 1
    masked = d_max < 0
    last_bucket = (d_min * step >= thresholds[N_BUCKETS - 1]) & (d_max <= max_dist)

    @pl.when(masked | last_bucket)
    def _():
        for h in range(N_HEADS):
            fill = jnp.where(masked, NEG, table_ref[N_BUCKETS - 1, h] * scale)
            o_ref[h] = jnp.full((rows, cols), fill, F32)

    @pl.when(jnp.logical_not(masked | last_bucket))
    def _():
        lax.fori_loop(0, rows // chunk, body, 0)


def _bias_tiles(table, bases, *, rows, cols, step, max_dist, name, scale=1.0):
    n_tiles = len(bases)
    return pl.pallas_call(
        functools.partial(_bias_tiles_kernel, rows=rows, cols=cols, step=step, max_dist=max_dist, scale=scale,
                          thresholds=_bucket_thresholds(), chunk=BIAS_CHUNK),
        grid=(n_tiles,),
        in_specs=[pl.BlockSpec(memory_space=pltpu.SMEM), pl.BlockSpec(memory_space=pltpu.SMEM)],
        out_specs=pl.BlockSpec((N_HEADS, None, rows, cols), lambda t: (0, t, 0, 0)),
        out_shape=jax.ShapeDtypeStruct((N_HEADS, n_tiles, rows, cols), F32),
        compiler_params=_params(("arbitrary",)),
        name=name,
    )(jnp.asarray(bases, jnp.int32), table)


def _band_bias(table, step, max_dist, name):
    return _bias_tiles(table, (BAND, 0), rows=BAND, cols=2 * BAND, step=step, max_dist=max_dist, name=name)


def _moba_bias(table, n_blocks):
    bases = tuple(MOBA_BLOCK * d for d in range(1 - FLASH_T // MOBA_BLOCK, n_blocks))
    return _bias_tiles(table, bases, rows=MOBA_BLOCK, cols=MOBA_BLOCK, step=1, max_dist=UNBOUNDED_DISTANCE,
                       name="moba_bias", scale=LOG2E)


def _widen(a, low):
    outs = []
    for p in range(GROUP_W // LANES):
        pair = a[:, p * LANES:(p + 1) * LANES]
        outs.append(jnp.where(low, pair, 0.0))
        outs.append(jnp.where(low, pltpu.roll(pair, HEAD_DIM, axis=1), 0.0))
    return jnp.concatenate(outs, axis=1)


def _inproj_kernel(x_ref, g_ref, w_ref, fb_ref, pq_ref, pk_ref, cq_ref, ck_ref,
                   y_ref, ya4_ref, ya16_ref, qb_ref, kb_ref, vb_ref, qd_ref, kd_ref, vd_ref,
                   carry_ref, fold_ref, *, tm):
    t = pl.program_id(1)

    @pl.when(t == 0)
    def _():
        carry_ref[...] = jnp.zeros_like(carry_ref)

    h = _rms(x_ref[...], g_ref[...]).astype(BF16)

    def mm(c0, width=GROUP_W):
        return _dot(h, w_ref[:, c0:c0 + width])

    f = mm(FD_OFF, LANES) + fb_ref[...]
    logf = jnp.minimum(f, 0.0) - jnp.log(1.0 + jnp.exp(-jnp.abs(f)))
    srow = lax.broadcasted_iota(jnp.int32, (tm, LANES), 0)
    c = logf
    k = 1
    while k < tm:
        c = c + jnp.where(srow >= k, pltpu.roll(c, k, axis=0), 0.0)
        k *= 2
    c = c + carry_ref[0:1, :]
    carry_ref[0:1, :] = c[tm - 1:tm, :]

    c2 = c * LOG2E
    c_hi = c2.astype(BF16)
    r1 = c2 - c_hi.astype(F32)
    c_mid = r1.astype(BF16)
    c_lo = (r1 - c_mid.astype(F32)).astype(BF16)
    head_lane = lax.broadcasted_iota(jnp.int32, (tm, LANES), 1) < N_HEADS
    pieces = [jnp.where(head_lane, piece.astype(F32), 0.0) for piece in (c_hi, c_mid, c_lo)]
    feats = (pieces[0] + pltpu.roll(pieces[1], N_HEADS, axis=1)
             + pltpu.roll(pieces[2], 2 * N_HEADS, axis=1)).astype(BF16)

    for grp in range(NY // GROUP_W):
        c0 = grp * GROUP_W
        a = mm(c0)
        if c0 in (QA_B * LANES, QC_B * LANES):
            a = a * SCALE
        y_ref[:, c0:c0 + GROUP_W] = a.astype(BF16)
        if c0 < A_COLS:
            fold_ref[2 * grp] = a[:, :LANES]
            fold_ref[2 * grp + 1] = a[:, LANES:]

    for dil, ref in ((4, ya4_ref), (16, ya16_ref)):
        for r in range(dil):
            for slab in range(A_COLS // LANES):
                piece = fold_ref[slab, pl.ds(r, tm // dil, stride=dil), :]
                ref[r, :, slab * LANES:(slab + 1) * LANES] = piece.astype(BF16)

    low = lax.broadcasted_iota(jnp.int32, (tm, LANES), 1) < HEAD_DIM
    lane = lax.broadcasted_iota(jnp.int32, (tm, WIDE), 1) % LANES
    row = lax.broadcasted_iota(jnp.int32, (tm, WIDE), 0)

    qb_ref[...] = _widen(mm(QB_OFF) * (SCALE * LOG2E), low).astype(BF16)
    vb_ref[...] = jnp.where(lane == AUG0, 1.0, _widen(mm(VB_OFF), low)).astype(BF16)
    vd_ref[...] = jnp.where(lane == AUG0, 1.0, _widen(mm(VD_OFF), low)).astype(BF16)
    blk = (t * tm + row) // MOBA_BLOCK
    kb_ref[...] = jnp.where(lane == AUG0 + blk, 1.0, _widen(mm(KB_OFF), low)).astype(BF16)

    qd = _widen(mm(QD_OFF) * (SCALE * LOG2E), low) + _dot(feats, pq_ref[...]) + cq_ref[...]
    kd = _widen(mm(KD_OFF), low) + _dot(feats, pk_ref[...]) + ck_ref[...]
    qd_ref[...] = qd.astype(BF16)
    kd_ref[...] = kd.astype(BF16)


def _inproj(x, g, w, fb, consts, layer):
    B, S, D = x.shape
    tm = PROJ_TM
    pq, pk, cq, ck, _ = consts
    const = lambda b, t: (0, 0)
    of_layer = lambda b, t: (layer, 0, 0)
    tile = lambda b, t: (b, t, 0)
    fold_tile = lambda b, t: (b, 0, t, 0)
    wide_shape = jax.ShapeDtypeStruct((B, S, WIDE), BF16)
    out_shape = (jax.ShapeDtypeStruct((B, S, NY), BF16),
                 jax.ShapeDtypeStruct((B, 4, S // 4, A_COLS), BF16),
                 jax.ShapeDtypeStruct((B, 16, S // 16, A_COLS), BF16),
                 ) + (wide_shape,) * 6
    wide_spec = pl.BlockSpec((None, tm, WIDE), tile)
    return pl.pallas_call(
        functools.partial(_inproj_kernel, tm=tm),
        grid=(B, S // tm),
        in_specs=[pl.BlockSpec((None, tm, D), tile),
                  _resident((None, 1, D), of_layer),
                  _resident((None, D, N_PROJ), of_layer),
                  _resident((None, 1, LANES), of_layer),
                  _resident((LANES, WIDE), const),
                  _resident((LANES, WIDE), const),
                  _resident((1, WIDE), const),
                  _resident((1, WIDE), const)],
        out_specs=(pl.BlockSpec((None, tm, NY), tile),
                   pl.BlockSpec((None, 4, tm // 4, A_COLS), fold_tile),
                   pl.BlockSpec((None, 16, tm // 16, A_COLS), fold_tile),
                   ) + (wide_spec,) * 6,
        out_shape=out_shape,
        scratch_shapes=[pltpu.VMEM((8, LANES), F32),
                        pltpu.VMEM((A_COLS // LANES, tm, LANES), F32)],
        compiler_params=_params(("arbitrary", "arbitrary")),
        name="inproj",
    )(x, g, w, fb, pq, pk, cq, ck)


def _banded_kernel(*refs, dil, n_tiles, with_sink):
    refs = list(refs)
    q_ref, k_ref, v_ref, bias_ref = refs[:4]
    rest = refs[4:]
    sink_ref = rest.pop(0) if with_sink else None
    o_ref, m_ref, l_ref = rest
    pair = pl.program_id(1)
    low = lax.broadcasted_iota(jnp.int32, (BAND, LANES), 1) < HEAD_DIM
    keep = (low.astype(F32).astype(BF16), (1.0 - low.astype(F32)).astype(BF16))

    def body(it, carry):
        r = it // n_tiles
        n = it % n_tiles
        r0 = pl.multiple_of(n * BAND, BAND)
        k0 = pl.multiple_of(jnp.maximum(r0 - BAND, 0), BAND)
        variant = jnp.where(n == 0, 1, 0)
        qt = q_ref[r, pl.ds(r0, BAND), :]
        kt = k_ref[r, pl.ds(k0, 2 * BAND), :]
        vt = v_ref[r, pl.ds(k0, 2 * BAND), :]
        accs, ms, ls = [], [], []
        for hh in range(2):
            s = _dot_nt(qt * keep[hh], kt) + bias_ref[hh, variant]
            m = jnp.max(s, axis=1, keepdims=True)
            if with_sink:
                sink = sink_ref[2 * pair + hh]
                m = jnp.maximum(m, sink)
            p = jnp.exp(s - m)
            l = jnp.sum(p, axis=1, keepdims=True)
            if with_sink:
                l = l + jnp.exp(sink - m)
            accs.append(_dot(p.astype(BF16), vt))
            ms.append(jnp.broadcast_to(m, (BAND, LANES)))
            ls.append(jnp.broadcast_to(l, (BAND, LANES)))
        o_ref[r, pl.ds(r0, BAND), :] = jnp.where(low, accs[0], accs[1])
        m_ref[r, pl.ds(r0, BAND), :] = jnp.where(low, ms[0], ms[1])
        l_ref[r, pl.ds(r0, BAND), :] = jnp.where(low, ls[0], ls[1])
        return carry

    lax.fori_loop(0, dil * n_tiles, body, 0, unroll=BANDED_UNROLL)


def _banded(src, bias, sinks, *, q_blk, k_blk, v_blk, name):
    B, dil, L, _ = src.shape
    n_tiles = L // BAND
    assert n_tiles & (n_tiles - 1) == 0 and (dil * n_tiles) % BANDED_UNROLL == 0
    col = lambda blk: (lambda b, p: (b, 0, 0, blk + p))
    in_specs = [pl.BlockSpec((None, dil, L, LANES), col(q_blk)),
                pl.BlockSpec((None, dil, L, LANES), col(k_blk)),
                pl.BlockSpec((None, dil, L, LANES), col(v_blk)),
                pl.BlockSpec((2, 2, BAND, 2 * BAND), lambda b, p: (p, 0, 0, 0))]
    args = [src, src, src, bias]
    if sinks is not None:
        in_specs.append(pl.BlockSpec(memory_space=pltpu.SMEM))
        args.append(sinks)
    o_spec = pl.BlockSpec((None, dil, L, LANES), lambda b, p: (b, 0, 0, p))
    o_shape = jax.ShapeDtypeStruct((B, dil, L, GROUP_W), F32)
    n_out = 3
    return pl.pallas_call(
        functools.partial(_banded_kernel, dil=dil, n_tiles=n_tiles, with_sink=sinks is not None),
        grid=(B, 2),
        in_specs=in_specs,
        out_specs=(o_spec,) * n_out,
        out_shape=(o_shape,) * n_out,
        compiler_params=_params(("arbitrary",) * 2),
        name=name,
    )(*args)


def _online_update(scores, vts, carry):
    n_heads = len(scores)
    ms = []
    for hh in range(n_heads):
        m_new = carry[2 * hh]
        for s in scores[hh]:
            m_new = jnp.maximum(m_new, jnp.max(s, axis=1, keepdims=True))
        ms.append(m_new)
    new = []
    for hh in range(n_heads):
        m, acc = carry[2 * hh:2 * hh + 2]
        ps = [jnp.exp2(s - ms[hh]).astype(BF16) for s in scores[hh]]
        p = ps[0] if len(ps) == 1 else jnp.concatenate(ps, axis=1)
        acc = jnp.exp2(m - ms[hh]) * acc + _dot(p, vts[hh])
        new.extend((ms[hh], acc))
    return tuple(new)


def _flash_init(tq, n_heads=2):
    return (jnp.full((tq, 1), NEG, F32), jnp.zeros((tq, LANES), F32)) * n_heads


def _finish_pair(o_ref, carry, tq):
    low = lax.broadcasted_iota(jnp.int32, (tq, LANES), 1) < HEAD_DIM
    outs = [acc / acc[:, AUG0:AUG0 + 1] for acc in (carry[1], carry[3])]
    o_ref[...] = jnp.where(low, outs[0], pltpu.roll(outs[1], HEAD_DIM, axis=1))


def _moba_kernel(q_ref, k_ref, v_ref, bias_ref, shift_ref, o_ref, qaug_ref, *, n_blocks):
    i = pl.program_id(2)
    t, blk = FLASH_T, MOBA_BLOCK
    per = t // blk
    seq = n_blocks * blk

    @pl.when(i == 0)
    def _():
        blk_id = lax.broadcasted_iota(jnp.int32, (n_blocks, seq), 0)
        blk_f = blk_id.astype(F32)
        own = lax.broadcasted_iota(jnp.int32, (n_blocks, seq), 1) // blk
        for hh in range(2):
            q = q_ref[:, hh * LANES:(hh + 1) * LANES]
            kf = k_ref[:, hh * LANES:(hh + 1) * LANES].astype(F32)
            km = jnp.mean(kf.reshape(n_blocks, blk, LANES), axis=1)
            k1 = km.astype(BF16)
            r1 = km - k1.astype(F32)
            k2 = r1.astype(BF16)
            k3 = (r1 - k2.astype(F32)).astype(BF16)
            gate = _dot_nt(k1, q) + _dot_nt(k2, q) + _dot_nt(k3, q)

            avail = blk_id < own
            sel = jnp.zeros((n_blocks, seq), jnp.bool_)
            for _ in range(MOBA_TOPK):
                cur = jnp.where(avail, gate, -jnp.inf)
                top = jnp.max(cur, axis=0, keepdims=True)
                is_top = avail & (cur == top)
                first = jnp.min(jnp.where(is_top, blk_f, float(n_blocks)), axis=0, keepdims=True)
                pick = blk_f == first
                sel = sel | pick
                avail = avail & jnp.logical_not(pick)
            penalty = jnp.where(sel | (blk_id == own), 0.0, NEG).astype(BF16)
            placed = lax.dot_general(penalty, shift_ref[0:n_blocks, :], (((0,), (0,)), ((), ())),
                                     preferred_element_type=F32)
            qaug_ref[:, hh * LANES:(hh + 1) * LANES] = (q.astype(F32) + placed).astype(BF16)

    q0 = pl.multiple_of(i * t, t)
    q_aug = [qaug_ref[pl.ds(q0, t), hh * LANES:(hh + 1) * LANES] for hh in range(2)]

    def tile(n, q_rows, k_lo, k_hi, carry_rows):
        r0 = pl.multiple_of(n * t + k_lo * blk, blk)
        width = (k_hi - k_lo) * blk
        d0 = per * (i - n) + per - 1
        scores, vts = [], []
        for hh in range(2):
            kt = k_ref[pl.ds(r0, width), hh * LANES:(hh + 1) * LANES]
            s = _dot_nt(q_aug[hh][q_rows[0] * blk:q_rows[1] * blk], kt)
            rows = [jnp.concatenate([s[(a - q_rows[0]) * blk:(a - q_rows[0] + 1) * blk,
                                       (b - k_lo) * blk:(b - k_lo + 1) * blk] + bias_ref[hh, d0 + a - b]
                                     for b in range(k_lo, k_hi)], axis=1) for a in range(*q_rows)]
            scores.append([jnp.concatenate(rows, axis=0)])
            vts.append(v_ref[pl.ds(r0, width), hh * LANES:(hh + 1) * LANES])
        return _online_update(scores, vts, carry_rows)

    carry = lax.fori_loop(0, i, lambda n, c: tile(n, (0, per), 0, per, c), _flash_init(t))
    hp = per // 2
    carry = tile(i, (0, per), 0, hp, carry)
    low_rows = tile(i, (hp, per), hp, per, tuple(c[hp * blk:] for c in carry))
    carry = tuple(jnp.concatenate([c[:hp * blk], lo], axis=0) for c, lo in zip(carry, low_rows))
    _finish_pair(o_ref, carry, t)


def _moba(qb, kb, vb, bias, shift):
    B, S, _ = qb.shape
    nb = S // MOBA_BLOCK
    t = FLASH_T
    assert nb <= AUG0 and t % (2 * MOBA_BLOCK) == 0 and S % t == 0
    n_bias = nb + t // MOBA_BLOCK - 1
    return pl.pallas_call(
        functools.partial(_moba_kernel, n_blocks=nb),
        grid=(B, 2, S // t),
        in_specs=[pl.BlockSpec((None, S, 2 * LANES), lambda b, p, i: (b, 0, p)),
                  pl.BlockSpec((None, S, 2 * LANES), lambda b, p, i: (b, 0, p)),
                  pl.BlockSpec((None, S, 2 * LANES), lambda b, p, i: (b, 0, p)),
                  pl.BlockSpec((2, n_bias, MOBA_BLOCK, MOBA_BLOCK), lambda b, p, i: (p, 0, 0, 0)),
                  pl.BlockSpec((LANES, LANES), lambda b, p, i: (0, 0))],
        out_specs=pl.BlockSpec((None, t, LANES), lambda b, p, i: (b, i, p)),
        out_shape=jax.ShapeDtypeStruct((B, S, GROUP_W), F32),
        scratch_shapes=[pltpu.VMEM((S, 2 * LANES), BF16)],
        compiler_params=_params(("arbitrary",) * 3),
        name="moba",
    )(qb, kb, vb, bias, shift)


def _fox_kernel(q_ref, k_ref, v_ref, o_ref, *, tq, tk):
    i = pl.program_id(2)
    qs = [q_ref[:, hh * LANES:(hh + 1) * LANES] for hh in range(2)]

    def step(n, carry):
        r0 = pl.multiple_of(n * tk, tk)
        scores, vts = [], []
        for hh in range(2):
            kt = k_ref[pl.ds(r0, tk), hh * LANES:(hh + 1) * LANES]
            scores.append([_dot_nt(qs[hh], kt)])
            vts.append(v_ref[pl.ds(r0, tk), hh * LANES:(hh + 1) * LANES])
        return _online_update(scores, vts, carry)

    carry = lax.fori_loop(0, i, step, _flash_init(tq))
    half = tq // 2
    d0 = pl.multiple_of(i * tq, tq)

    def diag(q_lo, k0, carry_rows):
        rows = tq - q_lo
        row = lax.broadcasted_iota(jnp.int32, (rows, half), 0)
        col = lax.broadcasted_iota(jnp.int32, (rows, half), 1)
        scores, vts = [], []
        for hh in range(2):
            kt = k_ref[pl.ds(k0, half), hh * LANES:(hh + 1) * LANES]
            s = jnp.where(col <= row, _dot_nt(q_ref[q_lo:tq, hh * LANES:(hh + 1) * LANES], kt), NEG)
            scores.append([s])
            vts.append(v_ref[pl.ds(k0, half), hh * LANES:(hh + 1) * LANES])
        return _online_update(scores, vts, carry_rows)

    carry = diag(0, d0, carry)
    low_rows = diag(half, d0 + half, tuple(c[half:] for c in carry))
    carry = tuple(jnp.concatenate([c[:half], lo], axis=0) for c, lo in zip(carry, low_rows))
    _finish_pair(o_ref, carry, tq)


def _fox(qd, kd, vd):
    B, S, _ = qd.shape
    tq, tk = FOX_TQ, FOX_TK
    assert S % tk == 0 and tk == tq
    return pl.pallas_call(
        functools.partial(_fox_kernel, tq=tq, tk=tk),
        grid=(B, 2, S // tq),
        in_specs=[pl.BlockSpec((None, tq, 2 * LANES), lambda b, p, i: (b, i, p)),
                  pl.BlockSpec((None, S, 2 * LANES), lambda b, p, i: (b, 0, p)),
                  pl.BlockSpec((None, S, 2 * LANES), lambda b, p, i: (b, 0, p))],
        out_specs=pl.BlockSpec((None, tq, LANES), lambda b, p, i: (b, i, p)),
        out_shape=jax.ShapeDtypeStruct((B, S, GROUP_W), F32),
        compiler_params=_params(("arbitrary",) * 3),
        name="fox",
    )(qd, kd, vd)


def _outffn_kernel(x_ref, a1_ref, a4_ref, a16_ref, m1_ref, m4_ref, m16_ref, l1_ref, l4_ref, l16_ref,
                   ob_ref, oc_ref, lc_ref, od_ref,
                   mg_ref, wo_ref, g2_ref, wg_ref, wu_ref, wd_ref, fg_ref, out_ref, unfold_ref, *, final, tm):
    def unfold(ref, dil, slot):
        for r in range(dil):
            for s in range(GROUP_W // LANES):
                unfold_ref[slot, s, pl.ds(r, tm // dil, stride=dil), :] = ref[r, :, s * LANES:(s + 1) * LANES]
        return jnp.concatenate([unfold_ref[slot, s] for s in range(GROUP_W // LANES)], axis=1)

    m1, m4, m16 = m1_ref[...], unfold(m4_ref, 4, 0), unfold(m16_ref, 16, 1)
    l1, l4, l16 = l1_ref[...], unfold(l4_ref, 4, 2), unfold(l16_ref, 16, 3)
    a1, a4, a16 = a1_ref[...], unfold(a4_ref, 4, 4), unfold(a16_ref, 16, 5)
    m = jnp.maximum(jnp.maximum(m1, m4), m16)
    e1, e4, e16 = jnp.exp(m1 - m), jnp.exp(m4 - m), jnp.exp(m16 - m)
    mix_a = (e1 * a1 + e4 * a4 + e16 * a16) / (e1 * l1 + e4 * l4 + e16 * l16)
    mix_c = oc_ref[...] / lc_ref[...]

    x1 = x_ref[...]
    for g, part in enumerate((mix_a, ob_ref[...], mix_c, od_ref[...])):
        c0 = g * GROUP_W
        normed = _rms(part, mg_ref[:, c0:c0 + GROUP_W]).astype(BF16)
        x1 = x1 + _dot(normed, wo_ref[c0:c0 + GROUP_W, :])

    h = _rms(x1, g2_ref[...]).astype(BF16)
    ffn = jnp.zeros_like(x1)
    for c in range(D_FF // FF_CHUNK):
        c0 = c * FF_CHUNK
        gate = _dot(h, wg_ref[:, c0:c0 + FF_CHUNK])
        up = _dot(h, wu_ref[:, c0:c0 + FF_CHUNK])
        act = (gate * (1.0 / (1.0 + jnp.exp(-gate))) * up).astype(BF16)
        ffn = ffn + _dot(act, wd_ref[c0:c0 + FF_CHUNK, :])
    x2 = x1 + ffn
    if final:
        x2 = _rms(x2, fg_ref[...])
    out_ref[...] = x2


def _outffn(x, a_parts, ob, c_parts, od, mg, wo, g2, wg, wu, wd, fg, *, layer, final):
    B, S, D = x.shape
    tm = PROJ_TM
    tile = lambda b, t: (b, t, 0)
    const = lambda b, t: (0, 0)
    of_layer = lambda b, t: (layer, 0, 0)
    flat_spec = pl.BlockSpec((None, tm, GROUP_W), tile)

    def fold_spec(dil):
        if dil == 1:
            return pl.BlockSpec((None, None, tm, GROUP_W), lambda b, t: (b, 0, t, 0))
        return pl.BlockSpec((None, dil, tm // dil, GROUP_W), lambda b, t: (b, 0, t, 0))

    dils = [dil for _, dil in DILATED_GROUPS]
    a_specs = [fold_spec(d) for d in dils] * 3
    a_args = [p[k] for k in range(3) for p in a_parts]
    return pl.pallas_call(
        functools.partial(_outffn_kernel, final=final, tm=tm),
        grid=(B, S // tm),
        in_specs=[pl.BlockSpec((None, tm, D), tile)] + a_specs + [flat_spec] + [fold_spec(1)] * 2 + [flat_spec] + [
            _resident((None, 1, D), of_layer),
            _resident((None, D, D), of_layer),
            _resident((None, 1, D), of_layer),
            _resident((None, D, D_FF), of_layer),
            _resident((None, D, D_FF), of_layer),
            _resident((None, D_FF, D), of_layer),
            _resident((1, D), const)],
        out_specs=pl.BlockSpec((None, tm, D), tile),
        out_shape=jax.ShapeDtypeStruct((B, S, D), F32),
        scratch_shapes=[pltpu.VMEM((6, GROUP_W // LANES, tm, LANES), F32)],
        compiler_params=_params(("arbitrary", "arbitrary")),
        name="outffn",
    )(x, *a_args, ob, c_parts[0], c_parts[2], od, mg, wo, g2, wg, wu, wd, fg)


def kernel(x, norm1_g, w_in, f_bias, sinks, mix_norm_g, w_out, norm2_g, w_gate, w_up, w_down, rel_bias, final_g):
    B, S, D = x.shape
    depth = w_in.shape[0]
    max_dil = max(dil for _, dil in DILATED_GROUPS)
    assert D == D_MODEL and S % (2 * BAND * max_dil) == 0 and w_gate.shape[-1] == D_FF
    assert [dil for _, dil in DILATED_GROUPS] == [1, 4, 16] and S % PROJ_TM == 0

    w_proj = _rearrange_w_in(w_in)
    wo_b, wg_b, wu_b, wd_b = (w.astype(BF16) for w in (w_out, w_gate, w_up, w_down))
    fb = jnp.pad(f_bias, ((0, 0), (0, LANES - N_HEADS)))[:, None, :]
    g1, mg, g2 = norm1_g[:, None, :], mix_norm_g[:, None, :], norm2_g[:, None, :]
    consts = _placement_constants()
    table_a = rel_bias[:, :N_HEADS]
    table_b = rel_bias[:, N_HEADS:2 * N_HEADS]
    table_c = rel_bias[:, 2 * N_HEADS:]
    bias_a = [_band_bias(table_a, dil, window // dil, f"dilated{dil}_bias") for window, dil in DILATED_GROUPS]
    bias_c = _band_bias(table_c, 1, SWA_WINDOW - 1, "swa_bias")
    bias_b = _moba_bias(table_b, S // MOBA_BLOCK)

    for l in range(depth):
        y, ya4, ya16, qb, kb, vb, qd, kd, vd = _inproj(x, g1, w_proj, fb, consts, l)
        y4 = y.reshape(B, 1, S, NY)
        a_parts = [_banded(src, bias_a[g], None, q_blk=QA_B, k_blk=KA_B, v_blk=VA_B, name=f"dilated{dil}")
                   for g, ((_, dil), src) in enumerate(zip(DILATED_GROUPS, (y4, ya4, ya16)))]
        c_parts = _banded(y4, bias_c, sinks[l], q_blk=QC_B, k_blk=KC_B, v_blk=VC_B, name="swa")
        o_b = _moba(qb, kb, vb, bias_b, consts[4])
        o_d = _fox(qd, kd, vd)
        x = _outffn(x, a_parts, o_b, c_parts, o_d, mg, wo_b, g2, wg_b, wu_b, wd_b, final_g[None],
                    layer=l, final=(l == depth - 1))
    return x
```

```python
import functools
import math

import numpy as np
import jax
import jax.numpy as jnp
from jax import lax
from jax.experimental import pallas as pl
from jax.experimental.pallas import tpu as pltpu

F32 = jnp.float32
BF16 = jnp.bfloat16

LANES = 128
VMEM_LIMIT_BYTES = 56 * 1024 * 1024

D_MODEL = 1024
HEAD_DIM = 64
N_HEADS = 4
GROUP_W = N_HEADS * HEAD_DIM
KV_HEADS_C = 2
DILATED_GROUPS = ((128, 1), (512, 4), (2048, 16))
BAND = 128
MOBA_BLOCK = 256
MOBA_TOPK = 3
SWA_WINDOW = 128
N_BUCKETS = 32
T5_MAX_DISTANCE = 2048
D_FF = 2816
FF_CHUNK = 256
EPS = 1e-6
NEG = -1e30
SCALE = HEAD_DIM ** -0.5

LOG2E = math.log2(math.e)
QA_B, KA_B, VA_B, QC_B, KC_B, VC_B = 0, 2, 4, 6, 8, 10
NY = 12 * LANES
A_COLS = 3 * GROUP_W
WIDE = N_HEADS * LANES
QB_OFF, KB_OFF, VB_OFF, QD_OFF, KD_OFF, VD_OFF = (NY + g * GROUP_W for g in range(6))
FD_OFF = NY + 6 * GROUP_W
N_PROJ = FD_OFF + LANES
AUG0 = HEAD_DIM

PROJ_TM = 512
FLASH_T = 1024
FOX_TQ, FOX_TK = 1024, 1024
BANDED_UNROLL = 16
MIX_CHUNK = 512
BIAS_CHUNK = 32
UNBOUNDED_DISTANCE = 1 << 16


def _params(sem):
    return pltpu.CompilerParams(dimension_semantics=sem, vmem_limit_bytes=VMEM_LIMIT_BYTES)


def _resident(shape, index_map):
    return pl.BlockSpec(shape, index_map, pipeline_mode=pl.Buffered(1))


def _dot(a, b):
    return jnp.dot(a, b, preferred_element_type=F32)


def _dot_nt(a, b):
    return lax.dot_general(a, b, (((1,), (1,)), ((), ())), preferred_element_type=F32)


def _rms(x, g):
    return x * lax.rsqrt(jnp.mean(x * x, axis=-1, keepdims=True) + EPS) * g


def _rearrange_w_in(w):
    kv_w = KV_HEADS_C * HEAD_DIM
    sizes = [GROUP_W] * 6 + [GROUP_W, kv_w, kv_w] + [GROUP_W] * 3 + [N_HEADS]
    starts = np.concatenate([[0], np.cumsum(sizes)])
    assert w.shape[-1] == starts[-1]
    w = w.astype(BF16)
    qa, ka, va, qb, kb, vb, qc, kc, vc, qd, kd, vd, fd = (w[..., a:b] for a, b in zip(starts[:-1], starts[1:]))

    def per_query_head(kv):
        heads = [kv[..., h * HEAD_DIM:(h + 1) * HEAD_DIM] for h in range(KV_HEADS_C)]
        return [h for h in heads for _ in range(N_HEADS // KV_HEADS_C)]

    fd = jnp.pad(fd, [(0, 0)] * (w.ndim - 1) + [(0, LANES - N_HEADS)])
    out = jnp.concatenate([qa, ka, va, qc, *per_query_head(kc), *per_query_head(vc),
                           qb, kb, vb, qd, kd, vd, fd], axis=-1)
    assert out.shape[-1] == N_PROJ
    return out


def _t5_bucket_np(dist):
    n = np.maximum(dist, 0)
    max_exact = N_BUCKETS // 2
    nf = np.maximum(n, 1).astype(np.float32)
    large = max_exact + (np.log(nf / np.float32(max_exact)) / np.float32(math.log(T5_MAX_DISTANCE / max_exact))
                         * np.float32(N_BUCKETS - max_exact)).astype(np.int32)
    large = np.minimum(large, N_BUCKETS - 1)
    return np.where(n < max_exact, n, large).astype(np.int32)


def _bucket_thresholds():
    n = np.arange(UNBOUNDED_DISTANCE)
    bucket = _t5_bucket_np(n)
    assert (np.diff(bucket) >= 0).all() and bucket[-1] == N_BUCKETS - 1
    thr = np.searchsorted(bucket, np.arange(N_BUCKETS), side="left")
    assert (np.searchsorted(thr, n, side="right") - 1 == bucket).all()
    return tuple(int(t) for t in thr)


def _placement_constants():
    pq = np.zeros((LANES, WIDE), np.float32)
    pk = np.zeros((LANES, WIDE), np.float32)
    cq = np.zeros((1, WIDE), np.float32)
    ck = np.zeros((1, WIDE), np.float32)
    for h in range(N_HEADS):
        for part in range(3):
            pq[part * N_HEADS + h, h * LANES + AUG0 + part] = 1.0
            pk[part * N_HEADS + h, h * LANES + AUG0 + 3 + part] = -1.0
            ck[0, h * LANES + AUG0 + part] = 1.0
            cq[0, h * LANES + AUG0 + 3 + part] = 1.0
    shift = np.zeros((LANES, LANES), np.float32)
    for n in range(AUG0):
        shift[n, AUG0 + n] = 1.0
    return (jnp.asarray(pq, BF16), jnp.asarray(pk, BF16), jnp.asarray(cq), jnp.asarray(ck),
            jnp.asarray(shift, BF16))


def _bias_tiles_kernel(base_ref, table_ref, o_ref, *, rows, cols, step, max_dist, scale, thresholds, chunk):
    t = pl.program_id(0)
    base = base_ref[t]

    def body(c, carry):
        r0 = pl.multiple_of(c * chunk, chunk)
        i = lax.broadcasted_iota(jnp.int32, (chunk, cols), 0) + r0
        j = lax.broadcasted_iota(jnp.int32, (chunk, cols), 1)
        dist = base + i - j
        ok = (dist >= 0) & (dist <= max_dist)
        scaled = dist * step
        vals = [jnp.full((chunk, cols), table_ref[0, h] * scale, F32) for h in range(N_HEADS)]
        for b in range(1, N_BUCKETS):
            reached = scaled >= thresholds[b]
            vals = [jnp.where(reached, table_ref[b, h] * scale, v) for h, v in enumerate(vals)]
        for h in range(N_HEADS):
            o_ref[h, pl.ds(r0, chunk), :] = jnp.where(ok, vals[h], NEG)
        return carry

    d_min, d_max = base - (cols - 1), base + rows - 1
    masked = d_max < 0
    last_bucket = (d_min * step >= thresholds[N_BUCKETS - 1]) & (d_max <= max_dist)

    @pl.when(masked | last_bucket)
    def _():
        for h in range(N_HEADS):
            fill = jnp.where(masked, NEG, table_ref[N_BUCKETS - 1, h] * scale)
            o_ref[h] = jnp.full((rows, cols), fill, F32)

    @pl.when(jnp.logical_not(masked | last_bucket))
    def _():
        lax.fori_loop(0, rows // chunk, body, 0)


def _bias_tiles(table, bases, *, rows, cols, step, max_dist, name, scale=1.0):
    n_tiles = len(bases)
    return pl.pallas_call(
        functools.partial(_bias_tiles_kernel, rows=rows, cols=cols, step=step, max_dist=max_dist, scale=scale,
                          thresholds=_bucket_thresholds(), chunk=BIAS_CHUNK),
        grid=(n_tiles,),
        in_specs=[pl.BlockSpec(memory_space=pltpu.SMEM), pl.BlockSpec(memory_space=pltpu.SMEM)],
        out_specs=pl.BlockSpec((N_HEADS, None, rows, cols), lambda t: (0, t, 0, 0)),
        out_shape=jax.ShapeDtypeStruct((N_HEADS, n_tiles, rows, cols), F32),
        compiler_params=_params(("arbitrary",)),
        name=name,
    )(jnp.asarray(bases, jnp.int32), table)


def _band_bias(table, step, max_dist, name):
    return _bias_tiles(table, (BAND, 0), rows=BAND, cols=2 * BAND, step=step, max_dist=max_dist, name=name)


def _moba_bias(table, n_blocks):
    bases = tuple(MOBA_BLOCK * d for d in range(1 - FLASH_T // MOBA_BLOCK, n_blocks))
    return _bias_tiles(table, bases, rows=MOBA_BLOCK, cols=MOBA_BLOCK, step=1, max_dist=UNBOUNDED_DISTANCE,
                       name="moba_bias", scale=LOG2E)


def _widen(a, low):
    outs = []
    for p in range(GROUP_W // LANES):
        pair = a[:, p * LANES:(p + 1) * LANES]
        outs.append(jnp.where(low, pair, 0.0))
        outs.append(jnp.where(low, pltpu.roll(pair, HEAD_DIM, axis=1), 0.0))
    return jnp.concatenate(outs, axis=1)


def _inproj_kernel(x_ref, g_ref, w_ref, fb_ref, pq_ref, pk_ref, cq_ref, ck_ref,
                   y_ref, ya4_ref, ya16_ref, qb_ref, kb_ref, vb_ref, qd_ref, kd_ref, vd_ref,
                   carry_ref, fold_ref, *, tm):
    t = pl.program_id(1)

    @pl.when(t == 0)
    def _():
        carry_ref[...] = jnp.zeros_like(carry_ref)

    h = _rms(x_ref[...], g_ref[...]).astype(BF16)

    def mm(c0, width=GROUP_W):
        return _dot(h, w_ref[:, c0:c0 + width])

    f = mm(FD_OFF, LANES) + fb_ref[...]
    logf = jnp.minimum(f, 0.0) - jnp.log(1.0 + jnp.exp(-jnp.abs(f)))
    srow = lax.broadcasted_iota(jnp.int32, (tm, LANES), 0)
    c = logf
    k = 1
    while k < tm:
        c = c + jnp.where(srow >= k, pltpu.roll(c, k, axis=0), 0.0)
        k *= 2
    c = c + carry_ref[0:1, :]
    carry_ref[0:1, :] = c[tm - 1:tm, :]

    c2 = c * LOG2E
    c_hi = c2.astype(BF16)
    r1 = c2 - c_hi.astype(F32)
    c_mid = r1.astype(BF16)
    c_lo = (r1 - c_mid.astype(F32)).astype(BF16)
    head_lane = lax.broadcasted_iota(jnp.int32, (tm, LANES), 1) < N_HEADS
    pieces = [jnp.where(head_lane, piece.astype(F32), 0.0) for piece in (c_hi, c_mid, c_lo)]
    feats = (pieces[0] + pltpu.roll(pieces[1], N_HEADS, axis=1)
             + pltpu.roll(pieces[2], 2 * N_HEADS, axis=1)).astype(BF16)

    for grp in range(NY // GROUP_W):
        c0 = grp * GROUP_W
        a = mm(c0)
        if c0 in (QA_B * LANES, QC_B * LANES):
            a = a * SCALE
        y_ref[:, c0:c0 + GROUP_W] = a.astype(BF16)
        if c0 < A_COLS:
            fold_ref[2 * grp] = a[:, :LANES]
            fold_ref[2 * grp + 1] = a[:, LANES:]

    for dil, ref in ((4, ya4_ref), (16, ya16_ref)):
        for r in range(dil):
            for slab in range(A_COLS // LANES):
                piece = fold_ref[slab, pl.ds(r, tm // dil, stride=dil), :]
                ref[r, :, slab * LANES:(slab + 1) * LANES] = piece.astype(BF16)

    low = lax.broadcasted_iota(jnp.int32, (tm, LANES), 1) < HEAD_DIM
    lane = lax.broadcasted_iota(jnp.int32, (tm, WIDE), 1) % LANES
    row = lax.broadcasted_iota(jnp.int32, (tm, WIDE), 0)

    qb_ref[...] = _widen(mm(QB_OFF) * (SCALE * LOG2E), low).astype(BF16)
    vb_ref[...] = jnp.where(lane == AUG0, 1.0, _widen(mm(VB_OFF), low)).astype(BF16)
    vd_ref[...] = jnp.where(lane == AUG0, 1.0, _widen(mm(VD_OFF), low)).astype(BF16)
    blk = (t * tm + row) // MOBA_BLOCK
    kb_ref[...] = jnp.where(lane == AUG0 + blk, 1.0, _widen(mm(KB_OFF), low)).astype(BF16)

    qd = _widen(mm(QD_OFF) * (SCALE * LOG2E), low) + _dot(feats, pq_ref[...]) + cq_ref[...]
    kd = _widen(mm(KD_OFF), low) + _dot(feats, pk_ref[...]) + ck_ref[...]
    qd_ref[...] = qd.astype(BF16)
    kd_ref[...] = kd.astype(BF16)


def _inproj(x, g, w, fb, consts, layer):
    B, S, D = x.shape
    tm = PROJ_TM
    pq, pk, cq, ck, _ = consts
    const = lambda b, t: (0, 0)
    of_layer = lambda b, t: (layer, 0, 0)
    tile = lambda b, t: (b, t, 0)
    fold_tile = lambda b, t: (b, 0, t, 0)
    wide_shape = jax.ShapeDtypeStruct((B, S, WIDE), BF16)
    out_shape = (jax.ShapeDtypeStruct((B, S, NY), BF16),
                 jax.ShapeDtypeStruct((B, 4, S // 4, A_COLS), BF16),
                 jax.ShapeDtypeStruct((B, 16, S // 16, A_COLS), BF16),
                 ) + (wide_shape,) * 6
    wide_spec = pl.BlockSpec((None, tm, WIDE), tile)
    return pl.pallas_call(
        functools.partial(_inproj_kernel, tm=tm),
        grid=(B, S // tm),
        in_specs=[pl.BlockSpec((None, tm, D), tile),
                  _resident((None, 1, D), of_layer),
                  _resident((None, D, N_PROJ), of_layer),
                  _resident((None, 1, LANES), of_layer),
                  _resident((LANES, WIDE), const),
                  _resident((LANES, WIDE), const),
                  _resident((1, WIDE), const),
                  _resident((1, WIDE), const)],
        out_specs=(pl.BlockSpec((None, tm, NY), tile),
                   pl.BlockSpec((None, 4, tm // 4, A_COLS), fold_tile),
                   pl.BlockSpec((None, 16, tm // 16, A_COLS), fold_tile),
                   ) + (wide_spec,) * 6,
        out_shape=out_shape,
        scratch_shapes=[pltpu.VMEM((8, LANES), F32),
                        pltpu.VMEM((A_COLS // LANES, tm, LANES), F32)],
        compiler_params=_params(("arbitrary", "arbitrary")),
        name="inproj",
    )(x, g, w, fb, pq, pk, cq, ck)


def _banded_pass(q_ref, k_ref, v_ref, bias_ref, sinks, o_ref, m_ref, l_ref, *, dil, n_tiles):
    low = lax.broadcasted_iota(jnp.int32, (BAND, LANES), 1) < HEAD_DIM
    keep = (low.astype(F32).astype(BF16), (1.0 - low.astype(F32)).astype(BF16))

    def body(it, carry):
        r = it // n_tiles
        n = it % n_tiles
        r0 = pl.multiple_of(n * BAND, BAND)
        k0 = pl.multiple_of(jnp.maximum(r0 - BAND, 0), BAND)
        variant = jnp.where(n == 0, 1, 0)
        qt = q_ref[r, pl.ds(r0, BAND), :]
        kt = k_ref[r, pl.ds(k0, 2 * BAND), :]
        vt = v_ref[r, pl.ds(k0, 2 * BAND), :]
        accs, ms, ls = [], [], []
        for hh in range(2):
            s = _dot_nt(qt * keep[hh], kt) + bias_ref[hh, variant]
            m = jnp.max(s, axis=1, keepdims=True)
            if sinks is not None:
                m = jnp.maximum(m, sinks[hh])
            p = jnp.exp(s - m)
            l = jnp.sum(p, axis=1, keepdims=True)
            if sinks is not None:
                l = l + jnp.exp(sinks[hh] - m)
            accs.append(_dot(p.astype(BF16), vt))
            ms.append(jnp.broadcast_to(m, (BAND, LANES)))
            ls.append(jnp.broadcast_to(l, (BAND, LANES)))
        o_ref[r, pl.ds(r0, BAND), :] = jnp.where(low, accs[0], accs[1])
        m_ref[r, pl.ds(r0, BAND), :] = jnp.where(low, ms[0], ms[1])
        l_ref[r, pl.ds(r0, BAND), :] = jnp.where(low, ls[0], ls[1])
        return carry

    lax.fori_loop(0, dil * n_tiles, body, 0, unroll=BANDED_UNROLL)


def _banded_kernel(q_ref, k_ref, v_ref, bias_ref, sink_ref, o_ref, m_ref, l_ref, *, dil, n_tiles):
    pair = pl.program_id(1)
    sinks = (sink_ref[2 * pair], sink_ref[2 * pair + 1])
    _banded_pass(q_ref, k_ref, v_ref, bias_ref, sinks, o_ref, m_ref, l_ref, dil=dil, n_tiles=n_tiles)


def _dilated_mix_kernel(*refs, dils, seq):
    n = len(dils)
    qkv = [refs[3 * g:3 * g + 3] for g in range(n)]
    biases = refs[3 * n:4 * n]
    o_ref = refs[4 * n]
    results = refs[4 * n + 1:5 * n + 1]
    unfold_ref = refs[5 * n + 1]
    for g, dil in enumerate(dils):
        res = results[g]
        _banded_pass(*qkv[g], biases[g], None, res.at[0], res.at[1], res.at[2], dil=dil, n_tiles=seq // dil // BAND)

    chunk = MIX_CHUNK

    def combine(c, carry):
        parts = []
        for g, dil in enumerate(dils):
            rows = chunk // dil
            l0 = pl.multiple_of(c * rows, rows)
            if dil == 1:
                parts.append([results[g][k, 0, pl.ds(l0, rows), :] for k in range(3)])
                continue
            for k in range(3):
                for r in range(dil):
                    unfold_ref[g, k, pl.ds(r, rows, stride=dil), :] = results[g][k, r, pl.ds(l0, rows), :]
            parts.append([unfold_ref[g, k] for k in range(3)])
        m = functools.reduce(jnp.maximum, [p[1] for p in parts])
        es = [jnp.exp(p[1] - m) for p in parts]
        num = sum(e * p[0] for e, p in zip(es, parts))
        den = sum(e * p[2] for e, p in zip(es, parts))
        o_ref[pl.ds(pl.multiple_of(c * chunk, chunk), chunk), :] = num / den
        return carry

    lax.fori_loop(0, seq // chunk, combine, 0)


def _dilated_mix(srcs, biases):
    B = srcs[0].shape[0]
    dils = tuple(src.shape[1] for src in srcs)
    S = srcs[0].shape[1] * srcs[0].shape[2]
    col = lambda blk: (lambda b, p: (b, 0, 0, blk + p))
    in_specs, args = [], []
    for src, dil in zip(srcs, dils):
        for blk in (QA_B, KA_B, VA_B):
            in_specs.append(pl.BlockSpec((None, dil, S // dil, LANES), col(blk)))
            args.append(src)
    for bias in biases:
        in_specs.append(pl.BlockSpec((2, 2, BAND, 2 * BAND), lambda b, p: (p, 0, 0, 0)))
        args.append(bias)
    scratch = [pltpu.VMEM((3, dil, S // dil, LANES), F32) for dil in dils]
    scratch.append(pltpu.VMEM((len(dils), 3, MIX_CHUNK, LANES), F32))
    return pl.pallas_call(
        functools.partial(_dilated_mix_kernel, dils=dils, seq=S),
        grid=(B, 2),
        in_specs=in_specs,
        out_specs=pl.BlockSpec((None, S, LANES), lambda b, p: (b, 0, p)),
        out_shape=jax.ShapeDtypeStruct((B, S, GROUP_W), F32),
        scratch_shapes=scratch,
        compiler_params=_params(("arbitrary",) * 2),
        name="dilated_mix",
    )(*args)


def _swa(src, bias, sinks):
    B, dil, L, _ = src.shape
    n_tiles = L // BAND
    assert n_tiles & (n_tiles - 1) == 0 and (dil * n_tiles) % BANDED_UNROLL == 0
    col = lambda blk: (lambda b, p: (b, 0, 0, blk + p))
    o_spec = pl.BlockSpec((None, dil, L, LANES), lambda b, p: (b, 0, 0, p))
    o_shape = jax.ShapeDtypeStruct((B, dil, L, GROUP_W), F32)
    return pl.pallas_call(
        functools.partial(_banded_kernel, dil=dil, n_tiles=n_tiles),
        grid=(B, 2),
        in_specs=[pl.BlockSpec((None, dil, L, LANES), col(QC_B)),
                  pl.BlockSpec((None, dil, L, LANES), col(KC_B)),
                  pl.BlockSpec((None, dil, L, LANES), col(VC_B)),
                  pl.BlockSpec((2, 2, BAND, 2 * BAND), lambda b, p: (p, 0, 0, 0)),
                  pl.BlockSpec(memory_space=pltpu.SMEM)],
        out_specs=(o_spec,) * 3,
        out_shape=(o_shape,) * 3,
        compiler_params=_params(("arbitrary",) * 2),
        name="swa",
    )(src, src, src, bias, sinks)


def _online_update(scores, vts, carry):
    n_heads = len(scores)
    ms = []
    for hh in range(n_heads):
        m_new = carry[2 * hh]
        for s in scores[hh]:
            m_new = jnp.maximum(m_new, jnp.max(s, axis=1, keepdims=True))
        ms.append(m_new)
    new = []
    for hh in range(n_heads):
        m, acc = carry[2 * hh:2 * hh + 2]
        ps = [jnp.exp2(s - ms[hh]).astype(BF16) for s in scores[hh]]
        p = ps[0] if len(ps) == 1 else jnp.concatenate(ps, axis=1)
        acc = jnp.exp2(m - ms[hh]) * acc + _dot(p, vts[hh])
        new.extend((ms[hh], acc))
    return tuple(new)


def _flash_init(tq, n_heads=2):
    return (jnp.full((tq, 1), NEG, F32), jnp.zeros((tq, LANES), F32)) * n_heads


def _finish_pair(o_ref, carry, tq):
    low = lax.broadcasted_iota(jnp.int32, (tq, LANES), 1) < HEAD_DIM
    outs = [acc / acc[:, AUG0:AUG0 + 1] for acc in (carry[1], carry[3])]
    o_ref[...] = jnp.where(low, outs[0], pltpu.roll(outs[1], HEAD_DIM, axis=1))


def _moba_kernel(q_ref, k_ref, v_ref, bias_ref, shift_ref, o_ref, qaug_ref, *, n_blocks):
    i = pl.program_id(2)
    t, blk = FLASH_T, MOBA_BLOCK
    per = t // blk
    seq = n_blocks * blk

    @pl.when(i == 0)
    def _():
        blk_id = lax.broadcasted_iota(jnp.int32, (n_blocks, seq), 0)
        blk_f = blk_id.astype(F32)
        own = lax.broadcasted_iota(jnp.int32, (n_blocks, seq), 1) // blk
        for hh in range(2):
            q = q_ref[:, hh * LANES:(hh + 1) * LANES]
            kf = k_ref[:, hh * LANES:(hh + 1) * LANES].astype(F32)
            km = jnp.mean(kf.reshape(n_blocks, blk, LANES), axis=1)
            k1 = km.astype(BF16)
            r1 = km - k1.astype(F32)
            k2 = r1.astype(BF16)
            k3 = (r1 - k2.astype(F32)).astype(BF16)
            gate = _dot_nt(k1, q) + _dot_nt(k2, q) + _dot_nt(k3, q)

            avail = blk_id < own
            sel = jnp.zeros((n_blocks, seq), jnp.bool_)
            for _ in range(MOBA_TOPK):
                cur = jnp.where(avail, gate, -jnp.inf)
                top = jnp.max(cur, axis=0, keepdims=True)
                is_top = avail & (cur == top)
                first = jnp.min(jnp.where(is_top, blk_f, float(n_blocks)), axis=0, keepdims=True)
                pick = blk_f == first
                sel = sel | pick
                avail = avail & jnp.logical_not(pick)
            penalty = jnp.where(sel | (blk_id == own), 0.0, NEG).astype(BF16)
            placed = lax.dot_general(penalty, shift_ref[0:n_blocks, :], (((0,), (0,)), ((), ())),
                                     preferred_element_type=F32)
            qaug_ref[:, hh * LANES:(hh + 1) * LANES] = (q.astype(F32) + placed).astype(BF16)

    q0 = pl.multiple_of(i * t, t)
    q_aug = [qaug_ref[pl.ds(q0, t), hh * LANES:(hh + 1) * LANES] for hh in range(2)]

    def tile(n, q_rows, k_lo, k_hi, carry_rows):
        r0 = pl.multiple_of(n * t + k_lo * blk, blk)
        width = (k_hi - k_lo) * blk
        d0 = per * (i - n) + per - 1
        scores, vts = [], []
        for hh in range(2):
            kt = k_ref[pl.ds(r0, width), hh * LANES:(hh + 1) * LANES]
            s = _dot_nt(q_aug[hh][q_rows[0] * blk:q_rows[1] * blk], kt)
            rows = [jnp.concatenate([s[(a - q_rows[0]) * blk:(a - q_rows[0] + 1) * blk,
                                       (b - k_lo) * blk:(b - k_lo + 1) * blk] + bias_ref[hh, d0 + a - b]
                                     for b in range(k_lo, k_hi)], axis=1) for a in range(*q_rows)]
            scores.append([jnp.concatenate(rows, axis=0)])
            vts.append(v_ref[pl.ds(r0, width), hh * LANES:(hh + 1) * LANES])
        return _online_update(scores, vts, carry_rows)

    carry = lax.fori_loop(0, i, lambda n, c: tile(n, (0, per), 0, per, c), _flash_init(t))
    hp = per // 2
    carry = tile(i, (0, per), 0, hp, carry)
    low_rows = tile(i, (hp, per), hp, per, tuple(c[hp * blk:] for c in carry))
    carry = tuple(jnp.concatenate([c[:hp * blk], lo], axis=0) for c, lo in zip(carry, low_rows))
    _finish_pair(o_ref, carry, t)


def _moba(qb, kb, vb, bias, shift):
    B, S, _ = qb.shape
    nb = S // MOBA_BLOCK
    t = FLASH_T
    assert nb <= AUG0 and t % (2 * MOBA_BLOCK) == 0 and S % t == 0
    n_bias = nb + t // MOBA_BLOCK - 1
    return pl.pallas_call(
        functools.partial(_moba_kernel, n_blocks=nb),
        grid=(B, 2, S // t),
        in_specs=[pl.BlockSpec((None, S, 2 * LANES), lambda b, p, i: (b, 0, p)),
                  pl.BlockSpec((None, S, 2 * LANES), lambda b, p, i: (b, 0, p)),
                  pl.BlockSpec((None, S, 2 * LANES), lambda b, p, i: (b, 0, p)),
                  pl.BlockSpec((2, n_bias, MOBA_BLOCK, MOBA_BLOCK), lambda b, p, i: (p, 0, 0, 0)),
                  pl.BlockSpec((LANES, LANES), lambda b, p, i: (0, 0))],
        out_specs=pl.BlockSpec((None, t, LANES), lambda b, p, i: (b, i, p)),
        out_shape=jax.ShapeDtypeStruct((B, S, GROUP_W), F32),
        scratch_shapes=[pltpu.VMEM((S, 2 * LANES), BF16)],
        compiler_params=_params(("arbitrary",) * 3),
        name="moba",
    )(qb, kb, vb, bias, shift)


def _fox_kernel(q_ref, k_ref, v_ref, o_ref, *, tq, tk):
    i = pl.program_id(2)
    qs = [q_ref[:, hh * LANES:(hh + 1) * LANES] for hh in range(2)]

    def step(n, carry):
        r0 = pl.multiple_of(n * tk, tk)
        scores, vts = [], []
        for hh in range(2):
            kt = k_ref[pl.ds(r0, tk), hh * LANES:(hh + 1) * LANES]
            scores.append([_dot_nt(qs[hh], kt)])
            vts.append(v_ref[pl.ds(r0, tk), hh * LANES:(hh + 1) * LANES])
        return _online_update(scores, vts, carry)

    carry = lax.fori_loop(0, i, step, _flash_init(tq))
    half = tq // 2
    d0 = pl.multiple_of(i * tq, tq)

    def diag(q_lo, k0, carry_rows):
        rows = tq - q_lo
        row = lax.broadcasted_iota(jnp.int32, (rows, half), 0)
        col = lax.broadcasted_iota(jnp.int32, (rows, half), 1)
        scores, vts = [], []
        for hh in range(2):
            kt = k_ref[pl.ds(k0, half), hh * LANES:(hh + 1) * LANES]
            s = jnp.where(col <= row, _dot_nt(q_ref[q_lo:tq, hh * LANES:(hh + 1) * LANES], kt), NEG)
            scores.append([s])
            vts.append(v_ref[pl.ds(k0, half), hh * LANES:(hh + 1) * LANES])
        return _online_update(scores, vts, carry_rows)

    carry = diag(0, d0, carry)
    low_rows = diag(half, d0 + half, tuple(c[half:] for c in carry))
    carry = tuple(jnp.concatenate([c[:half], lo], axis=0) for c, lo in zip(carry, low_rows))
    _finish_pair(o_ref, carry, tq)


def _fox(qd, kd, vd):
    B, S, _ = qd.shape
    tq, tk = FOX_TQ, FOX_TK
    assert S % tk == 0 and tk == tq
    return pl.pallas_call(
        functools.partial(_fox_kernel, tq=tq, tk=tk),
        grid=(B, 2, S // tq),
        in_specs=[pl.BlockSpec((None, tq, 2 * LANES), lambda b, p, i: (b, i, p)),
                  pl.BlockSpec((None, S, 2 * LANES), lambda b, p, i: (b, 0, p)),
                  pl.BlockSpec((None, S, 2 * LANES), lambda b, p, i: (b, 0, p))],
        out_specs=pl.BlockSpec((None, tq, LANES), lambda b, p, i: (b, i, p)),
        out_shape=jax.ShapeDtypeStruct((B, S, GROUP_W), F32),
        compiler_params=_params(("arbitrary",) * 3),
        name="fox",
    )(qd, kd, vd)


def _outffn_kernel(x_ref, oa_ref, ob_ref, oc_ref, lc_ref, od_ref,
                   mg_ref, wo_ref, g2_ref, wg_ref, wu_ref, wd_ref, fg_ref, out_ref, *, final):
    mix_a = oa_ref[...]
    mix_c = oc_ref[...] / lc_ref[...]

    x1 = x_ref[...]
    for g, part in enumerate((mix_a, ob_ref[...], mix_c, od_ref[...])):
        c0 = g * GROUP_W
        normed = _rms(part, mg_ref[:, c0:c0 + GROUP_W]).astype(BF16)
        x1 = x1 + _dot(normed, wo_ref[c0:c0 + GROUP_W, :])

    h = _rms(x1, g2_ref[...]).astype(BF16)
    ffn = jnp.zeros_like(x1)
    for c in range(D_FF // FF_CHUNK):
        c0 = c * FF_CHUNK
        gate = _dot(h, wg_ref[:, c0:c0 + FF_CHUNK])
        up = _dot(h, wu_ref[:, c0:c0 + FF_CHUNK])
        act = (gate * (1.0 / (1.0 + jnp.exp(-gate))) * up).astype(BF16)
        ffn = ffn + _dot(act, wd_ref[c0:c0 + FF_CHUNK, :])
    x2 = x1 + ffn
    if final:
        x2 = _rms(x2, fg_ref[...])
    out_ref[...] = x2


def _outffn(x, oa, ob, c_parts, od, mg, wo, g2, wg, wu, wd, fg, *, layer, final):
    B, S, D = x.shape
    tm = PROJ_TM
    tile = lambda b, t: (b, t, 0)
    const = lambda b, t: (0, 0)
    of_layer = lambda b, t: (layer, 0, 0)
    flat_spec = pl.BlockSpec((None, tm, GROUP_W), tile)
    c_spec = pl.BlockSpec((None, None, tm, GROUP_W), lambda b, t: (b, 0, t, 0))
    return pl.pallas_call(
        functools.partial(_outffn_kernel, final=final),
        grid=(B, S // tm),
        in_specs=[pl.BlockSpec((None, tm, D), tile), flat_spec, flat_spec, c_spec, c_spec, flat_spec,
                  _resident((None, 1, D), of_layer),
                  _resident((None, D, D), of_layer),
                  _resident((None, 1, D), of_layer),
                  _resident((None, D, D_FF), of_layer),
                  _resident((None, D, D_FF), of_layer),
                  _resident((None, D_FF, D), of_layer),
                  _resident((1, D), const)],
        out_specs=pl.BlockSpec((None, tm, D), tile),
        out_shape=jax.ShapeDtypeStruct((B, S, D), F32),
        compiler_params=_params(("arbitrary", "arbitrary")),
        name="outffn",
    )(x, oa, ob, c_parts[0], c_parts[2], od, mg, wo, g2, wg, wu, wd, fg)


def kernel(x, norm1_g, w_in, f_bias, sinks, mix_norm_g, w_out, norm2_g, w_gate, w_up, w_down, rel_bias, final_g):
    B, S, D = x.shape
    depth = w_in.shape[0]
    max_dil = max(dil for _, dil in DILATED_GROUPS)
    assert D == D_MODEL and S % (2 * BAND * max_dil) == 0 and w_gate.shape[-1] == D_FF
    assert [dil for _, dil in DILATED_GROUPS] == [1, 4, 16] and S % PROJ_TM == 0

    w_proj = _rearrange_w_in(w_in)
    wo_b, wg_b, wu_b, wd_b = (w.astype(BF16) for w in (w_out, w_gate, w_up, w_down))
    fb = jnp.pad(f_bias, ((0, 0), (0, LANES - N_HEADS)))[:, None, :]
    g1, mg, g2 = norm1_g[:, None, :], mix_norm_g[:, None, :], norm2_g[:, None, :]
    consts = _placement_constants()
    table_a = rel_bias[:, :N_HEADS]
    table_b = rel_bias[:, N_HEADS:2 * N_HEADS]
    table_c = rel_bias[:, 2 * N_HEADS:]
    bias_a = [_band_bias(table_a, dil, window // dil, f"dilated{dil}_bias") for window, dil in DILATED_GROUPS]
    bias_c = _band_bias(table_c, 1, SWA_WINDOW - 1, "swa_bias")
    bias_b = _moba_bias(table_b, S // MOBA_BLOCK)

    for l in range(depth):
        y, ya4, ya16, qb, kb, vb, qd, kd, vd = _inproj(x, g1, w_proj, fb, consts, l)
        y4 = y.reshape(B, 1, S, NY)
        o_a = _dilated_mix((y4, ya4, ya16), bias_a)
        c_parts = _swa(y4, bias_c, sinks[l])
        o_b = _moba(qb, kb, vb, bias_b, consts[4])
        o_d = _fox(qd, kd, vd)
        x = _outffn(x, o_a, o_b, c_parts, o_d, mg, wo_b, g2, wg_b, wu_b, wd_b, final_g[None],
                    layer=l, final=(l == depth - 1))
    return x
```

```python
import functools
import math

import numpy as np
import jax
import jax.numpy as jnp
from jax import lax
from jax.experimental import pallas as pl
from jax.experimental.pallas import tpu as pltpu

F32 = jnp.float32
BF16 = jnp.bfloat16

LANES = 128
VMEM_LIMIT_BYTES = 56 * 1024 * 1024

D_MODEL = 1024
HEAD_DIM = 64
N_HEADS = 4
GROUP_W = N_HEADS * HEAD_DIM
KV_HEADS_C = 2
DILATED_GROUPS = ((128, 1), (512, 4), (2048, 16))
BAND = 128
MOBA_BLOCK = 256
MOBA_TOPK = 3
SWA_WINDOW = 128
N_BUCKETS = 32
T5_MAX_DISTANCE = 2048
D_FF = 2816
FF_CHUNK = 256
EPS = 1e-6
NEG = -1e30
SCALE = HEAD_DIM ** -0.5

LOG2E = math.log2(math.e)
QA_B, KA_B, VA_B, QC_B, KC_B, VC_B = 0, 2, 4, 6, 8, 10
NY = 12 * LANES
A_COLS = 3 * GROUP_W
WIDE = N_HEADS * LANES
QB_OFF, KB_OFF, VB_OFF, QD_OFF, KD_OFF, VD_OFF = (NY + g * GROUP_W for g in range(6))
FD_OFF = NY + 6 * GROUP_W
N_PROJ = FD_OFF + LANES
AUG0 = HEAD_DIM

PROJ_TM = 512
FLASH_T = 1024
FOX_TQ, FOX_TK = 1024, 1024
BANDED_UNROLL = 16
MIX_CHUNK = 512
BIAS_CHUNK = 32
UNBOUNDED_DISTANCE = 1 << 16


def _params(sem):
    return pltpu.CompilerParams(dimension_semantics=sem, vmem_limit_bytes=VMEM_LIMIT_BYTES)


def _resident(shape, index_map):
    return pl.BlockSpec(shape, index_map, pipeline_mode=pl.Buffered(1))


def _dot(a, b):
    return jnp.dot(a, b, preferred_element_type=F32)


def _dot_nt(a, b):
    return lax.dot_general(a, b, (((1,), (1,)), ((), ())), preferred_element_type=F32)


def _rms(x, g):
    return x * lax.rsqrt(jnp.mean(x * x, axis=-1, keepdims=True) + EPS) * g


def _rearrange_w_in(w):
    kv_w = KV_HEADS_C * HEAD_DIM
    sizes = [GROUP_W] * 6 + [GROUP_W, kv_w, kv_w] + [GROUP_W] * 3 + [N_HEADS]
    starts = np.concatenate([[0], np.cumsum(sizes)])
    assert w.shape[-1] == starts[-1]
    w = w.astype(BF16)
    qa, ka, va, qb, kb, vb, qc, kc, vc, qd, kd, vd, fd = (w[..., a:b] for a, b in zip(starts[:-1], starts[1:]))

    def per_query_head(kv):
        heads = [kv[..., h * HEAD_DIM:(h + 1) * HEAD_DIM] for h in range(KV_HEADS_C)]
        return [h for h in heads for _ in range(N_HEADS // KV_HEADS_C)]

    fd = jnp.pad(fd, [(0, 0)] * (w.ndim - 1) + [(0, LANES - N_HEADS)])
    out = jnp.concatenate([qa, ka, va, qc, *per_query_head(kc), *per_query_head(vc),
                           qb, kb, vb, qd, kd, vd, fd], axis=-1)
    assert out.shape[-1] == N_PROJ
    return out


def _t5_bucket_np(dist):
    n = np.maximum(dist, 0)
    max_exact = N_BUCKETS // 2
    nf = np.maximum(n, 1).astype(np.float32)
    large = max_exact + (np.log(nf / np.float32(max_exact)) / np.float32(math.log(T5_MAX_DISTANCE / max_exact))
                         * np.float32(N_BUCKETS - max_exact)).astype(np.int32)
    large = np.minimum(large, N_BUCKETS - 1)
    return np.where(n < max_exact, n, large).astype(np.int32)


def _bucket_thresholds():
    n = np.arange(UNBOUNDED_DISTANCE)
    bucket = _t5_bucket_np(n)
    assert (np.diff(bucket) >= 0).all() and bucket[-1] == N_BUCKETS - 1
    thr = np.searchsorted(bucket, np.arange(N_BUCKETS), side="left")
    assert (np.searchsorted(thr, n, side="right") - 1 == bucket).all()
    return tuple(int(t) for t in thr)


def _placement_constants():
    pq = np.zeros((LANES, WIDE), np.float32)
    pk = np.zeros((LANES, WIDE), np.float32)
    cq = np.zeros((1, WIDE), np.float32)
    ck = np.zeros((1, WIDE), np.float32)
    for h in range(N_HEADS):
        for part in range(3):
            pq[part * N_HEADS + h, h * LANES + AUG0 + part] = 1.0
            pk[part * N_HEADS + h, h * LANES + AUG0 + 3 + part] = -1.0
            ck[0, h * LANES + AUG0 + part] = 1.0
            cq[0, h * LANES + AUG0 + 3 + part] = 1.0
    shift = np.zeros((LANES, LANES), np.float32)
    for n in range(AUG0):
        shift[n, AUG0 + n] = 1.0
    return (jnp.asarray(pq, BF16), jnp.asarray(pk, BF16), jnp.asarray(cq), jnp.asarray(ck),
            jnp.asarray(shift, BF16))


def _bias_tiles_kernel(base_ref, table_ref, o_ref, *, rows, cols, step, max_dist, scale, thresholds, chunk):
    t = pl.program_id(0)
    base = base_ref[t]

    def body(c, carry):
        r0 = pl.multiple_of(c * chunk, chunk)
        i = lax.broadcasted_iota(jnp.int32, (chunk, cols), 0) + r0
        j = lax.broadcasted_iota(jnp.int32, (chunk, cols), 1)
        dist = base + i - j
        ok = (dist >= 0) & (dist <= max_dist)
        scaled = dist * step
        vals = [jnp.full((chunk, cols), table_ref[0, h] * scale, F32) for h in range(N_HEADS)]
        for b in range(1, N_BUCKETS):
            reached = scaled >= thresholds[b]
            vals = [jnp.where(reached, table_ref[b, h] * scale, v) for h, v in enumerate(vals)]
        for h in range(N_HEADS):
            o_ref[h, pl.ds(r0, chunk), :] = jnp.where(ok, vals[h], NEG)
        return carry

    d_min, d_max = base - (cols - 1), base + rows - 1
    masked = d_max < 0
    last_bucket = (d_min * step >= thresholds[N_BUCKETS - 1]) & (d_max <= max_dist)

    @pl.when(masked | last_bucket)
    def _():
        for h in range(N_HEADS):
            fill = jnp.where(masked, NEG, table_ref[N_BUCKETS - 1, h] * scale)
            o_ref[h] = jnp.full((rows, cols), fill, F32)

    @pl.when(jnp.logical_not(masked | last_bucket))
    def _():
        lax.fori_loop(0, rows // chunk, body, 0)


def _bias_tiles(table, bases, *, rows, cols, step, max_dist, name, scale=1.0):
    n_tiles = len(bases)
    return pl.pallas_call(
        functools.partial(_bias_tiles_kernel, rows=rows, cols=cols, step=step, max_dist=max_dist, scale=scale,
                          thresholds=_bucket_thresholds(), chunk=BIAS_CHUNK),
        grid=(n_tiles,),
        in_specs=[pl.BlockSpec(memory_space=pltpu.SMEM), pl.BlockSpec(memory_space=pltpu.SMEM)],
        out_specs=pl.BlockSpec((N_HEADS, None, rows, cols), lambda t: (0, t, 0, 0)),
        out_shape=jax.ShapeDtypeStruct((N_HEADS, n_tiles, rows, cols), F32),
        compiler_params=_params(("arbitrary",)),
        name=name,
    )(jnp.asarray(bases, jnp.int32), table)


def _band_bias(table, step, max_dist, name):
    return _bias_tiles(table, (BAND, 0), rows=BAND, cols=2 * BAND, step=step, max_dist=max_dist, name=name)


def _moba_bias(table, n_blocks):
    bases = tuple(MOBA_BLOCK * d for d in range(1 - FLASH_T // MOBA_BLOCK, n_blocks))
    return _bias_tiles(table, bases, rows=MOBA_BLOCK, cols=MOBA_BLOCK, step=1, max_dist=UNBOUNDED_DISTANCE,
                       name="moba_bias", scale=LOG2E)


def _widen(a, low):
    outs = []
    for p in range(GROUP_W // LANES):
        pair = a[:, p * LANES:(p + 1) * LANES]
        outs.append(jnp.where(low, pair, 0.0))
        outs.append(jnp.where(low, pltpu.roll(pair, HEAD_DIM, axis=1), 0.0))
    return jnp.concatenate(outs, axis=1)


def _inproj_kernel(x_ref, g_ref, w_ref, fb_ref, pq_ref, pk_ref, cq_ref, ck_ref,
                   y_ref, ya4_ref, ya16_ref, qb_ref, kb_ref, vb_ref, qd_ref, kd_ref, vd_ref,
                   carry_ref, fold_ref, *, tm):
    t = pl.program_id(1)

    @pl.when(t == 0)
    def _():
        carry_ref[...] = jnp.zeros_like(carry_ref)

    h = _rms(x_ref[...], g_ref[...]).astype(BF16)

    def mm(c0, width=GROUP_W):
        return _dot(h, w_ref[:, c0:c0 + width])

    f = mm(FD_OFF, LANES) + fb_ref[...]
    logf = jnp.minimum(f, 0.0) - jnp.log(1.0 + jnp.exp(-jnp.abs(f)))
    srow = lax.broadcasted_iota(jnp.int32, (tm, LANES), 0)
    c = logf
    k = 1
    while k < tm:
        c = c + jnp.where(srow >= k, pltpu.roll(c, k, axis=0), 0.0)
        k *= 2
    c = c + carry_ref[0:1, :]
    carry_ref[0:1, :] = c[tm - 1:tm, :]

    c2 = c * LOG2E
    c_hi = c2.astype(BF16)
    r1 = c2 - c_hi.astype(F32)
    c_mid = r1.astype(BF16)
    c_lo = (r1 - c_mid.astype(F32)).astype(BF16)
    head_lane = lax.broadcasted_iota(jnp.int32, (tm, LANES), 1) < N_HEADS
    pieces = [jnp.where(head_lane, piece.astype(F32), 0.0) for piece in (c_hi, c_mid, c_lo)]
    feats = (pieces[0] + pltpu.roll(pieces[1], N_HEADS, axis=1)
             + pltpu.roll(pieces[2], 2 * N_HEADS, axis=1)).astype(BF16)

    for grp in range(NY // GROUP_W):
        c0 = grp * GROUP_W
        a = mm(c0)
        if c0 in (QA_B * LANES, QC_B * LANES):
            a = a * SCALE
        y_ref[:, c0:c0 + GROUP_W] = a.astype(BF16)
        if c0 < A_COLS:
            fold_ref[2 * grp] = a[:, :LANES]
            fold_ref[2 * grp + 1] = a[:, LANES:]

    for dil, ref in ((4, ya4_ref), (16, ya16_ref)):
        for r in range(dil):
            for slab in range(A_COLS // LANES):
                piece = fold_ref[slab, pl.ds(r, tm // dil, stride=dil), :]
                ref[r, :, slab * LANES:(slab + 1) * LANES] = piece.astype(BF16)

    low = lax.broadcasted_iota(jnp.int32, (tm, LANES), 1) < HEAD_DIM
    lane = lax.broadcasted_iota(jnp.int32, (tm, WIDE), 1) % LANES
    row = lax.broadcasted_iota(jnp.int32, (tm, WIDE), 0)

    qb_ref[...] = _widen(mm(QB_OFF) * (SCALE * LOG2E), low).astype(BF16)
    vb_ref[...] = jnp.where(lane == AUG0, 1.0, _widen(mm(VB_OFF), low)).astype(BF16)
    vd_ref[...] = jnp.where(lane == AUG0, 1.0, _widen(mm(VD_OFF), low)).astype(BF16)
    blk = (t * tm + row) // MOBA_BLOCK
    kb_ref[...] = jnp.where(lane == AUG0 + blk, 1.0, _widen(mm(KB_OFF), low)).astype(BF16)

    qd = _widen(mm(QD_OFF) * (SCALE * LOG2E), low) + _dot(feats, pq_ref[...]) + cq_ref[...]
    kd = _widen(mm(KD_OFF), low) + _dot(feats, pk_ref[...]) + ck_ref[...]
    qd_ref[...] = qd.astype(BF16)
    kd_ref[...] = kd.astype(BF16)


def _inproj(x, g, w, fb, consts, layer):
    B, S, D = x.shape
    tm = PROJ_TM
    pq, pk, cq, ck, _ = consts
    const = lambda b, t: (0, 0)
    of_layer = lambda b, t: (layer, 0, 0)
    tile = lambda b, t: (b, t, 0)
    fold_tile = lambda b, t: (b, 0, t, 0)
    wide_shape = jax.ShapeDtypeStruct((B, S, WIDE), BF16)
    out_shape = (jax.ShapeDtypeStruct((B, S, NY), BF16),
                 jax.ShapeDtypeStruct((B, 4, S // 4, A_COLS), BF16),
                 jax.ShapeDtypeStruct((B, 16, S // 16, A_COLS), BF16),
                 ) + (wide_shape,) * 6
    wide_spec = pl.BlockSpec((None, tm, WIDE), tile)
    return pl.pallas_call(
        functools.partial(_inproj_kernel, tm=tm),
        grid=(B, S // tm),
        in_specs=[pl.BlockSpec((None, tm, D), tile),
                  _resident((None, 1, D), of_layer),
                  _resident((None, D, N_PROJ), of_layer),
                  _resident((None, 1, LANES), of_layer),
                  _resident((LANES, WIDE), const),
                  _resident((LANES, WIDE), const),
                  _resident((1, WIDE), const),
                  _resident((1, WIDE), const)],
        out_specs=(pl.BlockSpec((None, tm, NY), tile),
                   pl.BlockSpec((None, 4, tm // 4, A_COLS), fold_tile),
                   pl.BlockSpec((None, 16, tm // 16, A_COLS), fold_tile),
                   ) + (wide_spec,) * 6,
        out_shape=out_shape,
        scratch_shapes=[pltpu.VMEM((8, LANES), F32),
                        pltpu.VMEM((A_COLS // LANES, tm, LANES), F32)],
        compiler_params=_params(("arbitrary", "arbitrary")),
        name="inproj",
    )(x, g, w, fb, pq, pk, cq, ck)


def _banded_pass(q_ref, k_ref, v_ref, bias_ref, sinks, o_ref, m_ref, l_ref, *, dil, n_tiles):
    low = lax.broadcasted_iota(jnp.int32, (BAND, LANES), 1) < HEAD_DIM
    keep = (low.astype(F32).astype(BF16), (1.0 - low.astype(F32)).astype(BF16))

    def body(it, carry):
        r = it // n_tiles
        n = it % n_tiles
        r0 = pl.multiple_of(n * BAND, BAND)
        k0 = pl.multiple_of(jnp.maximum(r0 - BAND, 0), BAND)
        variant = jnp.where(n == 0, 1, 0)
        qt = q_ref[r, pl.ds(r0, BAND), :]
        kt = k_ref[r, pl.ds(k0, 2 * BAND), :]
        vt = v_ref[r, pl.ds(k0, 2 * BAND), :]
        accs, ms, ls = [], [], []
        for hh in range(2):
            s = _dot_nt(qt * keep[hh], kt) + bias_ref[hh, variant]
            m = jnp.max(s, axis=1, keepdims=True)
            if sinks is not None:
                m = jnp.maximum(m, sinks[hh])
            p = jnp.exp(s - m)
            l = jnp.sum(p, axis=1, keepdims=True)
            if sinks is not None:
                l = l + jnp.exp(sinks[hh] - m)
            accs.append(_dot(p.astype(BF16), vt))
            ms.append(jnp.broadcast_to(m, (BAND, LANES)))
            ls.append(jnp.broadcast_to(l, (BAND, LANES)))
        rows = pl.ds(r0, BAND) if dil == 1 else pl.ds(r0 * dil + r, BAND, stride=dil)
        o_ref[rows, :] = jnp.where(low, accs[0], accs[1])
        if m_ref is not None:
            m_ref[rows, :] = jnp.where(low, ms[0], ms[1])
        l_ref[rows, :] = jnp.where(low, ls[0], ls[1])
        return carry

    lax.fori_loop(0, dil * n_tiles, body, 0, unroll=BANDED_UNROLL)


def _banded_kernel(q_ref, k_ref, v_ref, bias_ref, sink_ref, o_ref, l_ref, *, dil, n_tiles):
    pair = pl.program_id(1)
    sinks = (sink_ref[2 * pair], sink_ref[2 * pair + 1])
    _banded_pass(q_ref, k_ref, v_ref, bias_ref, sinks, o_ref.at[0], None, l_ref.at[0], dil=dil, n_tiles=n_tiles)


def _dilated_mix_kernel(*refs, dils, seq):
    n = len(dils)
    qkv = [refs[3 * g:3 * g + 3] for g in range(n)]
    biases = refs[3 * n:4 * n]
    o_ref = refs[4 * n]
    results = refs[4 * n + 1]
    for g, dil in enumerate(dils):
        _banded_pass(*qkv[g], biases[g], None, results.at[g, 0], results.at[g, 1], results.at[g, 2], dil=dil,
                     n_tiles=seq // dil // BAND)

    chunk = MIX_CHUNK

    def combine(c, carry):
        rows = pl.ds(pl.multiple_of(c * chunk, chunk), chunk)
        m = functools.reduce(jnp.maximum, [results[g, 1, rows, :] for g in range(n)])
        es = [jnp.exp(results[g, 1, rows, :] - m) for g in range(n)]
        num = sum(e * results[g, 0, rows, :] for g, e in enumerate(es))
        den = sum(e * results[g, 2, rows, :] for g, e in enumerate(es))
        o_ref[rows, :] = num / den
        return carry

    lax.fori_loop(0, seq // chunk, combine, 0)


def _dilated_mix(srcs, biases):
    B = srcs[0].shape[0]
    dils = tuple(src.shape[1] for src in srcs)
    S = srcs[0].shape[1] * srcs[0].shape[2]
    col = lambda blk: (lambda b, p: (b, 0, 0, blk + p))
    in_specs, args = [], []
    for src, dil in zip(srcs, dils):
        for blk in (QA_B, KA_B, VA_B):
            in_specs.append(pl.BlockSpec((None, dil, S // dil, LANES), col(blk)))
            args.append(src)
    for bias in biases:
        in_specs.append(pl.BlockSpec((2, 2, BAND, 2 * BAND), lambda b, p: (p, 0, 0, 0)))
        args.append(bias)
    scratch = [pltpu.VMEM((len(dils), 3, S, LANES), F32)]
    return pl.pallas_call(
        functools.partial(_dilated_mix_kernel, dils=dils, seq=S),
        grid=(B, 2),
        in_specs=in_specs,
        out_specs=pl.BlockSpec((None, S, LANES), lambda b, p: (b, 0, p)),
        out_shape=jax.ShapeDtypeStruct((B, S, GROUP_W), F32),
        scratch_shapes=scratch,
        compiler_params=_params(("arbitrary",) * 2),
        name="dilated_mix",
    )(*args)


def _swa(src, bias, sinks):
    B, dil, L, _ = src.shape
    n_tiles = L // BAND
    assert n_tiles & (n_tiles - 1) == 0 and (dil * n_tiles) % BANDED_UNROLL == 0
    col = lambda blk: (lambda b, p: (b, 0, 0, blk + p))
    o_spec = pl.BlockSpec((None, dil, L, LANES), lambda b, p: (b, 0, 0, p))
    o_shape = jax.ShapeDtypeStruct((B, dil, L, GROUP_W), F32)
    return pl.pallas_call(
        functools.partial(_banded_kernel, dil=dil, n_tiles=n_tiles),
        grid=(B, 2),
        in_specs=[pl.BlockSpec((None, dil, L, LANES), col(QC_B)),
                  pl.BlockSpec((None, dil, L, LANES), col(KC_B)),
                  pl.BlockSpec((None, dil, L, LANES), col(VC_B)),
                  pl.BlockSpec((2, 2, BAND, 2 * BAND), lambda b, p: (p, 0, 0, 0)),
                  pl.BlockSpec(memory_space=pltpu.SMEM)],
        out_specs=(o_spec,) * 2,
        out_shape=(o_shape,) * 2,
        compiler_params=_params(("arbitrary",) * 2),
        name="swa",
    )(src, src, src, bias, sinks)


def _online_update(scores, vts, carry):
    n_heads = len(scores)
    ms = []
    for hh in range(n_heads):
        m_new = carry[2 * hh]
        for s in scores[hh]:
            m_new = jnp.maximum(m_new, jnp.max(s, axis=1, keepdims=True))
        ms.append(m_new)
    new = []
    for hh in range(n_heads):
        m, acc = carry[2 * hh:2 * hh + 2]
        ps = [jnp.exp2(s - ms[hh]).astype(BF16) for s in scores[hh]]
        p = ps[0] if len(ps) == 1 else jnp.concatenate(ps, axis=1)
        acc = jnp.exp2(m - ms[hh]) * acc + _dot(p, vts[hh])
        new.extend((ms[hh], acc))
    return tuple(new)


def _flash_init(tq, n_heads=2):
    return (jnp.full((tq, 1), NEG, F32), jnp.zeros((tq, LANES), F32)) * n_heads


def _finish_pair(o_ref, carry, tq):
    low = lax.broadcasted_iota(jnp.int32, (tq, LANES), 1) < HEAD_DIM
    outs = [acc / acc[:, AUG0:AUG0 + 1] for acc in (carry[1], carry[3])]
    o_ref[...] = jnp.where(low, outs[0], pltpu.roll(outs[1], HEAD_DIM, axis=1))


def _moba_kernel(q_ref, k_ref, v_ref, bias_ref, shift_ref, o_ref, qaug_ref, *, n_blocks):
    i = pl.program_id(2)
    t, blk = FLASH_T, MOBA_BLOCK
    per = t // blk
    seq = n_blocks * blk

    @pl.when(i == 0)
    def _():
        blk_id = lax.broadcasted_iota(jnp.int32, (n_blocks, seq), 0)
        blk_f = blk_id.astype(F32)
        own = lax.broadcasted_iota(jnp.int32, (n_blocks, seq), 1) // blk
        for hh in range(2):
            q = q_ref[:, hh * LANES:(hh + 1) * LANES]
            kf = k_ref[:, hh * LANES:(hh + 1) * LANES].astype(F32)
            km = jnp.mean(kf.reshape(n_blocks, blk, LANES), axis=1)
            k1 = km.astype(BF16)
            r1 = km - k1.astype(F32)
            k2 = r1.astype(BF16)
            k3 = (r1 - k2.astype(F32)).astype(BF16)
            gate = _dot_nt(k1, q) + _dot_nt(k2, q) + _dot_nt(k3, q)

            avail = blk_id < own
            sel = jnp.zeros((n_blocks, seq), jnp.bool_)
            for _ in range(MOBA_TOPK):
                cur = jnp.where(avail, gate, -jnp.inf)
                top = jnp.max(cur, axis=0, keepdims=True)
                is_top = avail & (cur == top)
                first = jnp.min(jnp.where(is_top, blk_f, float(n_blocks)), axis=0, keepdims=True)
                pick = blk_f == first
                sel = sel | pick
                avail = avail & jnp.logical_not(pick)
            penalty = jnp.where(sel | (blk_id == own), 0.0, NEG).astype(BF16)
            placed = lax.dot_general(penalty, shift_ref[0:n_blocks, :], (((0,), (0,)), ((), ())),
                                     preferred_element_type=F32)
            qaug_ref[:, hh * LANES:(hh + 1) * LANES] = (q.astype(F32) + placed).astype(BF16)

    q0 = pl.multiple_of(i * t, t)
    q_aug = [qaug_ref[pl.ds(q0, t), hh * LANES:(hh + 1) * LANES] for hh in range(2)]

    def tile(n, q_rows, k_lo, k_hi, carry_rows):
        r0 = pl.multiple_of(n * t + k_lo * blk, blk)
        width = (k_hi - k_lo) * blk
        d0 = per * (i - n) + per - 1
        scores, vts = [], []
        for hh in range(2):
            kt = k_ref[pl.ds(r0, width), hh * LANES:(hh + 1) * LANES]
            s = _dot_nt(q_aug[hh][q_rows[0] * blk:q_rows[1] * blk], kt)
            rows = [jnp.concatenate([s[(a - q_rows[0]) * blk:(a - q_rows[0] + 1) * blk,
                                       (b - k_lo) * blk:(b - k_lo + 1) * blk] + bias_ref[hh, d0 + a - b]
                                     for b in range(k_lo, k_hi)], axis=1) for a in range(*q_rows)]
            scores.append([jnp.concatenate(rows, axis=0)])
            vts.append(v_ref[pl.ds(r0, width), hh * LANES:(hh + 1) * LANES])
        return _online_update(scores, vts, carry_rows)

    carry = lax.fori_loop(0, i, lambda n, c: tile(n, (0, per), 0, per, c), _flash_init(t))
    hp = per // 2
    carry = tile(i, (0, per), 0, hp, carry)
    low_rows = tile(i, (hp, per), hp, per, tuple(c[hp * blk:] for c in carry))
    carry = tuple(jnp.concatenate([c[:hp * blk], lo], axis=0) for c, lo in zip(carry, low_rows))
    _finish_pair(o_ref, carry, t)


def _moba(qb, kb, vb, bias, shift):
    B, S, _ = qb.shape
    nb = S // MOBA_BLOCK
    t = FLASH_T
    assert nb <= AUG0 and t % (2 * MOBA_BLOCK) == 0 and S % t == 0
    n_bias = nb + t // MOBA_BLOCK - 1
    return pl.pallas_call(
        functools.partial(_moba_kernel, n_blocks=nb),
        grid=(B, 2, S // t),
        in_specs=[pl.BlockSpec((None, S, 2 * LANES), lambda b, p, i: (b, 0, p)),
                  pl.BlockSpec((None, S, 2 * LANES), lambda b, p, i: (b, 0, p)),
                  pl.BlockSpec((None, S, 2 * LANES), lambda b, p, i: (b, 0, p)),
                  pl.BlockSpec((2, n_bias, MOBA_BLOCK, MOBA_BLOCK), lambda b, p, i: (p, 0, 0, 0)),
                  pl.BlockSpec((LANES, LANES), lambda b, p, i: (0, 0))],
        out_specs=pl.BlockSpec((None, t, LANES), lambda b, p, i: (b, i, p)),
        out_shape=jax.ShapeDtypeStruct((B, S, GROUP_W), F32),
        scratch_shapes=[pltpu.VMEM((S, 2 * LANES), BF16)],
        compiler_params=_params(("arbitrary",) * 3),
        name="moba",
    )(qb, kb, vb, bias, shift)


def _fox_kernel(q_ref, k_ref, v_ref, o_ref, *, tq, tk):
    i = pl.program_id(2)
    qs = [q_ref[:, hh * LANES:(hh + 1) * LANES] for hh in range(2)]

    def step(n, carry):
        r0 = pl.multiple_of(n * tk, tk)
        scores, vts = [], []
        for hh in range(2):
            kt = k_ref[pl.ds(r0, tk), hh * LANES:(hh + 1) * LANES]
            scores.append([_dot_nt(qs[hh], kt)])
            vts.append(v_ref[pl.ds(r0, tk), hh * LANES:(hh + 1) * LANES])
        return _online_update(scores, vts, carry)

    carry = lax.fori_loop(0, i, step, _flash_init(tq))
    half = tq // 2
    d0 = pl.multiple_of(i * tq, tq)

    def diag(q_lo, k0, carry_rows):
        rows = tq - q_lo
        row = lax.broadcasted_iota(jnp.int32, (rows, half), 0)
        col = lax.broadcasted_iota(jnp.int32, (rows, half), 1)
        scores, vts = [], []
        for hh in range(2):
            kt = k_ref[pl.ds(k0, half), hh * LANES:(hh + 1) * LANES]
            s = jnp.where(col <= row, _dot_nt(q_ref[q_lo:tq, hh * LANES:(hh + 1) * LANES], kt), NEG)
            scores.append([s])
            vts.append(v_ref[pl.ds(k0, half), hh * LANES:(hh + 1) * LANES])
        return _online_update(scores, vts, carry_rows)

    carry = diag(0, d0, carry)
    low_rows = diag(half, d0 + half, tuple(c[half:] for c in carry))
    carry = tuple(jnp.concatenate([c[:half], lo], axis=0) for c, lo in zip(carry, low_rows))
    _finish_pair(o_ref, carry, tq)


def _fox(qd, kd, vd):
    B, S, _ = qd.shape
    tq, tk = FOX_TQ, FOX_TK
    assert S % tk == 0 and tk == tq
    return pl.pallas_call(
        functools.partial(_fox_kernel, tq=tq, tk=tk),
        grid=(B, 2, S // tq),
        in_specs=[pl.BlockSpec((None, tq, 2 * LANES), lambda b, p, i: (b, i, p)),
                  pl.BlockSpec((None, S, 2 * LANES), lambda b, p, i: (b, 0, p)),
                  pl.BlockSpec((None, S, 2 * LANES), lambda b, p, i: (b, 0, p))],
        out_specs=pl.BlockSpec((None, tq, LANES), lambda b, p, i: (b, i, p)),
        out_shape=jax.ShapeDtypeStruct((B, S, GROUP_W), F32),
        compiler_params=_params(("arbitrary",) * 3),
        name="fox",
    )(qd, kd, vd)


def _outffn_kernel(x_ref, oa_ref, ob_ref, oc_ref, lc_ref, od_ref,
                   mg_ref, wo_ref, g2_ref, wg_ref, wu_ref, wd_ref, fg_ref, out_ref, *, final):
    mix_a = oa_ref[...]
    mix_c = oc_ref[...] / lc_ref[...]

    x1 = x_ref[...]
    for g, part in enumerate((mix_a, ob_ref[...], mix_c, od_ref[...])):
        c0 = g * GROUP_W
        normed = _rms(part, mg_ref[:, c0:c0 + GROUP_W]).astype(BF16)
        x1 = x1 + _dot(normed, wo_ref[c0:c0 + GROUP_W, :])

    h = _rms(x1, g2_ref[...]).astype(BF16)
    ffn = jnp.zeros_like(x1)
    for c in range(D_FF // FF_CHUNK):
        c0 = c * FF_CHUNK
        gate = _dot(h, wg_ref[:, c0:c0 + FF_CHUNK])
        up = _dot(h, wu_ref[:, c0:c0 + FF_CHUNK])
        act = (gate * (1.0 / (1.0 + jnp.exp(-gate))) * up).astype(BF16)
        ffn = ffn + _dot(act, wd_ref[c0:c0 + FF_CHUNK, :])
    x2 = x1 + ffn
    if final:
        x2 = _rms(x2, fg_ref[...])
    out_ref[...] = x2


def _outffn(x, oa, ob, c_parts, od, mg, wo, g2, wg, wu, wd, fg, *, layer, final):
    B, S, D = x.shape
    tm = PROJ_TM
    tile = lambda b, t: (b, t, 0)
    const = lambda b, t: (0, 0)
    of_layer = lambda b, t: (layer, 0, 0)
    flat_spec = pl.BlockSpec((None, tm, GROUP_W), tile)
    c_spec = pl.BlockSpec((None, None, tm, GROUP_W), lambda b, t: (b, 0, t, 0))
    return pl.pallas_call(
        functools.partial(_outffn_kernel, final=final),
        grid=(B, S // tm),
        in_specs=[pl.BlockSpec((None, tm, D), tile), flat_spec, flat_spec, c_spec, c_spec, flat_spec,
                  _resident((None, 1, D), of_layer),
                  _resident((None, D, D), of_layer),
                  _resident((None, 1, D), of_layer),
                  _resident((None, D, D_FF), of_layer),
                  _resident((None, D, D_FF), of_layer),
                  _resident((None, D_FF, D), of_layer),
                  _resident((1, D), const)],
        out_specs=pl.BlockSpec((None, tm, D), tile),
        out_shape=jax.ShapeDtypeStruct((B, S, D), F32),
        compiler_params=_params(("arbitrary", "arbitrary")),
        name="outffn",
    )(x, oa, ob, c_parts[0], c_parts[1], od, mg, wo, g2, wg, wu, wd, fg)


def kernel(x, norm1_g, w_in, f_bias, sinks, mix_norm_g, w_out, norm2_g, w_gate, w_up, w_down, rel_bias, final_g):
    B, S, D = x.shape
    depth = w_in.shape[0]
    max_dil = max(dil for _, dil in DILATED_GROUPS)
    assert D == D_MODEL and S % (2 * BAND * max_dil) == 0 and w_gate.shape[-1] == D_FF
    assert [dil for _, dil in DILATED_GROUPS] == [1, 4, 16] and S % PROJ_TM == 0

    w_proj = _rearrange_w_in(w_in)
    wo_b, wg_b, wu_b, wd_b = (w.astype(BF16) for w in (w_out, w_gate, w_up, w_down))
    fb = jnp.pad(f_bias, ((0, 0), (0, LANES - N_HEADS)))[:, None, :]
    g1, mg, g2 = norm1_g[:, None, :], mix_norm_g[:, None, :], norm2_g[:, None, :]
    consts = _placement_constants()
    table_a = rel_bias[:, :N_HEADS]
    table_b = rel_bias[:, N_HEADS:2 * N_HEADS]
    table_c = rel_bias[:, 2 * N_HEADS:]
    bias_a = [_band_bias(table_a, dil, window // dil, f"dilated{dil}_bias") for window, dil in DILATED_GROUPS]
    bias_c = _band_bias(table_c, 1, SWA_WINDOW - 1, "swa_bias")
    bias_b = _moba_bias(table_b, S // MOBA_BLOCK)

    for l in range(depth):
        y, ya4, ya16, qb, kb, vb, qd, kd, vd = _inproj(x, g1, w_proj, fb, consts, l)
        y4 = y.reshape(B, 1, S, NY)
        o_a = _dilated_mix((y4, ya4, ya16), bias_a)
        c_parts = _swa(y4, bias_c, sinks[l])
        o_b = _moba(qb, kb, vb, bias_b, consts[4])
        o_d = _fox(qd, kd, vd)
        x = _outffn(x, o_a, o_b, c_parts, o_d, mg, wo_b, g2, wg_b, wu_b, wd_b, final_g[None],
                    layer=l, final=(l == depth - 1))
    return x
```

```python
import functools
import math

import numpy as np
import jax
import jax.numpy as jnp
from jax import lax
from jax.experimental import pallas as pl
from jax.experimental.pallas import tpu as pltpu

F32 = jnp.float32
BF16 = jnp.bfloat16

LANES = 128
VMEM_LIMIT_BYTES = 60 * 1024 * 1024

D_MODEL = 1024
HEAD_DIM = 64
N_HEADS = 4
GROUP_W = N_HEADS * HEAD_DIM
KV_HEADS_C = 2
DILATED_GROUPS = ((128, 1), (512, 4), (2048, 16))
BAND = 128
MOBA_BLOCK = 256
MOBA_TOPK = 3
SWA_WINDOW = 128
N_BUCKETS = 32
T5_MAX_DISTANCE = 2048
D_FF = 2816
FF_CHUNK = 256
EPS = 1e-6
NEG = -1e30
SCALE = HEAD_DIM ** -0.5

LOG2E = math.log2(math.e)
QA_B, KA_B, VA_B, QC_B, KC_B, VC_B = 0, 2, 4, 6, 8, 10
NY = 12 * LANES
A_COLS = 3 * GROUP_W
WIDE = N_HEADS * LANES
QB_OFF, KB_OFF, VB_OFF, QD_OFF, KD_OFF, VD_OFF = (NY + g * GROUP_W for g in range(6))
FD_OFF = NY + 6 * GROUP_W
N_PROJ = FD_OFF + LANES
AUG0 = HEAD_DIM

PROJ_TM = 512
OUT_TM = 1024
FLASH_T = 1024
FOX_TQ, FOX_TK = 1024, 1024
BANDED_UNROLL = 16
MIX_CHUNK = 512
BIAS_CHUNK = 32
UNBOUNDED_DISTANCE = 1 << 16


def _params(sem):
    return pltpu.CompilerParams(dimension_semantics=sem, vmem_limit_bytes=VMEM_LIMIT_BYTES)


def _resident(shape, index_map):
    return pl.BlockSpec(shape, index_map, pipeline_mode=pl.Buffered(1))


def _dot(a, b):
    return jnp.dot(a, b, preferred_element_type=F32)


def _dot_nt(a, b):
    return lax.dot_general(a, b, (((1,), (1,)), ((), ())), preferred_element_type=F32)


def _rms(x, g):
    return x * lax.rsqrt(jnp.mean(x * x, axis=-1, keepdims=True) + EPS) * g


def _rearrange_w_in(w):
    kv_w = KV_HEADS_C * HEAD_DIM
    sizes = [GROUP_W] * 6 + [GROUP_W, kv_w, kv_w] + [GROUP_W] * 3 + [N_HEADS]
    starts = np.concatenate([[0], np.cumsum(sizes)])
    assert w.shape[-1] == starts[-1]
    w = w.astype(BF16)
    qa, ka, va, qb, kb, vb, qc, kc, vc, qd, kd, vd, fd = (w[..., a:b] for a, b in zip(starts[:-1], starts[1:]))

    def per_query_head(kv):
        heads = [kv[..., h * HEAD_DIM:(h + 1) * HEAD_DIM] for h in range(KV_HEADS_C)]
        return [h for h in heads for _ in range(N_HEADS // KV_HEADS_C)]

    fd = jnp.pad(fd, [(0, 0)] * (w.ndim - 1) + [(0, LANES - N_HEADS)])
    out = jnp.concatenate([qa, ka, va, qc, *per_query_head(kc), *per_query_head(vc),
                           qb, kb, vb, qd, kd, vd, fd], axis=-1)
    assert out.shape[-1] == N_PROJ
    return out


def _t5_bucket_np(dist):
    n = np.maximum(dist, 0)
    max_exact = N_BUCKETS // 2
    nf = np.maximum(n, 1).astype(np.float32)
    large = max_exact + (np.log(nf / np.float32(max_exact)) / np.float32(math.log(T5_MAX_DISTANCE / max_exact))
                         * np.float32(N_BUCKETS - max_exact)).astype(np.int32)
    large = np.minimum(large, N_BUCKETS - 1)
    return np.where(n < max_exact, n, large).astype(np.int32)


def _bucket_thresholds():
    n = np.arange(UNBOUNDED_DISTANCE)
    bucket = _t5_bucket_np(n)
    assert (np.diff(bucket) >= 0).all() and bucket[-1] == N_BUCKETS - 1
    thr = np.searchsorted(bucket, np.arange(N_BUCKETS), side="left")
    assert (np.searchsorted(thr, n, side="right") - 1 == bucket).all()
    return tuple(int(t) for t in thr)


def _placement_constants():
    pq = np.zeros((LANES, WIDE), np.float32)
    pk = np.zeros((LANES, WIDE), np.float32)
    cq = np.zeros((1, WIDE), np.float32)
    ck = np.zeros((1, WIDE), np.float32)
    for h in range(N_HEADS):
        for part in range(3):
            pq[part * N_HEADS + h, h * LANES + AUG0 + part] = 1.0
            pk[part * N_HEADS + h, h * LANES + AUG0 + 3 + part] = -1.0
            ck[0, h * LANES + AUG0 + part] = 1.0
            cq[0, h * LANES + AUG0 + 3 + part] = 1.0
    shift = np.zeros((LANES, LANES), np.float32)
    for n in range(AUG0):
        shift[n, AUG0 + n] = 1.0
    return (jnp.asarray(pq, BF16), jnp.asarray(pk, BF16), jnp.asarray(cq), jnp.asarray(ck),
            jnp.asarray(shift, BF16))


def _bias_tiles_kernel(base_ref, table_ref, o_ref, *, rows, cols, step, max_dist, scale, thresholds, chunk):
    t = pl.program_id(0)
    base = base_ref[t]

    def body(c, carry):
        r0 = pl.multiple_of(c * chunk, chunk)
        i = lax.broadcasted_iota(jnp.int32, (chunk, cols), 0) + r0
        j = lax.broadcasted_iota(jnp.int32, (chunk, cols), 1)
        dist = base + i - j
        ok = (dist >= 0) & (dist <= max_dist)
        scaled = dist * step
        vals = [jnp.full((chunk, cols), table_ref[0, h] * scale, F32) for h in range(N_HEADS)]
        for b in range(1, N_BUCKETS):
            reached = scaled >= thresholds[b]
            vals = [jnp.where(reached, table_ref[b, h] * scale, v) for h, v in enumerate(vals)]
        for h in range(N_HEADS):
            o_ref[h, pl.ds(r0, chunk), :] = jnp.where(ok, vals[h], NEG)
        return carry

    d_min, d_max = base - (cols - 1), base + rows - 1
    masked = d_max < 0
    last_bucket = (d_min * step >= thresholds[N_BUCKETS - 1]) & (d_max <= max_dist)

    @pl.when(masked | last_bucket)
    def _():
        for h in range(N_HEADS):
            fill = jnp.where(masked, NEG, table_ref[N_BUCKETS - 1, h] * scale)
            o_ref[h] = jnp.full((rows, cols), fill, F32)

    @pl.when(jnp.logical_not(masked | last_bucket))
    def _():
        lax.fori_loop(0, rows // chunk, body, 0)


def _bias_tiles(table, bases, *, rows, cols, step, max_dist, name, scale=1.0):
    n_tiles = len(bases)
    return pl.pallas_call(
        functools.partial(_bias_tiles_kernel, rows=rows, cols=cols, step=step, max_dist=max_dist, scale=scale,
                          thresholds=_bucket_thresholds(), chunk=BIAS_CHUNK),
        grid=(n_tiles,),
        in_specs=[pl.BlockSpec(memory_space=pltpu.SMEM), pl.BlockSpec(memory_space=pltpu.SMEM)],
        out_specs=pl.BlockSpec((N_HEADS, None, rows, cols), lambda t: (0, t, 0, 0)),
        out_shape=jax.ShapeDtypeStruct((N_HEADS, n_tiles, rows, cols), F32),
        compiler_params=_params(("arbitrary",)),
        name=name,
    )(jnp.asarray(bases, jnp.int32), table)


def _band_bias(table, step, max_dist, name):
    return _bias_tiles(table, (BAND, 0), rows=BAND, cols=2 * BAND, step=step, max_dist=max_dist, name=name)


def _moba_bias(table, n_blocks):
    bases = tuple(MOBA_BLOCK * d for d in range(1 - FLASH_T // MOBA_BLOCK, n_blocks))
    return _bias_tiles(table, bases, rows=MOBA_BLOCK, cols=MOBA_BLOCK, step=1, max_dist=UNBOUNDED_DISTANCE,
                       name="moba_bias", scale=LOG2E)


def _widen(a, low):
    outs = []
    for p in range(GROUP_W // LANES):
        pair = a[:, p * LANES:(p + 1) * LANES]
        outs.append(jnp.where(low, pair, 0.0))
        outs.append(jnp.where(low, pltpu.roll(pair, HEAD_DIM, axis=1), 0.0))
    return jnp.concatenate(outs, axis=1)


def _inproj_kernel(x_ref, g_ref, w_ref, fb_ref, pq_ref, pk_ref, cq_ref, ck_ref,
                   y_ref, ya4_ref, ya16_ref, qb_ref, kb_ref, vb_ref, qd_ref, kd_ref, vd_ref,
                   carry_ref, fold_ref, *, tm):
    t = pl.program_id(1)

    @pl.when(t == 0)
    def _():
        carry_ref[...] = jnp.zeros_like(carry_ref)

    h = _rms(x_ref[...], g_ref[...]).astype(BF16)

    def mm(c0, width=GROUP_W):
        return _dot(h, w_ref[:, c0:c0 + width])

    f = mm(FD_OFF, LANES) + fb_ref[...]
    logf = jnp.minimum(f, 0.0) - jnp.log(1.0 + jnp.exp(-jnp.abs(f)))
    srow = lax.broadcasted_iota(jnp.int32, (tm, LANES), 0)
    c = logf
    k = 1
    while k < tm:
        c = c + jnp.where(srow >= k, pltpu.roll(c, k, axis=0), 0.0)
        k *= 2
    c = c + carry_ref[0:1, :]
    carry_ref[0:1, :] = c[tm - 1:tm, :]

    c2 = c * LOG2E
    c_hi = c2.astype(BF16)
    r1 = c2 - c_hi.astype(F32)
    c_mid = r1.astype(BF16)
    c_lo = (r1 - c_mid.astype(F32)).astype(BF16)
    head_lane = lax.broadcasted_iota(jnp.int32, (tm, LANES), 1) < N_HEADS
    pieces = [jnp.where(head_lane, piece.astype(F32), 0.0) for piece in (c_hi, c_mid, c_lo)]
    feats = (pieces[0] + pltpu.roll(pieces[1], N_HEADS, axis=1)
             + pltpu.roll(pieces[2], 2 * N_HEADS, axis=1)).astype(BF16)

    for grp in range(NY // GROUP_W):
        c0 = grp * GROUP_W
        a = mm(c0)
        if c0 in (QA_B * LANES, QC_B * LANES):
            a = a * SCALE
        y_ref[:, c0:c0 + GROUP_W] = a.astype(BF16)
        if c0 < A_COLS:
            fold_ref[2 * grp] = a[:, :LANES]
            fold_ref[2 * grp + 1] = a[:, LANES:]

    for dil, ref in ((4, ya4_ref), (16, ya16_ref)):
        for r in range(dil):
            for slab in range(A_COLS // LANES):
                piece = fold_ref[slab, pl.ds(r, tm // dil, stride=dil), :]
                ref[r, :, slab * LANES:(slab + 1) * LANES] = piece.astype(BF16)

    low = lax.broadcasted_iota(jnp.int32, (tm, LANES), 1) < HEAD_DIM
    lane = lax.broadcasted_iota(jnp.int32, (tm, WIDE), 1) % LANES
    row = lax.broadcasted_iota(jnp.int32, (tm, WIDE), 0)

    qb_ref[...] = _widen(mm(QB_OFF) * (SCALE * LOG2E), low).astype(BF16)
    vb_ref[...] = jnp.where(lane == AUG0, 1.0, _widen(mm(VB_OFF), low)).astype(BF16)
    vd_ref[...] = jnp.where(lane == AUG0, 1.0, _widen(mm(VD_OFF), low)).astype(BF16)
    blk = (t * tm + row) // MOBA_BLOCK
    kb_ref[...] = jnp.where(lane == AUG0 + blk, 1.0, _widen(mm(KB_OFF), low)).astype(BF16)

    qd = _widen(mm(QD_OFF) * (SCALE * LOG2E), low) + _dot(feats, pq_ref[...]) + cq_ref[...]
    kd = _widen(mm(KD_OFF), low) + _dot(feats, pk_ref[...]) + ck_ref[...]
    qd_ref[...] = qd.astype(BF16)
    kd_ref[...] = kd.astype(BF16)


def _inproj(x, g, w, fb, consts, layer):
    B, S, D = x.shape
    tm = PROJ_TM
    pq, pk, cq, ck, _ = consts
    const = lambda b, t: (0, 0)
    of_layer = lambda b, t: (layer, 0, 0)
    tile = lambda b, t: (b, t, 0)
    fold_tile = lambda b, t: (b, 0, t, 0)
    wide_shape = jax.ShapeDtypeStruct((B, S, WIDE), BF16)
    out_shape = (jax.ShapeDtypeStruct((B, S, NY), BF16),
                 jax.ShapeDtypeStruct((B, 4, S // 4, A_COLS), BF16),
                 jax.ShapeDtypeStruct((B, 16, S // 16, A_COLS), BF16),
                 ) + (wide_shape,) * 6
    wide_spec = pl.BlockSpec((None, tm, WIDE), tile)
    return pl.pallas_call(
        functools.partial(_inproj_kernel, tm=tm),
        grid=(B, S // tm),
        in_specs=[pl.BlockSpec((None, tm, D), tile),
                  _resident((None, 1, D), of_layer),
                  _resident((None, D, N_PROJ), of_layer),
                  _resident((None, 1, LANES), of_layer),
                  _resident((LANES, WIDE), const),
                  _resident((LANES, WIDE), const),
                  _resident((1, WIDE), const),
                  _resident((1, WIDE), const)],
        out_specs=(pl.BlockSpec((None, tm, NY), tile),
                   pl.BlockSpec((None, 4, tm // 4, A_COLS), fold_tile),
                   pl.BlockSpec((None, 16, tm // 16, A_COLS), fold_tile),
                   ) + (wide_spec,) * 6,
        out_shape=out_shape,
        scratch_shapes=[pltpu.VMEM((8, LANES), F32),
                        pltpu.VMEM((A_COLS // LANES, tm, LANES), F32)],
        compiler_params=_params(("arbitrary", "arbitrary")),
        name="inproj",
    )(x, g, w, fb, pq, pk, cq, ck)


def _banded_pass(q_ref, k_ref, v_ref, bias_ref, sinks, o_ref, m_ref, l_ref, *, dil, n_tiles):
    low = lax.broadcasted_iota(jnp.int32, (BAND, LANES), 1) < HEAD_DIM
    keep = (low.astype(F32).astype(BF16), (1.0 - low.astype(F32)).astype(BF16))

    def body(it, carry):
        r = it // n_tiles
        n = it % n_tiles
        r0 = pl.multiple_of(n * BAND, BAND)
        k0 = pl.multiple_of(jnp.maximum(r0 - BAND, 0), BAND)
        variant = jnp.where(n == 0, 1, 0)
        qt = q_ref[r, pl.ds(r0, BAND), :]
        kt = k_ref[r, pl.ds(k0, 2 * BAND), :]
        vt = v_ref[r, pl.ds(k0, 2 * BAND), :]
        accs, ms, ls = [], [], []
        for hh in range(2):
            s = _dot_nt(qt * keep[hh], kt) + bias_ref[hh, variant]
            m = jnp.max(s, axis=1, keepdims=True)
            if sinks is not None:
                m = jnp.maximum(m, sinks[hh])
            p = jnp.exp(s - m)
            l = jnp.sum(p, axis=1, keepdims=True)
            if sinks is not None:
                l = l + jnp.exp(sinks[hh] - m)
            accs.append(_dot(p.astype(BF16), vt))
            ms.append(jnp.broadcast_to(m, (BAND, LANES)))
            ls.append(jnp.broadcast_to(l, (BAND, LANES)))
        rows = pl.ds(r0, BAND) if dil == 1 else pl.ds(r0 * dil + r, BAND, stride=dil)
        o_ref[rows, :] = jnp.where(low, accs[0], accs[1])
        if m_ref is not None:
            m_ref[rows, :] = jnp.where(low, ms[0], ms[1])
        l_ref[rows, :] = jnp.where(low, ls[0], ls[1])
        return carry

    lax.fori_loop(0, dil * n_tiles, body, 0, unroll=BANDED_UNROLL)


def _banded_kernel(q_ref, k_ref, v_ref, bias_ref, sink_ref, o_ref, l_ref, *, dil, n_tiles):
    pair = pl.program_id(1)
    sinks = (sink_ref[2 * pair], sink_ref[2 * pair + 1])
    _banded_pass(q_ref, k_ref, v_ref, bias_ref, sinks, o_ref.at[0], None, l_ref.at[0], dil=dil, n_tiles=n_tiles)


def _dilated_mix_kernel(*refs, dils, seq):
    n = len(dils)
    qkv = [refs[3 * g:3 * g + 3] for g in range(n)]
    biases = refs[3 * n:4 * n]
    o_ref = refs[4 * n]
    results = refs[4 * n + 1]
    for g, dil in enumerate(dils):
        _banded_pass(*qkv[g], biases[g], None, results.at[g, 0], results.at[g, 1], results.at[g, 2], dil=dil,
                     n_tiles=seq // dil // BAND)

    chunk = MIX_CHUNK

    def combine(c, carry):
        rows = pl.ds(pl.multiple_of(c * chunk, chunk), chunk)
        m = functools.reduce(jnp.maximum, [results[g, 1, rows, :] for g in range(n)])
        es = [jnp.exp(results[g, 1, rows, :] - m) for g in range(n)]
        num = sum(e * results[g, 0, rows, :] for g, e in enumerate(es))
        den = sum(e * results[g, 2, rows, :] for g, e in enumerate(es))
        o_ref[rows, :] = num / den
        return carry

    lax.fori_loop(0, seq // chunk, combine, 0)


def _dilated_mix(srcs, biases):
    B = srcs[0].shape[0]
    dils = tuple(src.shape[1] for src in srcs)
    S = srcs[0].shape[1] * srcs[0].shape[2]
    col = lambda blk: (lambda b, p: (b, 0, 0, blk + p))
    in_specs, args = [], []
    for src, dil in zip(srcs, dils):
        for blk in (QA_B, KA_B, VA_B):
            in_specs.append(pl.BlockSpec((None, dil, S // dil, LANES), col(blk)))
            args.append(src)
    for bias in biases:
        in_specs.append(pl.BlockSpec((2, 2, BAND, 2 * BAND), lambda b, p: (p, 0, 0, 0)))
        args.append(bias)
    scratch = [pltpu.VMEM((len(dils), 3, S, LANES), F32)]
    return pl.pallas_call(
        functools.partial(_dilated_mix_kernel, dils=dils, seq=S),
        grid=(B, 2),
        in_specs=in_specs,
        out_specs=pl.BlockSpec((None, S, LANES), lambda b, p: (b, 0, p)),
        out_shape=jax.ShapeDtypeStruct((B, S, GROUP_W), F32),
        scratch_shapes=scratch,
        compiler_params=_params(("arbitrary",) * 2),
        name="dilated_mix",
    )(*args)


def _swa(src, bias, sinks):
    B, dil, L, _ = src.shape
    n_tiles = L // BAND
    assert n_tiles & (n_tiles - 1) == 0 and (dil * n_tiles) % BANDED_UNROLL == 0
    col = lambda blk: (lambda b, p: (b, 0, 0, blk + p))
    o_spec = pl.BlockSpec((None, dil, L, LANES), lambda b, p: (b, 0, 0, p))
    o_shape = jax.ShapeDtypeStruct((B, dil, L, GROUP_W), F32)
    return pl.pallas_call(
        functools.partial(_banded_kernel, dil=dil, n_tiles=n_tiles),
        grid=(B, 2),
        in_specs=[pl.BlockSpec((None, dil, L, LANES), col(QC_B)),
                  pl.BlockSpec((None, dil, L, LANES), col(KC_B)),
                  pl.BlockSpec((None, dil, L, LANES), col(VC_B)),
                  pl.BlockSpec((2, 2, BAND, 2 * BAND), lambda b, p: (p, 0, 0, 0)),
                  pl.BlockSpec(memory_space=pltpu.SMEM)],
        out_specs=(o_spec,) * 2,
        out_shape=(o_shape,) * 2,
        compiler_params=_params(("arbitrary",) * 2),
        name="swa",
    )(src, src, src, bias, sinks)


def _online_update(scores, vts, carry):
    n_heads = len(scores)
    ms = []
    for hh in range(n_heads):
        m_new = carry[2 * hh]
        for s in scores[hh]:
            m_new = jnp.maximum(m_new, jnp.max(s, axis=1, keepdims=True))
        ms.append(m_new)
    new = []
    for hh in range(n_heads):
        m, acc = carry[2 * hh:2 * hh + 2]
        ps = [jnp.exp2(s - ms[hh]).astype(BF16) for s in scores[hh]]
        p = ps[0] if len(ps) == 1 else jnp.concatenate(ps, axis=1)
        acc = jnp.exp2(m - ms[hh]) * acc + _dot(p, vts[hh])
        new.extend((ms[hh], acc))
    return tuple(new)


def _flash_init(tq, n_heads=2):
    return (jnp.full((tq, 1), NEG, F32), jnp.zeros((tq, LANES), F32)) * n_heads


def _finish_pair(o_ref, carry, tq):
    low = lax.broadcasted_iota(jnp.int32, (tq, LANES), 1) < HEAD_DIM
    outs = [acc / acc[:, AUG0:AUG0 + 1] for acc in (carry[1], carry[3])]
    o_ref[...] = jnp.where(low, outs[0], pltpu.roll(outs[1], HEAD_DIM, axis=1))


def _moba_kernel(q_ref, k_ref, v_ref, bias_ref, shift_ref, o_ref, qaug_ref, *, n_blocks):
    i = pl.program_id(2)
    t, blk = FLASH_T, MOBA_BLOCK
    per = t // blk
    seq = n_blocks * blk

    @pl.when(i == 0)
    def _():
        blk_id = lax.broadcasted_iota(jnp.int32, (n_blocks, seq), 0)
        blk_f = blk_id.astype(F32)
        own = lax.broadcasted_iota(jnp.int32, (n_blocks, seq), 1) // blk
        for hh in range(2):
            q = q_ref[:, hh * LANES:(hh + 1) * LANES]
            kf = k_ref[:, hh * LANES:(hh + 1) * LANES].astype(F32)
            km = jnp.mean(kf.reshape(n_blocks, blk, LANES), axis=1)
            k1 = km.astype(BF16)
            r1 = km - k1.astype(F32)
            k2 = r1.astype(BF16)
            k3 = (r1 - k2.astype(F32)).astype(BF16)
            gate = _dot_nt(k1, q) + _dot_nt(k2, q) + _dot_nt(k3, q)

            avail = blk_id < own
            sel = jnp.zeros((n_blocks, seq), jnp.bool_)
            for _ in range(MOBA_TOPK):
                cur = jnp.where(avail, gate, -jnp.inf)
                top = jnp.max(cur, axis=0, keepdims=True)
                is_top = avail & (cur == top)
                first = jnp.min(jnp.where(is_top, blk_f, float(n_blocks)), axis=0, keepdims=True)
                pick = blk_f == first
                sel = sel | pick
                avail = avail & jnp.logical_not(pick)
            penalty = jnp.where(sel | (blk_id == own), 0.0, NEG).astype(BF16)
            placed = lax.dot_general(penalty, shift_ref[0:n_blocks, :], (((0,), (0,)), ((), ())),
                                     preferred_element_type=F32)
            qaug_ref[:, hh * LANES:(hh + 1) * LANES] = (q.astype(F32) + placed).astype(BF16)

    q0 = pl.multiple_of(i * t, t)
    q_aug = [qaug_ref[pl.ds(q0, t), hh * LANES:(hh + 1) * LANES] for hh in range(2)]

    def tile(n, q_rows, k_lo, k_hi, carry_rows):
        r0 = pl.multiple_of(n * t + k_lo * blk, blk)
        width = (k_hi - k_lo) * blk
        d0 = per * (i - n) + per - 1
        scores, vts = [], []
        for hh in range(2):
            kt = k_ref[pl.ds(r0, width), hh * LANES:(hh + 1) * LANES]
            s = _dot_nt(q_aug[hh][q_rows[0] * blk:q_rows[1] * blk], kt)
            rows = [jnp.concatenate([s[(a - q_rows[0]) * blk:(a - q_rows[0] + 1) * blk,
                                       (b - k_lo) * blk:(b - k_lo + 1) * blk] + bias_ref[hh, d0 + a - b]
                                     for b in range(k_lo, k_hi)], axis=1) for a in range(*q_rows)]
            scores.append([jnp.concatenate(rows, axis=0)])
            vts.append(v_ref[pl.ds(r0, width), hh * LANES:(hh + 1) * LANES])
        return _online_update(scores, vts, carry_rows)

    carry = lax.fori_loop(0, i, lambda n, c: tile(n, (0, per), 0, per, c), _flash_init(t))
    hp = per // 2
    carry = tile(i, (0, per), 0, hp, carry)
    low_rows = tile(i, (hp, per), hp, per, tuple(c[hp * blk:] for c in carry))
    carry = tuple(jnp.concatenate([c[:hp * blk], lo], axis=0) for c, lo in zip(carry, low_rows))
    _finish_pair(o_ref, carry, t)


def _moba(qb, kb, vb, bias, shift):
    B, S, _ = qb.shape
    nb = S // MOBA_BLOCK
    t = FLASH_T
    assert nb <= AUG0 and t % (2 * MOBA_BLOCK) == 0 and S % t == 0
    n_bias = nb + t // MOBA_BLOCK - 1
    return pl.pallas_call(
        functools.partial(_moba_kernel, n_blocks=nb),
        grid=(B, 2, S // t),
        in_specs=[pl.BlockSpec((None, S, 2 * LANES), lambda b, p, i: (b, 0, p)),
                  pl.BlockSpec((None, S, 2 * LANES), lambda b, p, i: (b, 0, p)),
                  pl.BlockSpec((None, S, 2 * LANES), lambda b, p, i: (b, 0, p)),
                  pl.BlockSpec((2, n_bias, MOBA_BLOCK, MOBA_BLOCK), lambda b, p, i: (p, 0, 0, 0)),
                  pl.BlockSpec((LANES, LANES), lambda b, p, i: (0, 0))],
        out_specs=pl.BlockSpec((None, t, LANES), lambda b, p, i: (b, i, p)),
        out_shape=jax.ShapeDtypeStruct((B, S, GROUP_W), F32),
        scratch_shapes=[pltpu.VMEM((S, 2 * LANES), BF16)],
        compiler_params=_params(("arbitrary",) * 3),
        name="moba",
    )(qb, kb, vb, bias, shift)


def _fox_kernel(q_ref, k_ref, v_ref, o_ref, *, tq, tk):
    i = pl.program_id(2)
    qs = [q_ref[:, hh * LANES:(hh + 1) * LANES] for hh in range(2)]

    def step(n, carry):
        r0 = pl.multiple_of(n * tk, tk)
        scores, vts = [], []
        for hh in range(2):
            kt = k_ref[pl.ds(r0, tk), hh * LANES:(hh + 1) * LANES]
            scores.append([_dot_nt(qs[hh], kt)])
            vts.append(v_ref[pl.ds(r0, tk), hh * LANES:(hh + 1) * LANES])
        return _online_update(scores, vts, carry)

    carry = lax.fori_loop(0, i, step, _flash_init(tq))
    half = tq // 2
    d0 = pl.multiple_of(i * tq, tq)

    def diag(q_lo, k0, carry_rows):
        rows = tq - q_lo
        row = lax.broadcasted_iota(jnp.int32, (rows, half), 0)
        col = lax.broadcasted_iota(jnp.int32, (rows, half), 1)
        scores, vts = [], []
        for hh in range(2):
            kt = k_ref[pl.ds(k0, half), hh * LANES:(hh + 1) * LANES]
            s = jnp.where(col <= row, _dot_nt(q_ref[q_lo:tq, hh * LANES:(hh + 1) * LANES], kt), NEG)
            scores.append([s])
            vts.append(v_ref[pl.ds(k0, half), hh * LANES:(hh + 1) * LANES])
        return _online_update(scores, vts, carry_rows)

    carry = diag(0, d0, carry)
    low_rows = diag(half, d0 + half, tuple(c[half:] for c in carry))
    carry = tuple(jnp.concatenate([c[:half], lo], axis=0) for c, lo in zip(carry, low_rows))
    _finish_pair(o_ref, carry, tq)


def _fox(qd, kd, vd):
    B, S, _ = qd.shape
    tq, tk = FOX_TQ, FOX_TK
    assert S % tk == 0 and tk == tq
    return pl.pallas_call(
        functools.partial(_fox_kernel, tq=tq, tk=tk),
        grid=(B, 2, S // tq),
        in_specs=[pl.BlockSpec((None, tq, 2 * LANES), lambda b, p, i: (b, i, p)),
                  pl.BlockSpec((None, S, 2 * LANES), lambda b, p, i: (b, 0, p)),
                  pl.BlockSpec((None, S, 2 * LANES), lambda b, p, i: (b, 0, p))],
        out_specs=pl.BlockSpec((None, tq, LANES), lambda b, p, i: (b, i, p)),
        out_shape=jax.ShapeDtypeStruct((B, S, GROUP_W), F32),
        compiler_params=_params(("arbitrary",) * 3),
        name="fox",
    )(qd, kd, vd)


def _outffn_kernel(x_ref, oa_ref, ob_ref, oc_ref, lc_ref, od_ref,
                   mg_ref, wo_ref, g2_ref, wg_ref, wu_ref, wd_ref, fg_ref, out_ref, *, final):
    mix_a = oa_ref[...]
    mix_c = oc_ref[...] / lc_ref[...]

    x1 = x_ref[...]
    for g, part in enumerate((mix_a, ob_ref[...], mix_c, od_ref[...])):
        c0 = g * GROUP_W
        normed = _rms(part, mg_ref[:, c0:c0 + GROUP_W]).astype(BF16)
        x1 = x1 + _dot(normed, wo_ref[c0:c0 + GROUP_W, :])

    h = _rms(x1, g2_ref[...]).astype(BF16)
    ffn = jnp.zeros_like(x1)
    for c in range(D_FF // FF_CHUNK):
        c0 = c * FF_CHUNK
        gate = _dot(h, wg_ref[:, c0:c0 + FF_CHUNK])
        up = _dot(h, wu_ref[:, c0:c0 + FF_CHUNK])
        act = (gate * (1.0 / (1.0 + jnp.exp(-gate))) * up).astype(BF16)
        ffn = ffn + _dot(act, wd_ref[c0:c0 + FF_CHUNK, :])
    x2 = x1 + ffn
    if final:
        x2 = _rms(x2, fg_ref[...])
    out_ref[...] = x2


def _outffn(x, oa, ob, c_parts, od, mg, wo, g2, wg, wu, wd, fg, *, layer, final):
    B, S, D = x.shape
    tm = OUT_TM
    tile = lambda b, t: (b, t, 0)
    const = lambda b, t: (0, 0)
    of_layer = lambda b, t: (layer, 0, 0)
    flat_spec = pl.BlockSpec((None, tm, GROUP_W), tile)
    c_spec = pl.BlockSpec((None, None, tm, GROUP_W), lambda b, t: (b, 0, t, 0))
    return pl.pallas_call(
        functools.partial(_outffn_kernel, final=final),
        grid=(B, S // tm),
        in_specs=[pl.BlockSpec((None, tm, D), tile), flat_spec, flat_spec, c_spec, c_spec, flat_spec,
                  _resident((None, 1, D), of_layer),
                  _resident((None, D, D), of_layer),
                  _resident((None, 1, D), of_layer),
                  _resident((None, D, D_FF), of_layer),
                  _resident((None, D, D_FF), of_layer),
                  _resident((None, D_FF, D), of_layer),
                  _resident((1, D), const)],
        out_specs=pl.BlockSpec((None, tm, D), tile),
        out_shape=jax.ShapeDtypeStruct((B, S, D), F32),
        compiler_params=_params(("arbitrary", "arbitrary")),
        name="outffn",
    )(x, oa, ob, c_parts[0], c_parts[1], od, mg, wo, g2, wg, wu, wd, fg)


def kernel(x, norm1_g, w_in, f_bias, sinks, mix_norm_g, w_out, norm2_g, w_gate, w_up, w_down, rel_bias, final_g):
    B, S, D = x.shape
    depth = w_in.shape[0]
    max_dil = max(dil for _, dil in DILATED_GROUPS)
    assert D == D_MODEL and S % (2 * BAND * max_dil) == 0 and w_gate.shape[-1] == D_FF
    assert [dil for _, dil in DILATED_GROUPS] == [1, 4, 16] and S % PROJ_TM == 0

    w_proj = _rearrange_w_in(w_in)
    wo_b, wg_b, wu_b, wd_b = (w.astype(BF16) for w in (w_out, w_gate, w_up, w_down))
    fb = jnp.pad(f_bias, ((0, 0), (0, LANES - N_HEADS)))[:, None, :]
    g1, mg, g2 = norm1_g[:, None, :], mix_norm_g[:, None, :], norm2_g[:, None, :]
    consts = _placement_constants()
    table_a = rel_bias[:, :N_HEADS]
    table_b = rel_bias[:, N_HEADS:2 * N_HEADS]
    table_c = rel_bias[:, 2 * N_HEADS:]
    bias_a = [_band_bias(table_a, dil, window // dil, f"dilated{dil}_bias") for window, dil in DILATED_GROUPS]
    bias_c = _band_bias(table_c, 1, SWA_WINDOW - 1, "swa_bias")
    bias_b = _moba_bias(table_b, S // MOBA_BLOCK)

    for l in range(depth):
        y, ya4, ya16, qb, kb, vb, qd, kd, vd = _inproj(x, g1, w_proj, fb, consts, l)
        y4 = y.reshape(B, 1, S, NY)
        o_a = _dilated_mix((y4, ya4, ya16), bias_a)
        c_parts = _swa(y4, bias_c, sinks[l])
        o_b = _moba(qb, kb, vb, bias_b, consts[4])
        o_d = _fox(qd, kd, vd)
        x = _outffn(x, o_a, o_b, c_parts, o_d, mg, wo_b, g2, wg_b, wu_b, wd_b, final_g[None],
                    layer=l, final=(l == depth - 1))
    return x
```

```python
import functools
import math

import numpy as np
import jax
import jax.numpy as jnp
from jax import lax
from jax.experimental import pallas as pl
from jax.experimental.pallas import tpu as pltpu

F32 = jnp.float32
BF16 = jnp.bfloat16

LANES = 128
VMEM_LIMIT_BYTES = 60 * 1024 * 1024

D_MODEL = 1024
HEAD_DIM = 64
N_HEADS = 4
GROUP_W = N_HEADS * HEAD_DIM
KV_HEADS_C = 2
DILATED_GROUPS = ((128, 1), (512, 4), (2048, 16))
BAND = 128
MOBA_BLOCK = 256
MOBA_TOPK = 3
SWA_WINDOW = 128
N_BUCKETS = 32
T5_MAX_DISTANCE = 2048
D_FF = 2816
FF_CHUNK = 256
EPS = 1e-6
NEG = -1e30
SCALE = HEAD_DIM ** -0.5

LOG2E = math.log2(math.e)
QA_B, KA_B, VA_B, QC_B, KC_B, VC_B = 0, 2, 4, 6, 8, 10
NY = 12 * LANES
A_COLS = 3 * GROUP_W
WIDE = N_HEADS * LANES
QB_OFF, KB_OFF, VB_OFF, QD_OFF, KD_OFF, VD_OFF = (NY + g * GROUP_W for g in range(6))
FD_OFF = NY + 6 * GROUP_W
N_PROJ = FD_OFF + LANES
AUG0 = HEAD_DIM

PROJ_TM = 512
OUT_TM = 1024
FLASH_T = 1024
FOX_TQ, FOX_TK = 1024, 1024
BANDED_UNROLL = 32
MIX_CHUNK = 512
BIAS_CHUNK = 32
UNBOUNDED_DISTANCE = 1 << 16


def _params(sem):
    return pltpu.CompilerParams(dimension_semantics=sem, vmem_limit_bytes=VMEM_LIMIT_BYTES)


def _resident(shape, index_map):
    return pl.BlockSpec(shape, index_map, pipeline_mode=pl.Buffered(1))


def _dot(a, b):
    return jnp.dot(a, b, preferred_element_type=F32)


def _dot_nt(a, b):
    return lax.dot_general(a, b, (((1,), (1,)), ((), ())), preferred_element_type=F32)


def _rms(x, g):
    return x * lax.rsqrt(jnp.mean(x * x, axis=-1, keepdims=True) + EPS) * g


def _rearrange_w_in(w):
    kv_w = KV_HEADS_C * HEAD_DIM
    sizes = [GROUP_W] * 6 + [GROUP_W, kv_w, kv_w] + [GROUP_W] * 3 + [N_HEADS]
    starts = np.concatenate([[0], np.cumsum(sizes)])
    assert w.shape[-1] == starts[-1]
    w = w.astype(BF16)
    qa, ka, va, qb, kb, vb, qc, kc, vc, qd, kd, vd, fd = (w[..., a:b] for a, b in zip(starts[:-1], starts[1:]))

    def per_query_head(kv):
        heads = [kv[..., h * HEAD_DIM:(h + 1) * HEAD_DIM] for h in range(KV_HEADS_C)]
        return [h for h in heads for _ in range(N_HEADS // KV_HEADS_C)]

    fd = jnp.pad(fd, [(0, 0)] * (w.ndim - 1) + [(0, LANES - N_HEADS)])
    out = jnp.concatenate([qa, ka, va, qc, *per_query_head(kc), *per_query_head(vc),
                           qb, kb, vb, qd, kd, vd, fd], axis=-1)
    assert out.shape[-1] == N_PROJ
    return out


def _t5_bucket_np(dist):
    n = np.maximum(dist, 0)
    max_exact = N_BUCKETS // 2
    nf = np.maximum(n, 1).astype(np.float32)
    large = max_exact + (np.log(nf / np.float32(max_exact)) / np.float32(math.log(T5_MAX_DISTANCE / max_exact))
                         * np.float32(N_BUCKETS - max_exact)).astype(np.int32)
    large = np.minimum(large, N_BUCKETS - 1)
    return np.where(n < max_exact, n, large).astype(np.int32)


def _bucket_thresholds():
    n = np.arange(UNBOUNDED_DISTANCE)
    bucket = _t5_bucket_np(n)
    assert (np.diff(bucket) >= 0).all() and bucket[-1] == N_BUCKETS - 1
    thr = np.searchsorted(bucket, np.arange(N_BUCKETS), side="left")
    assert (np.searchsorted(thr, n, side="right") - 1 == bucket).all()
    return tuple(int(t) for t in thr)


def _placement_constants():
    pq = np.zeros((LANES, WIDE), np.float32)
    pk = np.zeros((LANES, WIDE), np.float32)
    cq = np.zeros((1, WIDE), np.float32)
    ck = np.zeros((1, WIDE), np.float32)
    for h in range(N_HEADS):
        for part in range(3):
            pq[part * N_HEADS + h, h * LANES + AUG0 + part] = 1.0
            pk[part * N_HEADS + h, h * LANES + AUG0 + 3 + part] = -1.0
            ck[0, h * LANES + AUG0 + part] = 1.0
            cq[0, h * LANES + AUG0 + 3 + part] = 1.0
    shift = np.zeros((LANES, LANES), np.float32)
    for n in range(AUG0):
        shift[n, AUG0 + n] = 1.0
    return (jnp.asarray(pq, BF16), jnp.asarray(pk, BF16), jnp.asarray(cq), jnp.asarray(ck),
            jnp.asarray(shift, BF16))


def _bias_tiles_kernel(base_ref, table_ref, o_ref, *, rows, cols, step, max_dist, scale, thresholds, chunk):
    t = pl.program_id(0)
    base = base_ref[t]

    def body(c, carry):
        r0 = pl.multiple_of(c * chunk, chunk)
        i = lax.broadcasted_iota(jnp.int32, (chunk, cols), 0) + r0
        j = lax.broadcasted_iota(jnp.int32, (chunk, cols), 1)
        dist = base + i - j
        ok = (dist >= 0) & (dist <= max_dist)
        scaled = dist * step
        vals = [jnp.full((chunk, cols), table_ref[0, h] * scale, F32) for h in range(N_HEADS)]
        for b in range(1, N_BUCKETS):
            reached = scaled >= thresholds[b]
            vals = [jnp.where(reached, table_ref[b, h] * scale, v) for h, v in enumerate(vals)]
        for h in range(N_HEADS):
            o_ref[h, pl.ds(r0, chunk), :] = jnp.where(ok, vals[h], NEG)
        return carry

    d_min, d_max = base - (cols - 1), base + rows - 1
    masked = d_max < 0
    last_bucket = (d_min * step >= thresholds[N_BUCKETS - 1]) & (d_max <= max_dist)

    @pl.when(masked | last_bucket)
    def _():
        for h in range(N_HEADS):
            fill = jnp.where(masked, NEG, table_ref[N_BUCKETS - 1, h] * scale)
            o_ref[h] = jnp.full((rows, cols), fill, F32)

    @pl.when(jnp.logical_not(masked | last_bucket))
    def _():
        lax.fori_loop(0, rows // chunk, body, 0)


def _bias_tiles(table, bases, *, rows, cols, step, max_dist, name, scale=1.0):
    n_tiles = len(bases)
    return pl.pallas_call(
        functools.partial(_bias_tiles_kernel, rows=rows, cols=cols, step=step, max_dist=max_dist, scale=scale,
                          thresholds=_bucket_thresholds(), chunk=BIAS_CHUNK),
        grid=(n_tiles,),
        in_specs=[pl.BlockSpec(memory_space=pltpu.SMEM), pl.BlockSpec(memory_space=pltpu.SMEM)],
        out_specs=pl.BlockSpec((N_HEADS, None, rows, cols), lambda t: (0, t, 0, 0)),
        out_shape=jax.ShapeDtypeStruct((N_HEADS, n_tiles, rows, cols), F32),
        compiler_params=_params(("arbitrary",)),
        name=name,
    )(jnp.asarray(bases, jnp.int32), table)


def _band_bias(table, step, max_dist, name):
    return _bias_tiles(table, (BAND, 0), rows=BAND, cols=2 * BAND, step=step, max_dist=max_dist, name=name)


def _moba_bias(table, n_blocks):
    bases = tuple(MOBA_BLOCK * d for d in range(1 - FLASH_T // MOBA_BLOCK, n_blocks))
    return _bias_tiles(table, bases, rows=MOBA_BLOCK, cols=MOBA_BLOCK, step=1, max_dist=UNBOUNDED_DISTANCE,
                       name="moba_bias", scale=LOG2E)


def _widen(a, low):
    outs = []
    for p in range(GROUP_W // LANES):
        pair = a[:, p * LANES:(p + 1) * LANES]
        outs.append(jnp.where(low, pair, 0.0))
        outs.append(jnp.where(low, pltpu.roll(pair, HEAD_DIM, axis=1), 0.0))
    return jnp.concatenate(outs, axis=1)


def _inproj_kernel(x_ref, g_ref, w_ref, fb_ref, pq_ref, pk_ref, cq_ref, ck_ref,
                   y_ref, ya4_ref, ya16_ref, qb_ref, kb_ref, vb_ref, qd_ref, kd_ref, vd_ref,
                   carry_ref, fold_ref, *, tm):
    t = pl.program_id(1)

    @pl.when(t == 0)
    def _():
        carry_ref[...] = jnp.zeros_like(carry_ref)

    h = _rms(x_ref[...], g_ref[...]).astype(BF16)

    def mm(c0, width=GROUP_W):
        return _dot(h, w_ref[:, c0:c0 + width])

    f = mm(FD_OFF, LANES) + fb_ref[...]
    logf = jnp.minimum(f, 0.0) - jnp.log(1.0 + jnp.exp(-jnp.abs(f)))
    srow = lax.broadcasted_iota(jnp.int32, (tm, LANES), 0)
    c = logf
    k = 1
    while k < tm:
        c = c + jnp.where(srow >= k, pltpu.roll(c, k, axis=0), 0.0)
        k *= 2
    c = c + carry_ref[0:1, :]
    carry_ref[0:1, :] = c[tm - 1:tm, :]

    c2 = c * LOG2E
    c_hi = c2.astype(BF16)
    r1 = c2 - c_hi.astype(F32)
    c_mid = r1.astype(BF16)
    c_lo = (r1 - c_mid.astype(F32)).astype(BF16)
    head_lane = lax.broadcasted_iota(jnp.int32, (tm, LANES), 1) < N_HEADS
    pieces = [jnp.where(head_lane, piece.astype(F32), 0.0) for piece in (c_hi, c_mid, c_lo)]
    feats = (pieces[0] + pltpu.roll(pieces[1], N_HEADS, axis=1)
             + pltpu.roll(pieces[2], 2 * N_HEADS, axis=1)).astype(BF16)

    for grp in range(NY // GROUP_W):
        c0 = grp * GROUP_W
        a = mm(c0)
        if c0 in (QA_B * LANES, QC_B * LANES):
            a = a * SCALE
        y_ref[:, c0:c0 + GROUP_W] = a.astype(BF16)
        if c0 < A_COLS:
            fold_ref[2 * grp] = a[:, :LANES]
            fold_ref[2 * grp + 1] = a[:, LANES:]

    for dil, ref in ((4, ya4_ref), (16, ya16_ref)):
        for r in range(dil):
            for slab in range(A_COLS // LANES):
                piece = fold_ref[slab, pl.ds(r, tm // dil, stride=dil), :]
                ref[r, :, slab * LANES:(slab + 1) * LANES] = piece.astype(BF16)

    low = lax.broadcasted_iota(jnp.int32, (tm, LANES), 1) < HEAD_DIM
    lane = lax.broadcasted_iota(jnp.int32, (tm, WIDE), 1) % LANES
    row = lax.broadcasted_iota(jnp.int32, (tm, WIDE), 0)

    qb_ref[...] = _widen(mm(QB_OFF) * (SCALE * LOG2E), low).astype(BF16)
    vb_ref[...] = jnp.where(lane == AUG0, 1.0, _widen(mm(VB_OFF), low)).astype(BF16)
    vd_ref[...] = jnp.where(lane == AUG0, 1.0, _widen(mm(VD_OFF), low)).astype(BF16)
    blk = (t * tm + row) // MOBA_BLOCK
    kb_ref[...] = jnp.where(lane == AUG0 + blk, 1.0, _widen(mm(KB_OFF), low)).astype(BF16)

    qd = _widen(mm(QD_OFF) * (SCALE * LOG2E), low) + _dot(feats, pq_ref[...]) + cq_ref[...]
    kd = _widen(mm(KD_OFF), low) + _dot(feats, pk_ref[...]) + ck_ref[...]
    qd_ref[...] = qd.astype(BF16)
    kd_ref[...] = kd.astype(BF16)


def _inproj(x, g, w, fb, consts, layer):
    B, S, D = x.shape
    tm = PROJ_TM
    pq, pk, cq, ck, _ = consts
    const = lambda b, t: (0, 0)
    of_layer = lambda b, t: (layer, 0, 0)
    tile = lambda b, t: (b, t, 0)
    fold_tile = lambda b, t: (b, 0, t, 0)
    wide_shape = jax.ShapeDtypeStruct((B, S, WIDE), BF16)
    out_shape = (jax.ShapeDtypeStruct((B, S, NY), BF16),
                 jax.ShapeDtypeStruct((B, 4, S // 4, A_COLS), BF16),
                 jax.ShapeDtypeStruct((B, 16, S // 16, A_COLS), BF16),
                 ) + (wide_shape,) * 6
    wide_spec = pl.BlockSpec((None, tm, WIDE), tile)
    return pl.pallas_call(
        functools.partial(_inproj_kernel, tm=tm),
        grid=(B, S // tm),
        in_specs=[pl.BlockSpec((None, tm, D), tile),
                  _resident((None, 1, D), of_layer),
                  _resident((None, D, N_PROJ), of_layer),
                  _resident((None, 1, LANES), of_layer),
                  _resident((LANES, WIDE), const),
                  _resident((LANES, WIDE), const),
                  _resident((1, WIDE), const),
                  _resident((1, WIDE), const)],
        out_specs=(pl.BlockSpec((None, tm, NY), tile),
                   pl.BlockSpec((None, 4, tm // 4, A_COLS), fold_tile),
                   pl.BlockSpec((None, 16, tm // 16, A_COLS), fold_tile),
                   ) + (wide_spec,) * 6,
        out_shape=out_shape,
        scratch_shapes=[pltpu.VMEM((8, LANES), F32),
                        pltpu.VMEM((A_COLS // LANES, tm, LANES), F32)],
        compiler_params=_params(("arbitrary", "arbitrary")),
        name="inproj",
    )(x, g, w, fb, pq, pk, cq, ck)


def _banded_pass(q_ref, k_ref, v_ref, bias_ref, sinks, o_ref, m_ref, l_ref, *, dil, n_tiles):
    low = lax.broadcasted_iota(jnp.int32, (BAND, LANES), 1) < HEAD_DIM
    keep = (low.astype(F32).astype(BF16), (1.0 - low.astype(F32)).astype(BF16))

    def body(it, carry):
        r = it // n_tiles
        n = it % n_tiles
        r0 = pl.multiple_of(n * BAND, BAND)
        k0 = pl.multiple_of(jnp.maximum(r0 - BAND, 0), BAND)
        variant = jnp.where(n == 0, 1, 0)
        qt = q_ref[r, pl.ds(r0, BAND), :]
        kt = k_ref[r, pl.ds(k0, 2 * BAND), :]
        vt = v_ref[r, pl.ds(k0, 2 * BAND), :]
        accs, ms, ls = [], [], []
        for hh in range(2):
            s = _dot_nt(qt * keep[hh], kt) + bias_ref[hh, variant]
            m = jnp.max(s, axis=1, keepdims=True)
            if sinks is not None:
                m = jnp.maximum(m, sinks[hh])
            p = jnp.exp(s - m)
            l = jnp.sum(p, axis=1, keepdims=True)
            if sinks is not None:
                l = l + jnp.exp(sinks[hh] - m)
            accs.append(_dot(p.astype(BF16), vt))
            ms.append(jnp.broadcast_to(m, (BAND, LANES)))
            ls.append(jnp.broadcast_to(l, (BAND, LANES)))
        rows = pl.ds(r0, BAND) if dil == 1 else pl.ds(r0 * dil + r, BAND, stride=dil)
        o_ref[rows, :] = jnp.where(low, accs[0], accs[1])
        if m_ref is not None:
            m_ref[rows, :] = jnp.where(low, ms[0], ms[1])
        l_ref[rows, :] = jnp.where(low, ls[0], ls[1])
        return carry

    lax.fori_loop(0, dil * n_tiles, body, 0, unroll=BANDED_UNROLL)


def _banded_kernel(q_ref, k_ref, v_ref, bias_ref, sink_ref, o_ref, l_ref, *, dil, n_tiles):
    pair = pl.program_id(1)
    sinks = (sink_ref[2 * pair], sink_ref[2 * pair + 1])
    _banded_pass(q_ref, k_ref, v_ref, bias_ref, sinks, o_ref.at[0], None, l_ref.at[0], dil=dil, n_tiles=n_tiles)


def _dilated_mix_kernel(*refs, dils, seq):
    n = len(dils)
    qkv = [refs[3 * g:3 * g + 3] for g in range(n)]
    biases = refs[3 * n:4 * n]
    o_ref = refs[4 * n]
    results = refs[4 * n + 1]
    for g, dil in enumerate(dils):
        _banded_pass(*qkv[g], biases[g], None, results.at[g, 0], results.at[g, 1], results.at[g, 2], dil=dil,
                     n_tiles=seq // dil // BAND)

    chunk = MIX_CHUNK

    def combine(c, carry):
        rows = pl.ds(pl.multiple_of(c * chunk, chunk), chunk)
        m = functools.reduce(jnp.maximum, [results[g, 1, rows, :] for g in range(n)])
        es = [jnp.exp(results[g, 1, rows, :] - m) for g in range(n)]
        num = sum(e * results[g, 0, rows, :] for g, e in enumerate(es))
        den = sum(e * results[g, 2, rows, :] for g, e in enumerate(es))
        o_ref[rows, :] = num / den
        return carry

    lax.fori_loop(0, seq // chunk, combine, 0)


def _dilated_mix(srcs, biases):
    B = srcs[0].shape[0]
    dils = tuple(src.shape[1] for src in srcs)
    S = srcs[0].shape[1] * srcs[0].shape[2]
    col = lambda blk: (lambda b, p: (b, 0, 0, blk + p))
    in_specs, args = [], []
    for src, dil in zip(srcs, dils):
        for blk in (QA_B, KA_B, VA_B):
            in_specs.append(pl.BlockSpec((None, dil, S // dil, LANES), col(blk)))
            args.append(src)
    for bias in biases:
        in_specs.append(pl.BlockSpec((2, 2, BAND, 2 * BAND), lambda b, p: (p, 0, 0, 0)))
        args.append(bias)
    scratch = [pltpu.VMEM((len(dils), 3, S, LANES), F32)]
    return pl.pallas_call(
        functools.partial(_dilated_mix_kernel, dils=dils, seq=S),
        grid=(B, 2),
        in_specs=in_specs,
        out_specs=pl.BlockSpec((None, S, LANES), lambda b, p: (b, 0, p)),
        out_shape=jax.ShapeDtypeStruct((B, S, GROUP_W), F32),
        scratch_shapes=scratch,
        compiler_params=_params(("arbitrary",) * 2),
        name="dilated_mix",
    )(*args)


def _swa(src, bias, sinks):
    B, dil, L, _ = src.shape
    n_tiles = L // BAND
    assert n_tiles & (n_tiles - 1) == 0 and (dil * n_tiles) % BANDED_UNROLL == 0
    col = lambda blk: (lambda b, p: (b, 0, 0, blk + p))
    o_spec = pl.BlockSpec((None, dil, L, LANES), lambda b, p: (b, 0, 0, p))
    o_shape = jax.ShapeDtypeStruct((B, dil, L, GROUP_W), F32)
    return pl.pallas_call(
        functools.partial(_banded_kernel, dil=dil, n_tiles=n_tiles),
        grid=(B, 2),
        in_specs=[pl.BlockSpec((None, dil, L, LANES), col(QC_B)),
                  pl.BlockSpec((None, dil, L, LANES), col(KC_B)),
                  pl.BlockSpec((None, dil, L, LANES), col(VC_B)),
                  pl.BlockSpec((2, 2, BAND, 2 * BAND), lambda b, p: (p, 0, 0, 0)),
                  pl.BlockSpec(memory_space=pltpu.SMEM)],
        out_specs=(o_spec,) * 2,
        out_shape=(o_shape,) * 2,
        compiler_params=_params(("arbitrary",) * 2),
        name="swa",
    )(src, src, src, bias, sinks)


def _online_update(scores, vts, carry):
    n_heads = len(scores)
    ms = []
    for hh in range(n_heads):
        m_new = carry[2 * hh]
        for s in scores[hh]:
            m_new = jnp.maximum(m_new, jnp.max(s, axis=1, keepdims=True))
        ms.append(m_new)
    new = []
    for hh in range(n_heads):
        m, acc = carry[2 * hh:2 * hh + 2]
        ps = [jnp.exp2(s - ms[hh]).astype(BF16) for s in scores[hh]]
        p = ps[0] if len(ps) == 1 else jnp.concatenate(ps, axis=1)
        acc = jnp.exp2(m - ms[hh]) * acc + _dot(p, vts[hh])
        new.extend((ms[hh], acc))
    return tuple(new)


def _flash_init(tq, n_heads=2):
    return (jnp.full((tq, 1), NEG, F32), jnp.zeros((tq, LANES), F32)) * n_heads


def _finish_pair(o_ref, carry, tq):
    low = lax.broadcasted_iota(jnp.int32, (tq, LANES), 1) < HEAD_DIM
    outs = [acc / acc[:, AUG0:AUG0 + 1] for acc in (carry[1], carry[3])]
    o_ref[...] = jnp.where(low, outs[0], pltpu.roll(outs[1], HEAD_DIM, axis=1))


def _moba_kernel(q_ref, k_ref, v_ref, bias_ref, shift_ref, o_ref, qaug_ref, *, n_blocks):
    i = pl.program_id(2)
    t, blk = FLASH_T, MOBA_BLOCK
    per = t // blk
    seq = n_blocks * blk

    @pl.when(i == 0)
    def _():
        blk_id = lax.broadcasted_iota(jnp.int32, (n_blocks, seq), 0)
        blk_f = blk_id.astype(F32)
        own = lax.broadcasted_iota(jnp.int32, (n_blocks, seq), 1) // blk
        for hh in range(2):
            q = q_ref[:, hh * LANES:(hh + 1) * LANES]
            kf = k_ref[:, hh * LANES:(hh + 1) * LANES].astype(F32)
            km = jnp.mean(kf.reshape(n_blocks, blk, LANES), axis=1)
            k1 = km.astype(BF16)
            r1 = km - k1.astype(F32)
            k2 = r1.astype(BF16)
            k3 = (r1 - k2.astype(F32)).astype(BF16)
            gate = _dot_nt(k1, q) + _dot_nt(k2, q) + _dot_nt(k3, q)

            avail = blk_id < own
            sel = jnp.zeros((n_blocks, seq), jnp.bool_)
            for _ in range(MOBA_TOPK):
                cur = jnp.where(avail, gate, -jnp.inf)
                top = jnp.max(cur, axis=0, keepdims=True)
                is_top = avail & (cur == top)
                first = jnp.min(jnp.where(is_top, blk_f, float(n_blocks)), axis=0, keepdims=True)
                pick = blk_f == first
                sel = sel | pick
                avail = avail & jnp.logical_not(pick)
            penalty = jnp.where(sel | (blk_id == own), 0.0, NEG).astype(BF16)
            placed = lax.dot_general(penalty, shift_ref[0:n_blocks, :], (((0,), (0,)), ((), ())),
                                     preferred_element_type=F32)
            qaug_ref[:, hh * LANES:(hh + 1) * LANES] = (q.astype(F32) + placed).astype(BF16)

    q0 = pl.multiple_of(i * t, t)
    q_aug = [qaug_ref[pl.ds(q0, t), hh * LANES:(hh + 1) * LANES] for hh in range(2)]

    def tile(n, q_rows, k_lo, k_hi, carry_rows):
        r0 = pl.multiple_of(n * t + k_lo * blk, blk)
        width = (k_hi - k_lo) * blk
        d0 = per * (i - n) + per - 1
        scores, vts = [], []
        for hh in range(2):
            kt = k_ref[pl.ds(r0, width), hh * LANES:(hh + 1) * LANES]
            s = _dot_nt(q_aug[hh][q_rows[0] * blk:q_rows[1] * blk], kt)
            rows = [jnp.concatenate([s[(a - q_rows[0]) * blk:(a - q_rows[0] + 1) * blk,
                                       (b - k_lo) * blk:(b - k_lo + 1) * blk] + bias_ref[hh, d0 + a - b]
                                     for b in range(k_lo, k_hi)], axis=1) for a in range(*q_rows)]
            scores.append([jnp.concatenate(rows, axis=0)])
            vts.append(v_ref[pl.ds(r0, width), hh * LANES:(hh + 1) * LANES])
        return _online_update(scores, vts, carry_rows)

    carry = lax.fori_loop(0, i, lambda n, c: tile(n, (0, per), 0, per, c), _flash_init(t))
    hp = per // 2
    carry = tile(i, (0, per), 0, hp, carry)
    low_rows = tile(i, (hp, per), hp, per, tuple(c[hp * blk:] for c in carry))
    carry = tuple(jnp.concatenate([c[:hp * blk], lo], axis=0) for c, lo in zip(carry, low_rows))
    _finish_pair(o_ref, carry, t)


def _moba(qb, kb, vb, bias, shift):
    B, S, _ = qb.shape
    nb = S // MOBA_BLOCK
    t = FLASH_T
    assert nb <= AUG0 and t % (2 * MOBA_BLOCK) == 0 and S % t == 0
    n_bias = nb + t // MOBA_BLOCK - 1
    return pl.pallas_call(
        functools.partial(_moba_kernel, n_blocks=nb),
        grid=(B, 2, S // t),
        in_specs=[pl.BlockSpec((None, S, 2 * LANES), lambda b, p, i: (b, 0, p)),
                  pl.BlockSpec((None, S, 2 * LANES), lambda b, p, i: (b, 0, p)),
                  pl.BlockSpec((None, S, 2 * LANES), lambda b, p, i: (b, 0, p)),
                  pl.BlockSpec((2, n_bias, MOBA_BLOCK, MOBA_BLOCK), lambda b, p, i: (p, 0, 0, 0)),
                  pl.BlockSpec((LANES, LANES), lambda b, p, i: (0, 0))],
        out_specs=pl.BlockSpec((None, t, LANES), lambda b, p, i: (b, i, p)),
        out_shape=jax.ShapeDtypeStruct((B, S, GROUP_W), F32),
        scratch_shapes=[pltpu.VMEM((S, 2 * LANES), BF16)],
        compiler_params=_params(("arbitrary",) * 3),
        name="moba",
    )(qb, kb, vb, bias, shift)


def _fox_kernel(q_ref, k_ref, v_ref, o_ref, *, tq, tk):
    i = pl.program_id(2)
    qs = [q_ref[:, hh * LANES:(hh + 1) * LANES] for hh in range(2)]

    def step(n, carry):
        r0 = pl.multiple_of(n * tk, tk)
        scores, vts = [], []
        for hh in range(2):
            kt = k_ref[pl.ds(r0, tk), hh * LANES:(hh + 1) * LANES]
            scores.append([_dot_nt(qs[hh], kt)])
            vts.append(v_ref[pl.ds(r0, tk), hh * LANES:(hh + 1) * LANES])
        return _online_update(scores, vts, carry)

    carry = lax.fori_loop(0, i, step, _flash_init(tq))
    half = tq // 2
    d0 = pl.multiple_of(i * tq, tq)

    def diag(q_lo, k0, carry_rows):
        rows = tq - q_lo
        row = lax.broadcasted_iota(jnp.int32, (rows, half), 0)
        col = lax.broadcasted_iota(jnp.int32, (rows, half), 1)
        scores, vts = [], []
        for hh in range(2):
            kt = k_ref[pl.ds(k0, half), hh * LANES:(hh + 1) * LANES]
            s = jnp.where(col <= row, _dot_nt(q_ref[q_lo:tq, hh * LANES:(hh + 1) * LANES], kt), NEG)
            scores.append([s])
            vts.append(v_ref[pl.ds(k0, half), hh * LANES:(hh + 1) * LANES])
        return _online_update(scores, vts, carry_rows)

    carry = diag(0, d0, carry)
    low_rows = diag(half, d0 + half, tuple(c[half:] for c in carry))
    carry = tuple(jnp.concatenate([c[:half], lo], axis=0) for c, lo in zip(carry, low_rows))
    _finish_pair(o_ref, carry, tq)


def _fox(qd, kd, vd):
    B, S, _ = qd.shape
    tq, tk = FOX_TQ, FOX_TK
    assert S % tk == 0 and tk == tq
    return pl.pallas_call(
        functools.partial(_fox_kernel, tq=tq, tk=tk),
        grid=(B, 2, S // tq),
        in_specs=[pl.BlockSpec((None, tq, 2 * LANES), lambda b, p, i: (b, i, p)),
                  pl.BlockSpec((None, S, 2 * LANES), lambda b, p, i: (b, 0, p)),
                  pl.BlockSpec((None, S, 2 * LANES), lambda b, p, i: (b, 0, p))],
        out_specs=pl.BlockSpec((None, tq, LANES), lambda b, p, i: (b, i, p)),
        out_shape=jax.ShapeDtypeStruct((B, S, GROUP_W), F32),
        compiler_params=_params(("arbitrary",) * 3),
        name="fox",
    )(qd, kd, vd)


def _outffn_kernel(x_ref, oa_ref, ob_ref, oc_ref, lc_ref, od_ref,
                   mg_ref, wo_ref, g2_ref, wg_ref, wu_ref, wd_ref, fg_ref, out_ref, *, final):
    mix_a = oa_ref[...]
    mix_c = oc_ref[...] / lc_ref[...]

    x1 = x_ref[...]
    for g, part in enumerate((mix_a, ob_ref[...], mix_c, od_ref[...])):
        c0 = g * GROUP_W
        normed = _rms(part, mg_ref[:, c0:c0 + GROUP_W]).astype(BF16)
        x1 = x1 + _dot(normed, wo_ref[c0:c0 + GROUP_W, :])

    h = _rms(x1, g2_ref[...]).astype(BF16)
    ffn = jnp.zeros_like(x1)
    for c in range(D_FF // FF_CHUNK):
        c0 = c * FF_CHUNK
        gate = _dot(h, wg_ref[:, c0:c0 + FF_CHUNK])
        up = _dot(h, wu_ref[:, c0:c0 + FF_CHUNK])
        act = (gate * (1.0 / (1.0 + jnp.exp(-gate))) * up).astype(BF16)
        ffn = ffn + _dot(act, wd_ref[c0:c0 + FF_CHUNK, :])
    x2 = x1 + ffn
    if final:
        x2 = _rms(x2, fg_ref[...])
    out_ref[...] = x2


def _outffn(x, oa, ob, c_parts, od, mg, wo, g2, wg, wu, wd, fg, *, layer, final):
    B, S, D = x.shape
    tm = OUT_TM
    tile = lambda b, t: (b, t, 0)
    const = lambda b, t: (0, 0)
    of_layer = lambda b, t: (layer, 0, 0)
    flat_spec = pl.BlockSpec((None, tm, GROUP_W), tile)
    c_spec = pl.BlockSpec((None, None, tm, GROUP_W), lambda b, t: (b, 0, t, 0))
    return pl.pallas_call(
        functools.partial(_outffn_kernel, final=final),
        grid=(B, S // tm),
        in_specs=[pl.BlockSpec((None, tm, D), tile), flat_spec, flat_spec, c_spec, c_spec, flat_spec,
                  _resident((None, 1, D), of_layer),
                  _resident((None, D, D), of_layer),
                  _resident((None, 1, D), of_layer),
                  _resident((None, D, D_FF), of_layer),
                  _resident((None, D, D_FF), of_layer),
                  _resident((None, D_FF, D), of_layer),
                  _resident((1, D), const)],
        out_specs=pl.BlockSpec((None, tm, D), tile),
        out_shape=jax.ShapeDtypeStruct((B, S, D), F32),
        compiler_params=_params(("arbitrary", "arbitrary")),
        name="outffn",
    )(x, oa, ob, c_parts[0], c_parts[1], od, mg, wo, g2, wg, wu, wd, fg)


def kernel(x, norm1_g, w_in, f_bias, sinks, mix_norm_g, w_out, norm2_g, w_gate, w_up, w_down, rel_bias, final_g):
    B, S, D = x.shape
    depth = w_in.shape[0]
    max_dil = max(dil for _, dil in DILATED_GROUPS)
    assert D == D_MODEL and S % (2 * BAND * max_dil) == 0 and w_gate.shape[-1] == D_FF
    assert [dil for _, dil in DILATED_GROUPS] == [1, 4, 16] and S % PROJ_TM == 0

    w_proj = _rearrange_w_in(w_in)
    wo_b, wg_b, wu_b, wd_b = (w.astype(BF16) for w in (w_out, w_gate, w_up, w_down))
    fb = jnp.pad(f_bias, ((0, 0), (0, LANES - N_HEADS)))[:, None, :]
    g1, mg, g2 = norm1_g[:, None, :], mix_norm_g[:, None, :], norm2_g[:, None, :]
    consts = _placement_constants()
    table_a = rel_bias[:, :N_HEADS]
    table_b = rel_bias[:, N_HEADS:2 * N_HEADS]
    table_c = rel_bias[:, 2 * N_HEADS:]
    bias_a = [_band_bias(table_a, dil, window // dil, f"dilated{dil}_bias") for window, dil in DILATED_GROUPS]
    bias_c = _band_bias(table_c, 1, SWA_WINDOW - 1, "swa_bias")
    bias_b = _moba_bias(table_b, S // MOBA_BLOCK)

    for l in range(depth):
        y, ya4, ya16, qb, kb, vb, qd, kd, vd = _inproj(x, g1, w_proj, fb, consts, l)
        y4 = y.reshape(B, 1, S, NY)
        o_a = _dilated_mix((y4, ya4, ya16), bias_a)
        c_parts = _swa(y4, bias_c, sinks[l])
        o_b = _moba(qb, kb, vb, bias_b, consts[4])
        o_d = _fox(qd, kd, vd)
        x = _outffn(x, o_a, o_b, c_parts, o_d, mg, wo_b, g2, wg_b, wu_b, wd_b, final_g[None],
                    layer=l, final=(l == depth - 1))
    return x
```

```python
import functools
import math

import numpy as np
import jax
import jax.numpy as jnp
from jax import lax
from jax.experimental import pallas as pl
from jax.experimental.pallas import tpu as pltpu

F32 = jnp.float32
BF16 = jnp.bfloat16

LANES = 128
VMEM_LIMIT_BYTES = 60 * 1024 * 1024

D_MODEL = 1024
HEAD_DIM = 64
N_HEADS = 4
GROUP_W = N_HEADS * HEAD_DIM
KV_HEADS_C = 2
DILATED_GROUPS = ((128, 1), (512, 4), (2048, 16))
BAND = 128
MOBA_BLOCK = 256
MOBA_TOPK = 3
SWA_WINDOW = 128
N_BUCKETS = 32
T5_MAX_DISTANCE = 2048
D_FF = 2816
FF_CHUNK = 256
EPS = 1e-6
NEG = -1e30
SCALE = HEAD_DIM ** -0.5

LOG2E = math.log2(math.e)
QA_B, KA_B, VA_B, QC_B, KC_B, VC_B = 0, 2, 4, 6, 8, 10
NY = 12 * LANES
A_COLS = 3 * GROUP_W
WIDE = N_HEADS * LANES
QB_OFF, KB_OFF, VB_OFF, QD_OFF, KD_OFF, VD_OFF = (NY + g * GROUP_W for g in range(6))
FD_OFF = NY + 6 * GROUP_W
N_PROJ = FD_OFF + LANES
AUG0 = HEAD_DIM

PROJ_TM = 512
OUT_TM = 1024
FLASH_T = 1024
FOX_TQ, FOX_TK = 1024, 1024
BANDED_UNROLL = 32
MIX_CHUNK = 512
BIAS_CHUNK = 32
UNBOUNDED_DISTANCE = 1 << 16


def _params(sem):
    return pltpu.CompilerParams(dimension_semantics=sem, vmem_limit_bytes=VMEM_LIMIT_BYTES)


def _resident(shape, index_map):
    return pl.BlockSpec(shape, index_map, pipeline_mode=pl.Buffered(1))


def _dot(a, b):
    return jnp.dot(a, b, preferred_element_type=F32)


def _dot_nt(a, b):
    return lax.dot_general(a, b, (((1,), (1,)), ((), ())), preferred_element_type=F32)


def _rms(x, g):
    return x * lax.rsqrt(jnp.mean(x * x, axis=-1, keepdims=True) + EPS) * g


def _rearrange_w_in(w):
    kv_w = KV_HEADS_C * HEAD_DIM
    sizes = [GROUP_W] * 6 + [GROUP_W, kv_w, kv_w] + [GROUP_W] * 3 + [N_HEADS]
    starts = np.concatenate([[0], np.cumsum(sizes)])
    assert w.shape[-1] == starts[-1]
    w = w.astype(BF16)
    qa, ka, va, qb, kb, vb, qc, kc, vc, qd, kd, vd, fd = (w[..., a:b] for a, b in zip(starts[:-1], starts[1:]))

    def per_query_head(kv):
        heads = [kv[..., h * HEAD_DIM:(h + 1) * HEAD_DIM] for h in range(KV_HEADS_C)]
        return [h for h in heads for _ in range(N_HEADS // KV_HEADS_C)]

    fd = jnp.pad(fd, [(0, 0)] * (w.ndim - 1) + [(0, LANES - N_HEADS)])
    out = jnp.concatenate([qa, ka, va, qc, *per_query_head(kc), *per_query_head(vc),
                           qb, kb, vb, qd, kd, vd, fd], axis=-1)
    assert out.shape[-1] == N_PROJ
    return out


def _t5_bucket_np(dist):
    n = np.maximum(dist, 0)
    max_exact = N_BUCKETS // 2
    nf = np.maximum(n, 1).astype(np.float32)
    large = max_exact + (np.log(nf / np.float32(max_exact)) / np.float32(math.log(T5_MAX_DISTANCE / max_exact))
                         * np.float32(N_BUCKETS - max_exact)).astype(np.int32)
    large = np.minimum(large, N_BUCKETS - 1)
    return np.where(n < max_exact, n, large).astype(np.int32)


def _bucket_thresholds():
    n = np.arange(UNBOUNDED_DISTANCE)
    bucket = _t5_bucket_np(n)
    assert (np.diff(bucket) >= 0).all() and bucket[-1] == N_BUCKETS - 1
    thr = np.searchsorted(bucket, np.arange(N_BUCKETS), side="left")
    assert (np.searchsorted(thr, n, side="right") - 1 == bucket).all()
    return tuple(int(t) for t in thr)


def _placement_constants():
    pq = np.zeros((LANES, WIDE), np.float32)
    pk = np.zeros((LANES, WIDE), np.float32)
    cq = np.zeros((1, WIDE), np.float32)
    ck = np.zeros((1, WIDE), np.float32)
    for h in range(N_HEADS):
        for part in range(3):
            pq[part * N_HEADS + h, h * LANES + AUG0 + part] = 1.0
            pk[part * N_HEADS + h, h * LANES + AUG0 + 3 + part] = -1.0
            ck[0, h * LANES + AUG0 + part] = 1.0
            cq[0, h * LANES + AUG0 + 3 + part] = 1.0
    shift = np.zeros((LANES, LANES), np.float32)
    for n in range(AUG0):
        shift[n, AUG0 + n] = 1.0
    return (jnp.asarray(pq, BF16), jnp.asarray(pk, BF16), jnp.asarray(cq), jnp.asarray(ck),
            jnp.asarray(shift, BF16))


def _bias_tiles_kernel(base_ref, table_ref, o_ref, *, rows, cols, step, max_dist, scale, thresholds, chunk):
    t = pl.program_id(0)
    base = base_ref[t]

    def body(c, carry):
        r0 = pl.multiple_of(c * chunk, chunk)
        i = lax.broadcasted_iota(jnp.int32, (chunk, cols), 0) + r0
        j = lax.broadcasted_iota(jnp.int32, (chunk, cols), 1)
        dist = base + i - j
        ok = (dist >= 0) & (dist <= max_dist)
        scaled = dist * step
        vals = [jnp.full((chunk, cols), table_ref[0, h] * scale, F32) for h in range(N_HEADS)]
        for b in range(1, N_BUCKETS):
            reached = scaled >= thresholds[b]
            vals = [jnp.where(reached, table_ref[b, h] * scale, v) for h, v in enumerate(vals)]
        for h in range(N_HEADS):
            o_ref[h, pl.ds(r0, chunk), :] = jnp.where(ok, vals[h], NEG)
        return carry

    d_min, d_max = base - (cols - 1), base + rows - 1
    masked = d_max < 0
    last_bucket = (d_min * step >= thresholds[N_BUCKETS - 1]) & (d_max <= max_dist)

    @pl.when(masked | last_bucket)
    def _():
        for h in range(N_HEADS):
            fill = jnp.where(masked, NEG, table_ref[N_BUCKETS - 1, h] * scale)
            o_ref[h] = jnp.full((rows, cols), fill, F32)

    @pl.when(jnp.logical_not(masked | last_bucket))
    def _():
        lax.fori_loop(0, rows // chunk, body, 0)


def _bias_tiles(table, bases, *, rows, cols, step, max_dist, name, scale=1.0):
    n_tiles = len(bases)
    return pl.pallas_call(
        functools.partial(_bias_tiles_kernel, rows=rows, cols=cols, step=step, max_dist=max_dist, scale=scale,
                          thresholds=_bucket_thresholds(), chunk=BIAS_CHUNK),
        grid=(n_tiles,),
        in_specs=[pl.BlockSpec(memory_space=pltpu.SMEM), pl.BlockSpec(memory_space=pltpu.SMEM)],
        out_specs=pl.BlockSpec((N_HEADS, None, rows, cols), lambda t: (0, t, 0, 0)),
        out_shape=jax.ShapeDtypeStruct((N_HEADS, n_tiles, rows, cols), F32),
        compiler_params=_params(("arbitrary",)),
        name=name,
    )(jnp.asarray(bases, jnp.int32), table)


def _band_bias(table, step, max_dist, name):
    return _bias_tiles(table, (BAND, 0), rows=BAND, cols=2 * BAND, step=step, max_dist=max_dist, name=name)


def _moba_bias(table, n_blocks):
    bases = tuple(MOBA_BLOCK * d for d in range(1 - FLASH_T // MOBA_BLOCK, n_blocks))
    return _bias_tiles(table, bases, rows=MOBA_BLOCK, cols=MOBA_BLOCK, step=1, max_dist=UNBOUNDED_DISTANCE,
                       name="moba_bias", scale=LOG2E)


def _widen(a, low):
    outs = []
    for p in range(GROUP_W // LANES):
        pair = a[:, p * LANES:(p + 1) * LANES]
        outs.append(jnp.where(low, pair, 0.0))
        outs.append(jnp.where(low, pltpu.roll(pair, HEAD_DIM, axis=1), 0.0))
    return jnp.concatenate(outs, axis=1)


def _inproj_kernel(x_ref, g_ref, w_ref, fb_ref, pq_ref, pk_ref, cq_ref, ck_ref,
                   y_ref, ya4_ref, ya16_ref, qb_ref, kb_ref, vb_ref, qd_ref, kd_ref, vd_ref,
                   carry_ref, fold_ref, *, tm):
    t = pl.program_id(1)

    @pl.when(t == 0)
    def _():
        carry_ref[...] = jnp.zeros_like(carry_ref)

    h = _rms(x_ref[...], g_ref[...]).astype(BF16)

    def mm(c0, width=GROUP_W):
        return _dot(h, w_ref[:, c0:c0 + width])

    f = mm(FD_OFF, LANES) + fb_ref[...]
    logf = jnp.minimum(f, 0.0) - jnp.log(1.0 + jnp.exp(-jnp.abs(f)))
    srow = lax.broadcasted_iota(jnp.int32, (tm, LANES), 0)
    c = logf
    k = 1
    while k < tm:
        c = c + jnp.where(srow >= k, pltpu.roll(c, k, axis=0), 0.0)
        k *= 2
    c = c + carry_ref[0:1, :]
    carry_ref[0:1, :] = c[tm - 1:tm, :]

    c2 = c * LOG2E
    c_hi = c2.astype(BF16)
    r1 = c2 - c_hi.astype(F32)
    c_mid = r1.astype(BF16)
    c_lo = (r1 - c_mid.astype(F32)).astype(BF16)
    head_lane = lax.broadcasted_iota(jnp.int32, (tm, LANES), 1) < N_HEADS
    pieces = [jnp.where(head_lane, piece.astype(F32), 0.0) for piece in (c_hi, c_mid, c_lo)]
    feats = (pieces[0] + pltpu.roll(pieces[1], N_HEADS, axis=1)
             + pltpu.roll(pieces[2], 2 * N_HEADS, axis=1)).astype(BF16)

    for grp in range(NY // GROUP_W):
        c0 = grp * GROUP_W
        a = mm(c0)
        if c0 in (QA_B * LANES, QC_B * LANES):
            a = a * SCALE
        y_ref[:, c0:c0 + GROUP_W] = a.astype(BF16)
        if c0 < A_COLS:
            fold_ref[2 * grp] = a[:, :LANES]
            fold_ref[2 * grp + 1] = a[:, LANES:]

    for dil, ref in ((4, ya4_ref), (16, ya16_ref)):
        for r in range(dil):
            for slab in range(A_COLS // LANES):
                piece = fold_ref[slab, pl.ds(r, tm // dil, stride=dil), :]
                ref[r, :, slab * LANES:(slab + 1) * LANES] = piece.astype(BF16)

    low = lax.broadcasted_iota(jnp.int32, (tm, LANES), 1) < HEAD_DIM
    lane = lax.broadcasted_iota(jnp.int32, (tm, WIDE), 1) % LANES
    row = lax.broadcasted_iota(jnp.int32, (tm, WIDE), 0)

    qb_ref[...] = _widen(mm(QB_OFF) * (SCALE * LOG2E), low).astype(BF16)
    vb_ref[...] = jnp.where(lane == AUG0, 1.0, _widen(mm(VB_OFF), low)).astype(BF16)
    vd_ref[...] = jnp.where(lane == AUG0, 1.0, _widen(mm(VD_OFF), low)).astype(BF16)
    blk = (t * tm + row) // MOBA_BLOCK
    kb_ref[...] = jnp.where(lane == AUG0 + blk, 1.0, _widen(mm(KB_OFF), low)).astype(BF16)

    qd = _widen(mm(QD_OFF) * (SCALE * LOG2E), low) + _dot(feats, pq_ref[...]) + cq_ref[...]
    kd = _widen(mm(KD_OFF), low) + _dot(feats, pk_ref[...]) + ck_ref[...]
    qd_ref[...] = qd.astype(BF16)
    kd_ref[...] = kd.astype(BF16)


def _inproj(x, g, w, fb, consts, layer):
    B, S, D = x.shape
    tm = PROJ_TM
    pq, pk, cq, ck, _ = consts
    const = lambda b, t: (0, 0)
    of_layer = lambda b, t: (layer, 0, 0)
    tile = lambda b, t: (b, t, 0)
    fold_tile = lambda b, t: (b, 0, t, 0)
    wide_shape = jax.ShapeDtypeStruct((B, S, WIDE), BF16)
    out_shape = (jax.ShapeDtypeStruct((B, S, NY), BF16),
                 jax.ShapeDtypeStruct((B, 4, S // 4, A_COLS), BF16),
                 jax.ShapeDtypeStruct((B, 16, S // 16, A_COLS), BF16),
                 ) + (wide_shape,) * 6
    wide_spec = pl.BlockSpec((None, tm, WIDE), tile)
    return pl.pallas_call(
        functools.partial(_inproj_kernel, tm=tm),
        grid=(B, S // tm),
        in_specs=[pl.BlockSpec((None, tm, D), tile),
                  _resident((None, 1, D), of_layer),
                  _resident((None, D, N_PROJ), of_layer),
                  _resident((None, 1, LANES), of_layer),
                  _resident((LANES, WIDE), const),
                  _resident((LANES, WIDE), const),
                  _resident((1, WIDE), const),
                  _resident((1, WIDE), const)],
        out_specs=(pl.BlockSpec((None, tm, NY), tile),
                   pl.BlockSpec((None, 4, tm // 4, A_COLS), fold_tile),
                   pl.BlockSpec((None, 16, tm // 16, A_COLS), fold_tile),
                   ) + (wide_spec,) * 6,
        out_shape=out_shape,
        scratch_shapes=[pltpu.VMEM((8, LANES), F32),
                        pltpu.VMEM((A_COLS // LANES, tm, LANES), F32)],
        compiler_params=_params(("arbitrary", "arbitrary")),
        name="inproj",
    )(x, g, w, fb, pq, pk, cq, ck)


def _banded_pass(q_ref, k_ref, v_ref, bias_ref, sinks, o_ref, m_ref, l_ref, *, dil, n_tiles):
    low = lax.broadcasted_iota(jnp.int32, (BAND, LANES), 1) < HEAD_DIM
    keep = (low.astype(F32).astype(BF16), (1.0 - low.astype(F32)).astype(BF16))

    def body(it, carry):
        r = it // n_tiles
        n = it % n_tiles
        r0 = pl.multiple_of(n * BAND, BAND)
        k0 = pl.multiple_of(jnp.maximum(r0 - BAND, 0), BAND)
        variant = jnp.where(n == 0, 1, 0)
        qt = q_ref[r, pl.ds(r0, BAND), :]
        kt = k_ref[r, pl.ds(k0, 2 * BAND), :]
        vt = v_ref[r, pl.ds(k0, 2 * BAND), :]
        accs, ms, ls = [], [], []
        for hh in range(2):
            s = _dot_nt(qt * keep[hh], kt) + bias_ref[hh, variant]
            m = jnp.max(s, axis=1, keepdims=True)
            if sinks is not None:
                m = jnp.maximum(m, sinks[hh])
            p = jnp.exp(s - m)
            l = jnp.sum(p, axis=1, keepdims=True)
            if sinks is not None:
                l = l + jnp.exp(sinks[hh] - m)
            accs.append(_dot(p.astype(BF16), vt))
            ms.append(jnp.broadcast_to(m, (BAND, LANES)))
            ls.append(jnp.broadcast_to(l, (BAND, LANES)))
        rows = pl.ds(r0, BAND) if dil == 1 else pl.ds(r0 * dil + r, BAND, stride=dil)
        o_ref[rows, :] = jnp.where(low, accs[0], accs[1])
        if m_ref is not None:
            m_ref[rows, :] = jnp.where(low, ms[0], ms[1])
        l_ref[rows, :] = jnp.where(low, ls[0], ls[1])
        return carry

    lax.fori_loop(0, dil * n_tiles, body, 0, unroll=BANDED_UNROLL)


def _banded_kernel(q_ref, k_ref, v_ref, bias_ref, sink_ref, o_ref, l_ref, *, dil, n_tiles):
    pair = pl.program_id(1)
    sinks = (sink_ref[2 * pair], sink_ref[2 * pair + 1])
    _banded_pass(q_ref, k_ref, v_ref, bias_ref, sinks, o_ref.at[0], None, l_ref.at[0], dil=dil, n_tiles=n_tiles)


def _dilated_mix_kernel(*refs, dils, seq):
    n = len(dils)
    qkv = [refs[3 * g:3 * g + 3] for g in range(n)]
    biases = refs[3 * n:4 * n]
    o_ref = refs[4 * n]
    results = refs[4 * n + 1]
    for g, dil in enumerate(dils):
        _banded_pass(*qkv[g], biases[g], None, results.at[g, 0], results.at[g, 1], results.at[g, 2], dil=dil,
                     n_tiles=seq // dil // BAND)

    chunk = MIX_CHUNK

    def combine(c, carry):
        rows = pl.ds(pl.multiple_of(c * chunk, chunk), chunk)
        m = functools.reduce(jnp.maximum, [results[g, 1, rows, :] for g in range(n)])
        es = [jnp.exp(results[g, 1, rows, :] - m) for g in range(n)]
        num = sum(e * results[g, 0, rows, :] for g, e in enumerate(es))
        den = sum(e * results[g, 2, rows, :] for g, e in enumerate(es))
        o_ref[rows, :] = num / den
        return carry

    lax.fori_loop(0, seq // chunk, combine, 0, unroll=True)


def _dilated_mix(srcs, biases):
    B = srcs[0].shape[0]
    dils = tuple(src.shape[1] for src in srcs)
    S = srcs[0].shape[1] * srcs[0].shape[2]
    col = lambda blk: (lambda b, p: (b, 0, 0, blk + p))
    in_specs, args = [], []
    for src, dil in zip(srcs, dils):
        for blk in (QA_B, KA_B, VA_B):
            in_specs.append(pl.BlockSpec((None, dil, S // dil, LANES), col(blk)))
            args.append(src)
    for bias in biases:
        in_specs.append(pl.BlockSpec((2, 2, BAND, 2 * BAND), lambda b, p: (p, 0, 0, 0)))
        args.append(bias)
    scratch = [pltpu.VMEM((len(dils), 3, S, LANES), F32)]
    return pl.pallas_call(
        functools.partial(_dilated_mix_kernel, dils=dils, seq=S),
        grid=(B, 2),
        in_specs=in_specs,
        out_specs=pl.BlockSpec((None, S, LANES), lambda b, p: (b, 0, p)),
        out_shape=jax.ShapeDtypeStruct((B, S, GROUP_W), F32),
        scratch_shapes=scratch,
        compiler_params=_params(("arbitrary",) * 2),
        name="dilated_mix",
    )(*args)


def _swa(src, bias, sinks):
    B, dil, L, _ = src.shape
    n_tiles = L // BAND
    assert n_tiles & (n_tiles - 1) == 0 and (dil * n_tiles) % BANDED_UNROLL == 0
    col = lambda blk: (lambda b, p: (b, 0, 0, blk + p))
    o_spec = pl.BlockSpec((None, dil, L, LANES), lambda b, p: (b, 0, 0, p))
    o_shape = jax.ShapeDtypeStruct((B, dil, L, GROUP_W), F32)
    return pl.pallas_call(
        functools.partial(_banded_kernel, dil=dil, n_tiles=n_tiles),
        grid=(B, 2),
        in_specs=[pl.BlockSpec((None, dil, L, LANES), col(QC_B)),
                  pl.BlockSpec((None, dil, L, LANES), col(KC_B)),
                  pl.BlockSpec((None, dil, L, LANES), col(VC_B)),
                  pl.BlockSpec((2, 2, BAND, 2 * BAND), lambda b, p: (p, 0, 0, 0)),
                  pl.BlockSpec(memory_space=pltpu.SMEM)],
        out_specs=(o_spec,) * 2,
        out_shape=(o_shape,) * 2,
        compiler_params=_params(("arbitrary",) * 2),
        name="swa",
    )(src, src, src, bias, sinks)


def _online_update(scores, vts, carry):
    n_heads = len(scores)
    ms = []
    for hh in range(n_heads):
        m_new = carry[2 * hh]
        for s in scores[hh]:
            m_new = jnp.maximum(m_new, jnp.max(s, axis=1, keepdims=True))
        ms.append(m_new)
    new = []
    for hh in range(n_heads):
        m, acc = carry[2 * hh:2 * hh + 2]
        ps = [jnp.exp2(s - ms[hh]).astype(BF16) for s in scores[hh]]
        p = ps[0] if len(ps) == 1 else jnp.concatenate(ps, axis=1)
        acc = jnp.exp2(m - ms[hh]) * acc + _dot(p, vts[hh])
        new.extend((ms[hh], acc))
    return tuple(new)


def _flash_init(tq, n_heads=2):
    return (jnp.full((tq, 1), NEG, F32), jnp.zeros((tq, LANES), F32)) * n_heads


def _finish_pair(o_ref, carry, tq):
    low = lax.broadcasted_iota(jnp.int32, (tq, LANES), 1) < HEAD_DIM
    outs = [acc / acc[:, AUG0:AUG0 + 1] for acc in (carry[1], carry[3])]
    o_ref[...] = jnp.where(low, outs[0], pltpu.roll(outs[1], HEAD_DIM, axis=1))


def _moba_kernel(q_ref, k_ref, v_ref, bias_ref, shift_ref, o_ref, qaug_ref, *, n_blocks):
    i = pl.program_id(2)
    t, blk = FLASH_T, MOBA_BLOCK
    per = t // blk
    seq = n_blocks * blk

    @pl.when(i == 0)
    def _():
        blk_id = lax.broadcasted_iota(jnp.int32, (n_blocks, seq), 0)
        blk_f = blk_id.astype(F32)
        own = lax.broadcasted_iota(jnp.int32, (n_blocks, seq), 1) // blk
        for hh in range(2):
            q = q_ref[:, hh * LANES:(hh + 1) * LANES]
            kf = k_ref[:, hh * LANES:(hh + 1) * LANES].astype(F32)
            km = jnp.mean(kf.reshape(n_blocks, blk, LANES), axis=1)
            k1 = km.astype(BF16)
            r1 = km - k1.astype(F32)
            k2 = r1.astype(BF16)
            k3 = (r1 - k2.astype(F32)).astype(BF16)
            gate = _dot_nt(k1, q) + _dot_nt(k2, q) + _dot_nt(k3, q)

            avail = blk_id < own
            sel = jnp.zeros((n_blocks, seq), jnp.bool_)
            for _ in range(MOBA_TOPK):
                cur = jnp.where(avail, gate, -jnp.inf)
                top = jnp.max(cur, axis=0, keepdims=True)
                is_top = avail & (cur == top)
                first = jnp.min(jnp.where(is_top, blk_f, float(n_blocks)), axis=0, keepdims=True)
                pick = blk_f == first
                sel = sel | pick
                avail = avail & jnp.logical_not(pick)
            penalty = jnp.where(sel | (blk_id == own), 0.0, NEG).astype(BF16)
            placed = lax.dot_general(penalty, shift_ref[0:n_blocks, :], (((0,), (0,)), ((), ())),
                                     preferred_element_type=F32)
            qaug_ref[:, hh * LANES:(hh + 1) * LANES] = (q.astype(F32) + placed).astype(BF16)

    q0 = pl.multiple_of(i * t, t)
    q_aug = [qaug_ref[pl.ds(q0, t), hh * LANES:(hh + 1) * LANES] for hh in range(2)]

    def tile(n, q_rows, k_lo, k_hi, carry_rows):
        r0 = pl.multiple_of(n * t + k_lo * blk, blk)
        width = (k_hi - k_lo) * blk
        d0 = per * (i - n) + per - 1
        scores, vts = [], []
        for hh in range(2):
            kt = k_ref[pl.ds(r0, width), hh * LANES:(hh + 1) * LANES]
            s = _dot_nt(q_aug[hh][q_rows[0] * blk:q_rows[1] * blk], kt)
            rows = [jnp.concatenate([s[(a - q_rows[0]) * blk:(a - q_rows[0] + 1) * blk,
                                       (b - k_lo) * blk:(b - k_lo + 1) * blk] + bias_ref[hh, d0 + a - b]
                                     for b in range(k_lo, k_hi)], axis=1) for a in range(*q_rows)]
            scores.append([jnp.concatenate(rows, axis=0)])
            vts.append(v_ref[pl.ds(r0, width), hh * LANES:(hh + 1) * LANES])
        return _online_update(scores, vts, carry_rows)

    carry = lax.fori_loop(0, i, lambda n, c: tile(n, (0, per), 0, per, c), _flash_init(t))
    hp = per // 2
    carry = tile(i, (0, per), 0, hp, carry)
    low_rows = tile(i, (hp, per), hp, per, tuple(c[hp * blk:] for c in carry))
    carry = tuple(jnp.concatenate([c[:hp * blk], lo], axis=0) for c, lo in zip(carry, low_rows))
    _finish_pair(o_ref, carry, t)


def _moba(qb, kb, vb, bias, shift):
    B, S, _ = qb.shape
    nb = S // MOBA_BLOCK
    t = FLASH_T
    assert nb <= AUG0 and t % (2 * MOBA_BLOCK) == 0 and S % t == 0
    n_bias = nb + t // MOBA_BLOCK - 1
    return pl.pallas_call(
        functools.partial(_moba_kernel, n_blocks=nb),
        grid=(B, 2, S // t),
        in_specs=[pl.BlockSpec((None, S, 2 * LANES), lambda b, p, i: (b, 0, p)),
                  pl.BlockSpec((None, S, 2 * LANES), lambda b, p, i: (b, 0, p)),
                  pl.BlockSpec((None, S, 2 * LANES), lambda b, p, i: (b, 0, p)),
                  pl.BlockSpec((2, n_bias, MOBA_BLOCK, MOBA_BLOCK), lambda b, p, i: (p, 0, 0, 0)),
                  pl.BlockSpec((LANES, LANES), lambda b, p, i: (0, 0))],
        out_specs=pl.BlockSpec((None, t, LANES), lambda b, p, i: (b, i, p)),
        out_shape=jax.ShapeDtypeStruct((B, S, GROUP_W), F32),
        scratch_shapes=[pltpu.VMEM((S, 2 * LANES), BF16)],
        compiler_params=_params(("arbitrary",) * 3),
        name="moba",
    )(qb, kb, vb, bias, shift)


def _fox_kernel(q_ref, k_ref, v_ref, o_ref, *, tq, tk):
    i = pl.program_id(2)
    qs = [q_ref[:, hh * LANES:(hh + 1) * LANES] for hh in range(2)]

    def step(n, carry):
        r0 = pl.multiple_of(n * tk, tk)
        scores, vts = [], []
        for hh in range(2):
            kt = k_ref[pl.ds(r0, tk), hh * LANES:(hh + 1) * LANES]
            scores.append([_dot_nt(qs[hh], kt)])
            vts.append(v_ref[pl.ds(r0, tk), hh * LANES:(hh + 1) * LANES])
        return _online_update(scores, vts, carry)

    carry = lax.fori_loop(0, i, step, _flash_init(tq))
    half = tq // 2
    d0 = pl.multiple_of(i * tq, tq)

    def diag(q_lo, k0, carry_rows):
        rows = tq - q_lo
        row = lax.broadcasted_iota(jnp.int32, (rows, half), 0)
        col = lax.broadcasted_iota(jnp.int32, (rows, half), 1)
        scores, vts = [], []
        for hh in range(2):
            kt = k_ref[pl.ds(k0, half), hh * LANES:(hh + 1) * LANES]
            s = jnp.where(col <= row, _dot_nt(q_ref[q_lo:tq, hh * LANES:(hh + 1) * LANES], kt), NEG)
            scores.append([s])
            vts.append(v_ref[pl.ds(k0, half), hh * LANES:(hh + 1) * LANES])
        return _online_update(scores, vts, carry_rows)

    carry = diag(0, d0, carry)
    low_rows = diag(half, d0 + half, tuple(c[half:] for c in carry))
    carry = tuple(jnp.concatenate([c[:half], lo], axis=0) for c, lo in zip(carry, low_rows))
    _finish_pair(o_ref, carry, tq)


def _fox(qd, kd, vd):
    B, S, _ = qd.shape
    tq, tk = FOX_TQ, FOX_TK
    assert S % tk == 0 and tk == tq
    return pl.pallas_call(
        functools.partial(_fox_kernel, tq=tq, tk=tk),
        grid=(B, 2, S // tq),
        in_specs=[pl.BlockSpec((None, tq, 2 * LANES), lambda b, p, i: (b, i, p)),
                  pl.BlockSpec((None, S, 2 * LANES), lambda b, p, i: (b, 0, p)),
                  pl.BlockSpec((None, S, 2 * LANES), lambda b, p, i: (b, 0, p))],
        out_specs=pl.BlockSpec((None, tq, LANES), lambda b, p, i: (b, i, p)),
        out_shape=jax.ShapeDtypeStruct((B, S, GROUP_W), F32),
        compiler_params=_params(("arbitrary",) * 3),
        name="fox",
    )(qd, kd, vd)


def _outffn_kernel(x_ref, oa_ref, ob_ref, oc_ref, lc_ref, od_ref,
                   mg_ref, wo_ref, g2_ref, wg_ref, wu_ref, wd_ref, fg_ref, out_ref, *, final):
    mix_a = oa_ref[...]
    mix_c = oc_ref[...] / lc_ref[...]

    x1 = x_ref[...]
    for g, part in enumerate((mix_a, ob_ref[...], mix_c, od_ref[...])):
        c0 = g * GROUP_W
        normed = _rms(part, mg_ref[:, c0:c0 + GROUP_W]).astype(BF16)
        x1 = x1 + _dot(normed, wo_ref[c0:c0 + GROUP_W, :])

    h = _rms(x1, g2_ref[...]).astype(BF16)
    ffn = jnp.zeros_like(x1)
    for c in range(D_FF // FF_CHUNK):
        c0 = c * FF_CHUNK
        gate = _dot(h, wg_ref[:, c0:c0 + FF_CHUNK])
        up = _dot(h, wu_ref[:, c0:c0 + FF_CHUNK])
        act = (gate * (1.0 / (1.0 + jnp.exp(-gate))) * up).astype(BF16)
        ffn = ffn + _dot(act, wd_ref[c0:c0 + FF_CHUNK, :])
    x2 = x1 + ffn
    if final:
        x2 = _rms(x2, fg_ref[...])
    out_ref[...] = x2


def _outffn(x, oa, ob, c_parts, od, mg, wo, g2, wg, wu, wd, fg, *, layer, final):
    B, S, D = x.shape
    tm = OUT_TM
    tile = lambda b, t: (b, t, 0)
    const = lambda b, t: (0, 0)
    of_layer = lambda b, t: (layer, 0, 0)
    flat_spec = pl.BlockSpec((None, tm, GROUP_W), tile)
    c_spec = pl.BlockSpec((None, None, tm, GROUP_W), lambda b, t: (b, 0, t, 0))
    return pl.pallas_call(
        functools.partial(_outffn_kernel, final=final),
        grid=(B, S // tm),
        in_specs=[pl.BlockSpec((None, tm, D), tile), flat_spec, flat_spec, c_spec, c_spec, flat_spec,
                  _resident((None, 1, D), of_layer),
                  _resident((None, D, D), of_layer),
                  _resident((None, 1, D), of_layer),
                  _resident((None, D, D_FF), of_layer),
                  _resident((None, D, D_FF), of_layer),
                  _resident((None, D_FF, D), of_layer),
                  _resident((1, D), const)],
        out_specs=pl.BlockSpec((None, tm, D), tile),
        out_shape=jax.ShapeDtypeStruct((B, S, D), F32),
        compiler_params=_params(("arbitrary", "arbitrary")),
        name="outffn",
    )(x, oa, ob, c_parts[0], c_parts[1], od, mg, wo, g2, wg, wu, wd, fg)


def kernel(x, norm1_g, w_in, f_bias, sinks, mix_norm_g, w_out, norm2_g, w_gate, w_up, w_down, rel_bias, final_g):
    B, S, D = x.shape
    depth = w_in.shape[0]
    max_dil = max(dil for _, dil in DILATED_GROUPS)
    assert D == D_MODEL and S % (2 * BAND * max_dil) == 0 and w_gate.shape[-1] == D_FF
    assert [dil for _, dil in DILATED_GROUPS] == [1, 4, 16] and S % PROJ_TM == 0

    w_proj = _rearrange_w_in(w_in)
    wo_b, wg_b, wu_b, wd_b = (w.astype(BF16) for w in (w_out, w_gate, w_up, w_down))
    fb = jnp.pad(f_bias, ((0, 0), (0, LANES - N_HEADS)))[:, None, :]
    g1, mg, g2 = norm1_g[:, None, :], mix_norm_g[:, None, :], norm2_g[:, None, :]
    consts = _placement_constants()
    table_a = rel_bias[:, :N_HEADS]
    table_b = rel_bias[:, N_HEADS:2 * N_HEADS]
    table_c = rel_bias[:, 2 * N_HEADS:]
    bias_a = [_band_bias(table_a, dil, window // dil, f"dilated{dil}_bias") for window, dil in DILATED_GROUPS]
    bias_c = _band_bias(table_c, 1, SWA_WINDOW - 1, "swa_bias")
    bias_b = _moba_bias(table_b, S // MOBA_BLOCK)

    for l in range(depth):
        y, ya4, ya16, qb, kb, vb, qd, kd, vd = _inproj(x, g1, w_proj, fb, consts, l)
        y4 = y.reshape(B, 1, S, NY)
        o_a = _dilated_mix((y4, ya4, ya16), bias_a)
        c_parts = _swa(y4, bias_c, sinks[l])
        o_b = _moba(qb, kb, vb, bias_b, consts[4])
        o_d = _fox(qd, kd, vd)
        x = _outffn(x, o_a, o_b, c_parts, o_d, mg, wo_b, g2, wg_b, wu_b, wd_b, final_g[None],
                    layer=l, final=(l == depth - 1))
    return x
```
